```python
import math
import jax, jax.numpy as jnp
from jax import lax
import numpy as np

D_MODEL = 1024
BATCH = 4
SEQ = 8192
DEPTH = 1
DEC_BATCH = 128
DEC_SEQ = 8
PAST_LEN = 16384
PAGE_SIZE = 128

N_HEADS = 8
N_KV_HEADS = 2
REP = N_HEADS // N_KV_HEADS
HEAD_DIM = 64
ATTN_WIDTH = N_HEADS * HEAD_DIM
KV_WIDTH = N_KV_HEADS * HEAD_DIM
WINDOW = 128
ATTN_BLOCK = WINDOW
ROT_DIM = HEAD_DIM // 4
ROPE_THETA = 500000.0
NEG_INF = -1e30
SSM_WIDTH = D_MODEL - ATTN_WIDTH
SSM_GROUP = 16
N_SSM_GROUPS = SSM_WIDTH // SSM_GROUP
SSM_STATE = 64
IN_WIDTH = ATTN_WIDTH + 2 * KV_WIDTH + SSM_WIDTH
PEER_HEADS = 8
N_KEYS = 128
N_EXPERTS = N_KEYS * N_KEYS
PEER_TOPK = 16
PEER_QDIM = 256
PEER_HALF = PEER_QDIM // 2
PEER_CHUNK = 128
RMS_EPS = 1e-5

kernel_name = 'hymba_swa_s5_peer_step'


def rmsnorm(x, g):
    xf = x.astype(jnp.float32)
    y = xf * lax.rsqrt(jnp.mean(xf * xf, axis=-1, keepdims=True) + RMS_EPS)
    return (y * g.astype(jnp.float32)).astype(x.dtype)


def partial_rope(x, pos):
    half = ROT_DIM // 2
    inv_freq = ROPE_THETA ** (-jnp.arange(half, dtype=jnp.float32) * 2.0 / ROT_DIM)
    ang = pos.astype(jnp.float32)[:, None] * inv_freq[None, :]
    cos = jnp.cos(ang)[:, None, :]
    sin = jnp.sin(ang)[:, None, :]
    xf = x.astype(jnp.float32)
    x1 = xf[..., :half]
    x2 = xf[..., half:ROT_DIM]
    out = jnp.concatenate([x1 * cos - x2 * sin, x2 * cos + x1 * sin, xf[..., ROT_DIM:]], axis=-1)
    return out.astype(x.dtype)


def window_attention(q, k, v, q_pos, k_pos, sinks):
    s = jnp.einsum('bnqgrd,bnkgd->bngrqk', q.astype(jnp.float32), k.astype(jnp.float32)) * (HEAD_DIM ** -0.5)
    diff = q_pos[:, :, None] - k_pos[:, None, :]
    mask = (diff >= 0) & (diff < WINDOW) & (k_pos[:, None, :] >= 0)
    s = jnp.where(mask[None, :, None, None], s, NEG_INF)
    sink = sinks.astype(jnp.float32).reshape(N_KV_HEADS, REP)[None, None, :, :, None, None]
    m = jnp.maximum(jnp.max(s, axis=-1, keepdims=True), sink)
    p = jnp.exp(s - m)
    w = p / (jnp.sum(p, axis=-1, keepdims=True) + jnp.exp(sink - m))
    return jnp.einsum('bngrqk,bnkgd->bnqgrd', w, v.astype(jnp.float32))


def _complex_affine_combine(e1, e2):
    a1r, a1i, b1r, b1i = e1
    a2r, a2i, b2r, b2i = e2
    return (a2r * a1r - a2i * a1i,
            a2r * a1i + a2i * a1r,
            a2r * b1r - a2i * b1i + b2r,
            a2r * b1i + a2i * b1r + b2i)


def s5_mixer(u, h0_re, h0_im, lam_re, lam_im, log_step, b_re, b_im, c_re, c_im, d_skip, w_glu, b_glu):
    f32 = jnp.float32
    bsz, t = u.shape[:2]
    uf = u.astype(f32).reshape(bsz, t, N_SSM_GROUPS, SSM_GROUP)
    lr = lam_re.astype(f32)
    li = lam_im.astype(f32)
    dt = jnp.exp(log_step.astype(f32))[:, None]
    mag = jnp.exp(lr * dt)
    ab_re = mag * jnp.cos(li * dt)
    ab_im = mag * jnp.sin(li * dt)
    den = lr * lr + li * li
    z_re = ((ab_re - 1.0) * lr + ab_im * li) / den
    z_im = (ab_im * lr - (ab_re - 1.0) * li) / den
    br = b_re.astype(f32)
    bi = b_im.astype(f32)
    bb_re = z_re[..., None] * br - z_im[..., None] * bi
    bb_im = z_re[..., None] * bi + z_im[..., None] * br
    bu_re = jnp.einsum('btgp,gnp->btgn', uf, bb_re)
    bu_im = jnp.einsum('btgp,gnp->btgn', uf, bb_im)
    h0r = h0_re.astype(f32)
    h0i = h0_im.astype(f32)
    bu_re = bu_re.at[:, 0].add(ab_re * h0r - ab_im * h0i)
    bu_im = bu_im.at[:, 0].add(ab_re * h0i + ab_im * h0r)
    a_re = jnp.broadcast_to(ab_re, bu_re.shape)
    a_im = jnp.broadcast_to(ab_im, bu_im.shape)
    _, _, h_re, h_im = lax.associative_scan(_complex_affine_combine, (a_re, a_im, bu_re, bu_im), axis=1)
    y = (jnp.einsum('btgn,gpn->btgp', h_re, c_re.astype(f32))
         - jnp.einsum('btgn,gpn->btgp', h_im, c_im.astype(f32))
         + d_skip.astype(f32).reshape(N_SSM_GROUPS, SSM_GROUP) * uf)
    y = jax.nn.gelu(y.reshape(bsz, t, SSM_WIDTH), approximate=False)
    y = y * jax.nn.sigmoid(y @ w_glu.astype(f32) + b_glu.astype(f32))
    return y.astype(u.dtype), h_re[:, -1], h_im[:, -1]


def peer_ffn(x, w_query, sub_keys, u_table, v_table):
    f32 = jnp.float32
    n_tok = x.shape[0]
    pad = (-n_tok) % PEER_CHUNK
    xb = jnp.pad(x, ((0, pad), (0, 0))).reshape(-1, PEER_CHUNK, D_MODEL)
    keys = sub_keys.astype(f32)

    def chunk(xc):
        q = (xc @ w_query).astype(f32).reshape(PEER_CHUNK, PEER_HEADS, 2, PEER_HALF)
        s = jnp.einsum('thcd,hckd->thck', q, keys)
        top_s, top_i = lax.top_k(s, PEER_TOPK)
        cand_s = (top_s[:, :, 0, :, None] + top_s[:, :, 1, None, :]).reshape(PEER_CHUNK, PEER_HEADS, PEER_TOPK * PEER_TOPK)
        cand_i = (top_i[:, :, 0, :, None] * N_KEYS + top_i[:, :, 1, None, :]).reshape(PEER_CHUNK, PEER_HEADS, PEER_TOPK * PEER_TOPK)
        best_s, best_j = lax.top_k(cand_s, PEER_TOPK)
        idx = jnp.take_along_axis(cand_i, best_j, axis=-1)
        gate = jax.nn.softmax(best_s, axis=-1)
        act = jax.nn.gelu(jnp.einsum('thkd,td->thk', u_table[idx].astype(f32), xc.astype(f32)), approximate=False)
        return jnp.einsum('thk,thkd->td', gate * act, v_table[idx].astype(f32)).astype(xc.dtype)

    y = lax.map(chunk, xb)
    return y.reshape(-1, D_MODEL)[:n_tok]


def layer(x, k_buf, v_buf, h0_re, h0_im, norm_mix, w_in, b_in, attn_sinks, lam_re, lam_im, log_step,
          b_re, b_im, c_re, c_im, d_skip, w_glu, b_glu, norm_attn_out, norm_ssm_out, w_out,
          norm_ffn, w_query, sub_keys, u_table, v_table):
    bsz, t, _ = x.shape
    xn = rmsnorm(x, norm_mix)
    proj = xn @ w_in + b_in
    q, k, v, u = jnp.split(proj, [ATTN_WIDTH, ATTN_WIDTH + KV_WIDTH, ATTN_WIDTH + 2 * KV_WIDTH], axis=-1)
    q = q.reshape(bsz, t, N_HEADS, HEAD_DIM)
    k = k.reshape(bsz, t, N_KV_HEADS, HEAD_DIM)
    v = v.reshape(bsz, t, N_KV_HEADS, HEAD_DIM)
    if k_buf is None:
        pos = jnp.arange(t, dtype=jnp.int32)
        q = partial_rope(q, pos)
        k = partial_rope(k, pos)
        nb = t // ATTN_BLOCK
        qb = q.reshape(bsz, nb, ATTN_BLOCK, N_KV_HEADS, REP, HEAD_DIM)
        kb = k.reshape(bsz, nb, ATTN_BLOCK, N_KV_HEADS, HEAD_DIM)
        vb = v.reshape(bsz, nb, ATTN_BLOCK, N_KV_HEADS, HEAD_DIM)
        pad_blk = ((0, 0), (1, 0), (0, 0), (0, 0), (0, 0))
        kk = jnp.concatenate([jnp.pad(kb, pad_blk)[:, :-1], kb], axis=2)
        vv = jnp.concatenate([jnp.pad(vb, pad_blk)[:, :-1], vb], axis=2)
        qpos = pos.reshape(nb, ATTN_BLOCK)
        kpos = jnp.concatenate([qpos - ATTN_BLOCK, qpos], axis=1)
        attn = window_attention(qb, kk, vv, qpos, kpos, attn_sinks)
        k_win = k[:, -WINDOW:]
        v_win = v[:, -WINDOW:]
    else:
        pos = PAST_LEN + jnp.arange(t, dtype=jnp.int32)
        q = partial_rope(q, pos)
        k = partial_rope(k, pos)
        kk = jnp.concatenate([k_buf.astype(k.dtype), k], axis=1)
        vv = jnp.concatenate([v_buf.astype(v.dtype), v], axis=1)
        kpos = jnp.concatenate([PAST_LEN - WINDOW + jnp.arange(WINDOW, dtype=jnp.int32), pos])
        attn = window_attention(q.reshape(bsz, 1, t, N_KV_HEADS, REP, HEAD_DIM), kk[:, None], vv[:, None],
                                pos[None], kpos[None], attn_sinks)
        k_win = kk[:, -WINDOW:]
        v_win = vv[:, -WINDOW:]
    attn = attn.reshape(bsz, t, ATTN_WIDTH).astype(x.dtype)
    ssm, h_re, h_im = s5_mixer(u, h0_re, h0_im, lam_re, lam_im, log_step, b_re, b_im, c_re, c_im,
                               d_skip, w_glu, b_glu)
    mix = jnp.concatenate([rmsnorm(attn, norm_attn_out), rmsnorm(ssm, norm_ssm_out)], axis=-1)
    x = x + mix @ w_out
    ffn = peer_ffn(rmsnorm(x, norm_ffn).reshape(-1, D_MODEL), w_query, sub_keys, u_table, v_table)
    x = x + ffn.reshape(x.shape)
    return x, k_win, v_win, h_re, h_im


def setup_inputs(seed: int = 0) -> dict:
    key = jax.random.key(seed)
    ks = jax.random.split(key, 32)
    f32 = jnp.float32
    L = DEPTH

    def nrm(k, shape, scale):
        return jax.random.normal(k, shape, f32) * scale

    n_idx = jnp.arange(SSM_STATE, dtype=f32)
    return {
        'x_prompt': nrm(ks[0], (BATCH, SEQ, D_MODEL), 1.0),
        'x_sample': nrm(ks[1], (DEC_BATCH, DEC_SEQ, D_MODEL), 1.0),
        'cache_k': nrm(ks[2], (L, DEC_BATCH, WINDOW, N_KV_HEADS, HEAD_DIM), 1.0),
        'cache_v': nrm(ks[3], (L, DEC_BATCH, WINDOW, N_KV_HEADS, HEAD_DIM), 1.0),
        'state_ssm_re': nrm(ks[4], (L, DEC_BATCH, N_SSM_GROUPS, SSM_STATE), 0.3),
        'state_ssm_im': nrm(ks[5], (L, DEC_BATCH, N_SSM_GROUPS, SSM_STATE), 0.3),
        'norm_mix': 1.0 + nrm(ks[6], (L, D_MODEL), 0.02),
        'w_in': nrm(ks[7], (L, D_MODEL, IN_WIDTH), D_MODEL ** -0.5),
        'b_in': nrm(ks[8], (L, IN_WIDTH), 0.02),
        'attn_sinks': nrm(ks[9], (L, N_HEADS), 1.0),
        'ssm_lam_re': -0.5 + nrm(ks[10], (L, N_SSM_GROUPS, SSM_STATE), 0.01),
        'ssm_lam_im': math.pi * n_idx + nrm(ks[11], (L, N_SSM_GROUPS, SSM_STATE), 0.01),
        'ssm_log_step': jax.random.uniform(ks[12], (L, N_SSM_GROUPS), f32, math.log(1e-3), math.log(1e-1)),
        'ssm_b_re': nrm(ks[13], (L, N_SSM_GROUPS, SSM_STATE, SSM_GROUP), (2 * SSM_GROUP) ** -0.5),
        'ssm_b_im': nrm(ks[14], (L, N_SSM_GROUPS, SSM_STATE, SSM_GROUP), (2 * SSM_GROUP) ** -0.5),
        'ssm_c_re': nrm(ks[15], (L, N_SSM_GROUPS, SSM_GROUP, SSM_STATE), (2 * SSM_STATE) ** -0.5),
        'ssm_c_im': nrm(ks[16], (L, N_SSM_GROUPS, SSM_GROUP, SSM_STATE), (2 * SSM_STATE) ** -0.5),
        'ssm_d': nrm(ks[17], (L, SSM_WIDTH), 1.0),
        'ssm_w_glu': nrm(ks[18], (L, SSM_WIDTH, SSM_WIDTH), SSM_WIDTH ** -0.5),
        'ssm_b_glu': nrm(ks[19], (L, SSM_WIDTH), 0.02),
        'norm_attn_out': 1.0 + nrm(ks[20], (L, ATTN_WIDTH), 0.02),
        'norm_ssm_out': 1.0 + nrm(ks[21], (L, SSM_WIDTH), 0.02),
        'w_out': nrm(ks[22], (L, D_MODEL, D_MODEL), D_MODEL ** -0.5),
        'norm_ffn': 1.0 + nrm(ks[23], (L, D_MODEL), 0.02),
        'peer_w_query': nrm(ks[24], (L, D_MODEL, PEER_HEADS * PEER_QDIM), D_MODEL ** -0.5),
        'peer_sub_keys': nrm(ks[25], (L, PEER_HEADS, 2, N_KEYS, PEER_HALF), PEER_HALF ** -0.5),
        'peer_u': nrm(ks[26], (L, N_EXPERTS, D_MODEL), D_MODEL ** -0.5),
        'peer_v': nrm(ks[27], (L, N_EXPERTS, D_MODEL), PEER_HEADS ** -0.5),
        'norm_final': 1.0 + nrm(ks[28], (D_MODEL,), 0.02),
    }


def reference(x_prompt, x_sample, cache_k, cache_v, state_ssm_re, state_ssm_im, norm_mix, w_in, b_in,
              attn_sinks, ssm_lam_re, ssm_lam_im, ssm_log_step, ssm_b_re, ssm_b_im, ssm_c_re, ssm_c_im,
              ssm_d, ssm_w_glu, ssm_b_glu, norm_attn_out, norm_ssm_out, w_out, norm_ffn, peer_w_query,
              peer_sub_keys, peer_u, peer_v, norm_final):
    xp = x_prompt
    xs = x_sample
    pk, pv, pr, pim, sk, sv, sr, sim = [], [], [], [], [], [], [], []
    for l in range(DEPTH):
        lp = [w[l] for w in (norm_mix, w_in, b_in, attn_sinks, ssm_lam_re, ssm_lam_im, ssm_log_step,
                             ssm_b_re, ssm_b_im, ssm_c_re, ssm_c_im, ssm_d, ssm_w_glu, ssm_b_glu,
                             norm_attn_out, norm_ssm_out, w_out, norm_ffn, peer_w_query, peer_sub_keys,
                             peer_u, peer_v)]
        h_zero = jnp.zeros((xp.shape[0], N_SSM_GROUPS, SSM_STATE), jnp.float32)
        xp, k1, v1, r1, i1 = layer(xp, None, None, h_zero, h_zero, *lp)
        xs, k2, v2, r2, i2 = layer(xs, cache_k[l], cache_v[l], state_ssm_re[l], state_ssm_im[l], *lp)
        pk.append(k1); pv.append(v1); pr.append(r1); pim.append(i1)
        sk.append(k2); sv.append(v2); sr.append(r2); sim.append(i2)
    y_prompt = rmsnorm(xp, norm_final)
    y_sample = rmsnorm(xs, norm_final)
    return (y_prompt, y_sample, jnp.stack(pk), jnp.stack(pv), jnp.stack(pr), jnp.stack(pim),
            jnp.stack(sk), jnp.stack(sv), jnp.stack(sr), jnp.stack(sim))
```

```python
import functools
import math

import jax
import jax.numpy as jnp
from jax import lax
from jax.experimental import pallas as pl
from jax.experimental.pallas import tpu as pltpu

F32 = jnp.float32
BF16 = jnp.bfloat16
I32 = jnp.int32

D_MODEL = 1024
N_HEADS = 8
N_KV_HEADS = 2
HEAD_DIM = 64
ATTN_WIDTH = N_HEADS * HEAD_DIM
KV_WIDTH = N_KV_HEADS * HEAD_DIM
WINDOW = 128
ROT_DIM = HEAD_DIM // 4
ROPE_THETA = 500000.0
NEG_INF = -1e30
SSM_WIDTH = D_MODEL - ATTN_WIDTH
SSM_GROUP = 16
N_SSM_GROUPS = SSM_WIDTH // SSM_GROUP
SSM_STATE = 64
SSM_LANES = N_SSM_GROUPS * SSM_STATE
IN_WIDTH = ATTN_WIDTH + 2 * KV_WIDTH + SSM_WIDTH
PEER_HEADS = 8
N_KEYS = 128
PEER_TOPK = 16
PEER_QDIM = 256
PEER_HALF = PEER_QDIM // 2
N_PAIRS = PEER_HEADS * PEER_TOPK
RMS_EPS = 1e-5

LANES = 128
SUBLANES = 8
VMEM_LIMIT = 56 * 1024 * 1024

HI_MASK = -65536


def _cparams(sem):
    return pltpu.CompilerParams(dimension_semantics=sem, vmem_limit_bytes=VMEM_LIMIT)


def _split(w):
    hi = w.astype(BF16)
    lo = (w - hi.astype(F32)).astype(BF16)
    return hi, lo


def _dot1(a, b):
    return jnp.dot(a.astype(BF16), b, preferred_element_type=F32)


def _dot3(a, b_hi, b_lo):
    a_hi, a_lo = _split(a)
    d = functools.partial(jnp.dot, preferred_element_type=F32)
    return d(a_hi, b_hi) + (d(a_lo, b_hi) + d(a_hi, b_lo))


def _dot3_nt(a_hi, a_lo, b):
    b_hi, b_lo = _split(b)
    d = functools.partial(lax.dot_general, dimension_numbers=(((1,), (1,)), ((), ())),
                          preferred_element_type=F32)
    return d(a_hi, b_hi) + (d(a_lo, b_hi) + d(a_hi, b_lo))


def _gelu(x):
    return 0.5 * x * (1.0 + lax.erf(x * (2.0 ** -0.5)))


def _rms(x, g):
    return x * lax.rsqrt(jnp.mean(x * x, axis=-1, keepdims=True) + RMS_EPS) * g


def _inproj_kernel(x_ref, g_ref, whi_ref, wlo_ref, b_ref, c_ref, sa_ref, sb_ref,
                   q_ref, k_ref, v_ref, u_ref):
    xn = _rms(x_ref[...], g_ref[...])
    proj = _dot3(xn, whi_ref[...], wlo_ref[...]) + b_ref[...]
    c = c_ref[...]
    sa = sa_ref[...]
    sb = sb_ref[...]

    def rope(t):
        return t * c + pltpu.roll(t, LANES - ROT_DIM // 2, 1) * sa + pltpu.roll(t, ROT_DIM // 2, 1) * sb

    for j in range(ATTN_WIDTH // LANES):
        q_ref[:, j * LANES:(j + 1) * LANES] = rope(proj[:, j * LANES:(j + 1) * LANES])
    k_ref[...] = rope(proj[:, ATTN_WIDTH:ATTN_WIDTH + KV_WIDTH])
    v_ref[...] = proj[:, ATTN_WIDTH + KV_WIDTH:ATTN_WIDTH + 2 * KV_WIDTH]
    u_ref[...] = proj[:, ATTN_WIDTH + 2 * KV_WIDTH:]


def _rope_tables(pos):
    half = ROT_DIM // 2
    inv_freq = ROPE_THETA ** (-jnp.arange(half, dtype=F32) * 2.0 / ROT_DIM)
    ang = pos.astype(F32)[:, None] * inv_freq[None, :]
    cos = jnp.cos(ang)
    sin = jnp.sin(ang)
    t = pos.shape[0]
    one = jnp.ones((t, HEAD_DIM - ROT_DIM), F32)
    zero = jnp.zeros((t, HEAD_DIM - ROT_DIM), F32)
    zh = jnp.zeros((t, half), F32)
    c = jnp.concatenate([cos, cos, one], axis=1)
    sa = jnp.concatenate([-sin, zh, zero], axis=1)
    sb = jnp.concatenate([zh, sin, zero], axis=1)
    return tuple(jnp.tile(a, (1, LANES // HEAD_DIM)) for a in (c, sa, sb))


def _in_proj(x2d, pos_tables, n_pos_tiles, g, w_hi, w_lo, b, tm):
    n = x2d.shape[0]
    c, sa, sb = pos_tables
    row = lambda i: (i, 0)
    fixed = lambda i: (0, 0)
    pos_map = lambda i: (i % n_pos_tiles, 0)
    return pl.pallas_call(
        _inproj_kernel,
        grid=(n // tm,),
        in_specs=[pl.BlockSpec((tm, D_MODEL), row), pl.BlockSpec((1, D_MODEL), fixed),
                  pl.BlockSpec((D_MODEL, IN_WIDTH), fixed), pl.BlockSpec((D_MODEL, IN_WIDTH), fixed),
                  pl.BlockSpec((1, IN_WIDTH), fixed),
                  pl.BlockSpec((tm, LANES), pos_map), pl.BlockSpec((tm, LANES), pos_map),
                  pl.BlockSpec((tm, LANES), pos_map)],
        out_specs=[pl.BlockSpec((tm, ATTN_WIDTH), row), pl.BlockSpec((tm, KV_WIDTH), row),
                   pl.BlockSpec((tm, KV_WIDTH), row), pl.BlockSpec((tm, SSM_WIDTH), row)],
        out_shape=[jax.ShapeDtypeStruct((n, ATTN_WIDTH), F32), jax.ShapeDtypeStruct((n, KV_WIDTH), F32),
                   jax.ShapeDtypeStruct((n, KV_WIDTH), F32), jax.ShapeDtypeStruct((n, SSM_WIDTH), F32)],
        compiler_params=_cparams(("arbitrary",)),
        name="in_proj",
    )(x2d, g, w_hi, w_lo, b, c, sa, sb)


def _attn_kernel(sink_ref, q_ref, kp_ref, kc_ref, vp_ref, vc_ref, g_ref, o_ref, *win_refs, tq, decode):
    q = q_ref[0] * (HEAD_DIM ** -0.5)
    kp = kp_ref[0]
    kc = kc_ref[0]
    vp = vp_ref[0]
    vc = vc_ref[0]
    half = LANES // 2
    lane = lax.broadcasted_iota(I32, (1, LANES), 1)
    lo = lane < half
    qi = lax.broadcasted_iota(I32, (tq, WINDOW), 0)
    kj = lax.broadcasted_iota(I32, (tq, WINDOW), 1)
    first_off = 0 if decode else jnp.where(pl.program_id(1) > 0, 0, WINDOW)
    m_prev = kj > qi + first_off
    m_cur = kj <= qi

    def variants(t):
        r = pltpu.roll(t, half, 1)
        return [[t, r], [r, t]]

    kpv = [[a.astype(BF16) for a in row] for row in variants(kp)]
    kcv = [[a.astype(BF16) for a in row] for row in variants(kc)]
    lane_sel = [lo, jnp.logical_not(lo)]
    vpv = [[jnp.where(lane_sel[s], a, 0.0).astype(BF16) for s, a in enumerate(row)] for row in variants(vp)]
    vcv = [[jnp.where(lane_sel[s], a, 0.0).astype(BF16) for s, a in enumerate(row)] for row in variants(vc)]
    nt = functools.partial(lax.dot_general, dimension_numbers=(((1,), (1,)), ((), ())),
                           preferred_element_type=F32)
    outs = []
    for j in range(ATTN_WIDTH // LANES):
        g = (2 * j) // (N_HEADS // N_KV_HEADS)
        qt = q[:, j * LANES:(j + 1) * LANES]
        acc = jnp.zeros((tq, LANES), F32)
        for s in range(2):
            h = 2 * j + s
            qm = jnp.where(lane_sel[s], qt, 0.0).astype(BF16)
            sp = jnp.where(m_prev, nt(qm, kpv[g][s]), NEG_INF)
            sc = jnp.where(m_cur, nt(qm, kcv[g][s]), NEG_INF)
            sink = sink_ref[h]
            m = jnp.maximum(jnp.maximum(jnp.max(sp, axis=-1, keepdims=True),
                                        jnp.max(sc, axis=-1, keepdims=True)), sink)
            pp = jnp.exp(sp - m)
            pc = jnp.exp(sc - m)
            den = (jnp.sum(pp, axis=-1, keepdims=True) + jnp.sum(pc, axis=-1, keepdims=True)
                   + jnp.exp(sink - m))
            o = (jnp.dot(pp.astype(BF16), vpv[g][s], preferred_element_type=F32)
                 + jnp.dot(pc.astype(BF16), vcv[g][s], preferred_element_type=F32))
            acc = acc + o / den
        outs.append(acc)
    attn = jnp.concatenate(outs, axis=1)
    o_ref[0] = _rms(attn, g_ref[...])
    if decode:
        kw_ref, vw_ref = win_refs
        kw_ref[0, :WINDOW - tq] = kp[tq:]
        kw_ref[0, WINDOW - tq:] = kc[:tq]
        vw_ref[0, :WINDOW - tq] = vp[tq:]
        vw_ref[0, WINDOW - tq:] = vc[:tq]


def _attention(sinks, q3, k_prev, k_cur, v_prev, v_cur, g, decode):
    b, t, _ = q3.shape
    tq = t if decode else WINDOW
    nb = t // tq
    cur = lambda i, n: (i, n, 0)
    prev = (lambda i, n: (i, 0, 0)) if decode else (lambda i, n: (i, jnp.maximum(n - 1, 0), 0))
    kvb = (1, WINDOW, KV_WIDTH)
    out_specs = [pl.BlockSpec((1, tq, ATTN_WIDTH), cur)]
    out_shape = [jax.ShapeDtypeStruct((b, t, ATTN_WIDTH), F32)]
    if decode:
        out_specs += [pl.BlockSpec(kvb, cur), pl.BlockSpec(kvb, cur)]
        out_shape += [jax.ShapeDtypeStruct((b, WINDOW, KV_WIDTH), F32)] * 2
    return pl.pallas_call(
        functools.partial(_attn_kernel, tq=tq, decode=decode),
        grid=(b, nb),
        in_specs=[pl.BlockSpec(memory_space=pltpu.SMEM),
                  pl.BlockSpec((1, tq, ATTN_WIDTH), cur),
                  pl.BlockSpec(kvb, prev), pl.BlockSpec(kvb, cur),
                  pl.BlockSpec(kvb, prev), pl.BlockSpec(kvb, cur),
                  pl.BlockSpec((1, ATTN_WIDTH), lambda i, n: (0, 0))],
        out_specs=out_specs,
        out_shape=out_shape,
        compiler_params=_cparams(("arbitrary", "arbitrary")),
        name="attn_decode" if decode else "attn_prompt",
    )(sinks, q3, k_prev, k_cur, v_prev, v_cur, g)


def _s5_param_kernel(lr_ref, li_ref, ls_ref, abr_ref, abi_ref, zr_ref, zi_ref):
    lr = lr_ref[...]
    li = li_ref[...]
    dt = jnp.exp(ls_ref[...])
    mag = jnp.exp(lr * dt)
    ab_re = mag * jnp.cos(li * dt)
    ab_im = mag * jnp.sin(li * dt)
    den = lr * lr + li * li
    abr_ref[...] = ab_re
    abi_ref[...] = ab_im
    zr_ref[...] = ((ab_re - 1.0) * lr + ab_im * li) / den
    zi_ref[...] = (ab_im * lr - (ab_re - 1.0) * li) / den


def _s5_params(lam_re, lam_im, log_step):
    ls = jnp.broadcast_to(log_step[:, None], lam_re.shape)
    outs = pl.pallas_call(
        _s5_param_kernel,
        out_shape=[jax.ShapeDtypeStruct(lam_re.shape, F32)] * 4,
        name="s5_params",
    )(lam_re, lam_im, ls)
    return [o.reshape(1, SSM_LANES) for o in outs]


def _s5_kernel(u_ref, h0r_ref, h0i_ref, brh_ref, brl_ref, bih_ref, bil_ref, zr_ref, zi_ref,
               ar_ref, ai_ref, cr_ref, ci_ref, d_ref, wg_ref, bg_ref, g_ref,
               y_ref, hr_ref, hi_ref, sr_ref, si_ref, *, tt, ns):
    @pl.when(pl.program_id(1) == 0)
    def _():
        hr_ref[0] = h0r_ref[0]
        hi_ref[0] = h0i_ref[0]

    u = u_ref[0]
    pr = _dot3(u, brh_ref[...], brl_ref[...])
    pi = _dot3(u, bih_ref[...], bil_ref[...])
    zr = zr_ref[...]
    zi = zi_ref[...]
    sr_ref[...] = zr * pr - zi * pi
    si_ref[...] = zr * pi + zi * pr
    ar = jnp.broadcast_to(ar_ref[...], (ns, SSM_LANES))
    ai = jnp.broadcast_to(ai_ref[...], (ns, SSM_LANES))

    if ns <= SUBLANES:
        def body(t, carry):
            hr, hi = carry
            rows = pl.ds(t * ns, ns)
            nr = ar * hr - ai * hi + sr_ref[rows, :]
            ni = ar * hi + ai * hr + si_ref[rows, :]
            sr_ref[rows, :] = nr
            si_ref[rows, :] = ni
            return nr, ni

        hr, hi = lax.fori_loop(0, tt, body, (hr_ref[0], hi_ref[0]), unroll=8)
        hr_ref[0] = hr
        hi_ref[0] = hi
    else:
        def body(t, carry):
            rows = pl.ds(pl.multiple_of(t * ns, ns), ns)
            hr = hr_ref[0]
            hi = hi_ref[0]
            nr = ar * hr - ai * hi + sr_ref[rows, :]
            ni = ar * hi + ai * hr + si_ref[rows, :]
            sr_ref[rows, :] = nr
            si_ref[rows, :] = ni
            hr_ref[0] = nr
            hi_ref[0] = ni
            return carry

        lax.fori_loop(0, tt, body, 0)

    y = _dot1(sr_ref[...], cr_ref[...]) - _dot1(si_ref[...], ci_ref[...]) + d_ref[...] * u
    y = _gelu(y)
    gate = _dot1(y, wg_ref[...]) + bg_ref[...]
    y = y * (1.0 / (1.0 + jnp.exp(-gate)))
    y_ref[0] = _rms(y, g_ref[...])


def _s5(u3, h0r, h0i, consts, tt, ns):
    nb, rows, _ = u3.shape
    r = tt * ns
    fixed = lambda b, t: (0, 0)
    tile = lambda b, t: (b, t, 0)
    seq = lambda b, t: (b, 0, 0)
    const_specs = [pl.BlockSpec(c.shape, fixed) for c in consts]
    return pl.pallas_call(
        functools.partial(_s5_kernel, tt=tt, ns=ns),
        grid=(nb, rows // r),
        in_specs=[pl.BlockSpec((1, r, SSM_WIDTH), tile),
                  pl.BlockSpec((1, ns, SSM_LANES), seq), pl.BlockSpec((1, ns, SSM_LANES), seq)] + const_specs,
        out_specs=[pl.BlockSpec((1, r, SSM_WIDTH), tile),
                   pl.BlockSpec((1, ns, SSM_LANES), seq), pl.BlockSpec((1, ns, SSM_LANES), seq)],
        out_shape=[jax.ShapeDtypeStruct((nb, rows, SSM_WIDTH), F32),
                   jax.ShapeDtypeStruct((nb, ns, SSM_LANES), F32),
                   jax.ShapeDtypeStruct((nb, ns, SSM_LANES), F32)],
        scratch_shapes=[pltpu.VMEM((r, SSM_LANES), F32), pltpu.VMEM((r, SSM_LANES), F32)],
        compiler_params=_cparams(("arbitrary", "arbitrary")),
        name="s5_ns%d" % ns,
    )(u3, h0r, h0i, *consts)


def _outproj_kernel(a_ref, s_ref, x_ref, wa_ref, ws_ref, g_ref, x1_ref, xn_ref):
    x1 = x_ref[...] + _dot1(a_ref[...], wa_ref[...]) + _dot1(s_ref[...], ws_ref[...])
    x1_ref[...] = x1
    xn_ref[...] = _rms(x1, g_ref[...])


def _out_proj(attn_n, ssm_n, x2d, w_attn, w_ssm, g, tm):
    n = x2d.shape[0]
    row = lambda i: (i, 0)
    fixed = lambda i: (0, 0)
    return pl.pallas_call(
        _outproj_kernel,
        grid=(n // tm,),
        in_specs=[pl.BlockSpec((tm, ATTN_WIDTH), row), pl.BlockSpec((tm, SSM_WIDTH), row),
                  pl.BlockSpec((tm, D_MODEL), row),
                  pl.BlockSpec((ATTN_WIDTH, D_MODEL), fixed), pl.BlockSpec((SSM_WIDTH, D_MODEL), fixed),
                  pl.BlockSpec((1, D_MODEL), fixed)],
        out_specs=[pl.BlockSpec((tm, D_MODEL), row), pl.BlockSpec((tm, D_MODEL), row)],
        out_shape=[jax.ShapeDtypeStruct((n, D_MODEL), F32)] * 2,
        compiler_params=_cparams(("arbitrary",)),
        name="out_proj",
    )(attn_n, ssm_n, x2d, w_attn, w_ssm, g)


def _topk_rows(s, ids, k):
    vals, picks = [], []
    sentinel = jnp.iinfo(jnp.int32).max
    for _ in range(k):
        m = jnp.max(s, axis=0, keepdims=True)
        pick = jnp.min(jnp.where(s == m, ids, sentinel), axis=0, keepdims=True)
        vals.append(m)
        picks.append(pick)
        s = jnp.where(ids == pick, -jnp.inf, s)
    return vals, picks


def _cand_layout():
    pieces = [(0, 1, 0, PEER_TOPK)]
    pieces += [(a, a + 1, 0, SUBLANES) for a in range(1, SUBLANES)]
    pieces += [(SUBLANES, PEER_TOPK, 0, 1)]
    return pieces


def _topk_kernel(xn_ref, wqh_ref, wql_ref, keys_ref, row_ref, shift_ref, gate_ref, *, tm):
    h = pl.program_id(1)
    q = _dot3(xn_ref[...], wqh_ref[h], wql_ref[h])
    key_id = lax.broadcasted_iota(I32, (N_KEYS, tm), 0)
    sub_v, sub_i = [], []
    for c in range(2):
        q_hi, q_lo = _split(q[:, c * PEER_HALF:(c + 1) * PEER_HALF])
        keys = keys_ref[2 * h + c]
        k_hi, k_lo = _split(keys)
        d = functools.partial(lax.dot_general, dimension_numbers=(((1,), (1,)), ((), ())),
                              preferred_element_type=F32)
        s = d(k_hi, q_hi) + (d(k_lo, q_hi) + d(k_hi, q_lo))
        vals, picks = _topk_rows(s, key_id, PEER_TOPK)
        sub_v.append(vals)
        sub_i.append(picks)
    cs, ce, cf = [], [], []

    def rows_of(lst, lo, hi):
        return lst[lo] if hi - lo == 1 else jnp.concatenate(lst[lo:hi], axis=0)

    for a_lo, a_hi, b_lo, b_hi in _cand_layout():
        na, nbb = a_hi - a_lo, b_hi - b_lo
        rows = max(na, nbb)
        cs.append(rows_of(sub_v[0], a_lo, a_hi) + rows_of(sub_v[1], b_lo, b_hi))
        ce.append(rows_of(sub_i[0], a_lo, a_hi) * N_KEYS + rows_of(sub_i[1], b_lo, b_hi))
        r = lax.broadcasted_iota(I32, (rows, tm), 0)
        cf.append((a_lo + r) * PEER_TOPK + b_lo if na > 1 else a_lo * PEER_TOPK + b_lo + r)
    cand_s = jnp.concatenate(cs, axis=0)
    cand_e = jnp.concatenate(ce, axis=0)
    cand_f = jnp.concatenate(cf, axis=0)
    best_v, best_e = [], []
    sentinel = jnp.iinfo(jnp.int32).max
    for _ in range(PEER_TOPK):
        m = jnp.max(cand_s, axis=0, keepdims=True)
        f = jnp.min(jnp.where(cand_s == m, cand_f, sentinel), axis=0, keepdims=True)
        hit = cand_f == f
        best_v.append(m)
        best_e.append(jnp.max(jnp.where(hit, cand_e, -1), axis=0, keepdims=True))
        cand_s = jnp.where(hit, -jnp.inf, cand_s)
    bv = jnp.concatenate(best_v, axis=0)
    be = jnp.concatenate(best_e, axis=0)
    ex = jnp.exp(bv - bv[0:1])
    gate_ref[...] = ex / jnp.sum(ex, axis=0, keepdims=True)
    row_ref[...] = lax.shift_right_logical(be, 1)
    shift_ref[...] = (1 - (be & 1)) * 16


def _peer_topk(xn2d, wq_hi, wq_lo, keys, tm):
    n = xn2d.shape[0]
    out = lambda i, h: (h, i)
    return pl.pallas_call(
        functools.partial(_topk_kernel, tm=tm),
        grid=(n // tm, PEER_HEADS),
        in_specs=[pl.BlockSpec((tm, D_MODEL), lambda i, h: (i, 0)),
                  pl.BlockSpec(wq_hi.shape, lambda i, h: (0, 0, 0)),
                  pl.BlockSpec(wq_lo.shape, lambda i, h: (0, 0, 0)),
                  pl.BlockSpec(keys.shape, lambda i, h: (0, 0, 0))],
        out_specs=[pl.BlockSpec((PEER_TOPK, tm), out)] * 3,
        out_shape=[jax.ShapeDtypeStruct((N_PAIRS, n), I32), jax.ShapeDtypeStruct((N_PAIRS, n), I32),
                   jax.ShapeDtypeStruct((N_PAIRS, n), F32)],
        compiler_params=_cparams(("arbitrary", "arbitrary")),
        name="peer_topk",
    )(xn2d, wq_hi, wq_lo, keys)


def _pack_kernel(t_ref, o_ref):
    even = t_ref[:, 0].astype(BF16).astype(F32)
    odd = t_ref[:, 1].astype(BF16).astype(F32)
    hi = pltpu.bitcast(odd, I32) & HI_MASK
    lo = lax.shift_right_logical(pltpu.bitcast(even, I32), 16)
    o_ref[...] = hi | lo


def _pack_table(table, tb=256):
    e = table.shape[0]
    t4 = table.reshape(e // 2, 2, SUBLANES, LANES)
    return pl.pallas_call(
        _pack_kernel,
        grid=(e // 2 // tb,),
        in_specs=[pl.BlockSpec((tb, 2, SUBLANES, LANES), lambda i: (i, 0, 0, 0))],
        out_specs=pl.BlockSpec((tb, SUBLANES, LANES), lambda i: (i, 0, 0)),
        out_shape=jax.ShapeDtypeStruct((e // 2, SUBLANES, LANES), I32),
        compiler_params=_cparams(("arbitrary",)),
        name="pack_table",
    )(t4)


def _unpack(word, shift):
    return pltpu.bitcast((word << shift) & HI_MASK, F32)


def _fold8(p):
    sub = lax.broadcasted_iota(I32, (SUBLANES, LANES), 0)
    step = SUBLANES // 2
    while step >= 1:
        first = (sub % (2 * step)) < step
        n = len(p) // 2
        p = [jnp.where(first, p[j] + pltpu.roll(p[j], SUBLANES - step, 0),
                       p[j + n] + pltpu.roll(p[j + n], step, 0)) for j in range(n)]
        step //= 2
    return p[0]


def _load_tile_scalars(i, tm, pairs):
    copies = [pltpu.make_async_copy(src.at[pl.ds(i * tm, tm)], dst, sem) for src, dst, sem in pairs]
    for c in copies:
        c.start()
    for c in copies:
        c.wait()


def _load_table(tbl_hbm, tbl_vmem, sem):
    @pl.when(pl.program_id(0) == 0)
    def _():
        c = pltpu.make_async_copy(tbl_hbm, tbl_vmem, sem)
        c.start()
        c.wait()


def _peer_act_kernel(row_hbm, shift_hbm, x_ref, gate_ref, tbl_hbm, w_ref,
                     tbl, row_s, shift_s, sems, *, tm):
    i = pl.program_id(0)
    _load_table(tbl_hbm, tbl, sems.at[2])
    _load_tile_scalars(i, tm, [(row_hbm, row_s, sems.at[0]), (shift_hbm, shift_s, sems.at[1])])
    lane = lax.broadcasted_iota(I32, (SUBLANES, tm), 1)
    n_groups = N_PAIRS // SUBLANES

    def token(t, acts):
        xt = x_ref[t]
        out = []
        for gidx in range(n_groups):
            prods = []
            for j in range(SUBLANES):
                k = gidx * SUBLANES + j
                prods.append(_unpack(tbl[row_s[t, k]], shift_s[t, k]) * xt)
            col = jnp.sum(_fold8(prods), axis=1, keepdims=True)
            out.append(jnp.where(lane == t, col, acts[gidx]))
        return tuple(out)

    acts = lax.fori_loop(0, tm, token, tuple(jnp.zeros((SUBLANES, tm), F32) for _ in range(n_groups)))
    act = jnp.concatenate(acts, axis=0)
    w = gate_ref[...] * _gelu(act)
    w_ref[...] = w.T


def _peer_act(rows_t, shifts_t, xn3, gate, tbl, tm):
    n = xn3.shape[0]
    return pl.pallas_call(
        functools.partial(_peer_act_kernel, tm=tm),
        grid=(n // tm,),
        in_specs=[pl.BlockSpec(memory_space=pl.ANY), pl.BlockSpec(memory_space=pl.ANY),
                  pl.BlockSpec((tm, SUBLANES, LANES), lambda i: (i, 0, 0)),
                  pl.BlockSpec((N_PAIRS, tm), lambda i: (0, i)),
                  pl.BlockSpec(memory_space=pl.ANY)],
        out_specs=pl.BlockSpec((tm, N_PAIRS), lambda i: (i, 0)),
        out_shape=jax.ShapeDtypeStruct((n, N_PAIRS), F32),
        scratch_shapes=[pltpu.VMEM(tbl.shape, I32), pltpu.SMEM((tm, N_PAIRS), I32),
                        pltpu.SMEM((tm, N_PAIRS), I32), pltpu.SemaphoreType.DMA((3,))],
        compiler_params=_cparams(("arbitrary",)),
        name="peer_act",
    )(rows_t, shifts_t, xn3, gate, tbl)


def _peer_out_kernel(row_hbm, shift_hbm, w_hbm, x1_ref, tbl_hbm, o_ref,
                     tbl, row_s, shift_s, w_s, sems, *, tm):
    i = pl.program_id(0)
    _load_table(tbl_hbm, tbl, sems.at[3])
    _load_tile_scalars(i, tm, [(row_hbm, row_s, sems.at[0]), (shift_hbm, shift_s, sems.at[1]),
                               (w_hbm, w_s, sems.at[2])])
    n_acc = 4

    def token(t, carry):
        accs = [x1_ref[t]] + [jnp.zeros((SUBLANES, LANES), F32)] * (n_acc - 1)
        for k in range(N_PAIRS):
            accs[k % n_acc] = accs[k % n_acc] + w_s[t, k] * _unpack(tbl[row_s[t, k]], shift_s[t, k])
        o_ref[t] = (accs[0] + accs[1]) + (accs[2] + accs[3])
        return carry

    lax.fori_loop(0, tm, token, 0)


def _peer_out(rows_t, shifts_t, w_t, x1_3, tbl, tm):
    n = x1_3.shape[0]
    blk = pl.BlockSpec((tm, SUBLANES, LANES), lambda i: (i, 0, 0))
    return pl.pallas_call(
        functools.partial(_peer_out_kernel, tm=tm),
        grid=(n // tm,),
        in_specs=[pl.BlockSpec(memory_space=pl.ANY)] * 3 + [blk, pl.BlockSpec(memory_space=pl.ANY)],
        out_specs=blk,
        out_shape=jax.ShapeDtypeStruct(x1_3.shape, F32),
        scratch_shapes=[pltpu.VMEM(tbl.shape, I32), pltpu.SMEM((tm, N_PAIRS), I32),
                        pltpu.SMEM((tm, N_PAIRS), I32), pltpu.SMEM((tm, N_PAIRS), F32),
                        pltpu.SemaphoreType.DMA((4,))],
        compiler_params=_cparams(("arbitrary",)),
        name="peer_out",
    )(rows_t, shifts_t, w_t, x1_3, tbl)


def _final_kernel(x_ref, g_ref, o_ref):
    o_ref[...] = _rms(x_ref[...], g_ref[...])


def _final_norm(x2d, g, tm):
    n = x2d.shape[0]
    return pl.pallas_call(
        _final_kernel,
        grid=(n // tm,),
        in_specs=[pl.BlockSpec((tm, D_MODEL), lambda i: (i, 0)), pl.BlockSpec((1, D_MODEL), lambda i: (0, 0))],
        out_specs=pl.BlockSpec((tm, D_MODEL), lambda i: (i, 0)),
        out_shape=jax.ShapeDtypeStruct(x2d.shape, F32),
        compiler_params=_cparams(("arbitrary",)),
        name="final_norm",
    )(x2d, g)


def _block_diag_in(b):
    eye = jnp.eye(N_SSM_GROUPS, dtype=F32)
    return jnp.einsum('gnp,gh->gphn', b, eye).reshape(SSM_WIDTH, SSM_LANES)


def _block_diag_out(c):
    eye = jnp.eye(N_SSM_GROUPS, dtype=F32)
    return jnp.einsum('gpn,gh->gnhp', c, eye).reshape(SSM_LANES, SSM_WIDTH)


def _layer_weights(norm_mix, w_in, b_in, attn_sinks, lam_re, lam_im, log_step, b_re, b_im, c_re, c_im,
                   d_skip, w_glu, b_glu, norm_attn_out, norm_ssm_out, w_out, norm_ffn, w_query, sub_keys,
                   u_table, v_table):
    w = {}
    w['norm_mix'] = norm_mix.reshape(1, D_MODEL)
    w['w_in'] = _split(w_in)
    w['b_in'] = b_in.reshape(1, IN_WIDTH)
    w['sinks'] = attn_sinks
    ab_re, ab_im, z_re, z_im = _s5_params(lam_re, lam_im, log_step)
    w['s5'] = (list(_split(_block_diag_in(b_re))) + list(_split(_block_diag_in(b_im)))
               + [z_re, z_im, ab_re, ab_im,
                  _block_diag_out(c_re).astype(BF16), _block_diag_out(c_im).astype(BF16),
                  d_skip.reshape(1, SSM_WIDTH), w_glu.astype(BF16), b_glu.reshape(1, SSM_WIDTH),
                  norm_ssm_out.reshape(1, SSM_WIDTH)])
    w['norm_attn_out'] = norm_attn_out.reshape(1, ATTN_WIDTH)
    w['w_out_attn'] = w_out[:ATTN_WIDTH].astype(BF16)
    w['w_out_ssm'] = w_out[ATTN_WIDTH:].astype(BF16)
    w['norm_ffn'] = norm_ffn.reshape(1, D_MODEL)
    wq = w_query.reshape(D_MODEL, PEER_HEADS, PEER_QDIM).transpose(1, 0, 2)
    w['w_query'] = _split(wq)
    w['keys'] = sub_keys.reshape(PEER_HEADS * 2, N_KEYS, PEER_HALF)
    w['u_tbl'] = _pack_table(u_table)
    w['v_tbl'] = _pack_table(v_table)
    return w


def _peer(xn2d, x1_2d, w, tm_topk, tm_gather):
    n = xn2d.shape[0]
    rows, shifts, gate = _peer_topk(xn2d, w['w_query'][0], w['w_query'][1], w['keys'], tm_topk)
    rows_t = rows.T
    shifts_t = shifts.T
    wt = _peer_act(rows_t, shifts_t, xn2d.reshape(n, SUBLANES, LANES), gate, w['u_tbl'], tm_gather)
    out = _peer_out(rows_t, shifts_t, wt, x1_2d.reshape(n, SUBLANES, LANES), w['v_tbl'], tm_gather)
    return out.reshape(n, D_MODEL)


def _token_tile(n, cap):
    t = cap
    while n % t:
        t //= 2
    return t


def _mix_and_ffn(x2d, attn_n, ssm_n, w, norm_final):
    n = x2d.shape[0]
    x1, xn2 = _out_proj(attn_n, ssm_n, x2d, w['w_out_attn'], w['w_out_ssm'], w['norm_ffn'], _token_tile(n, 512))
    x2 = _peer(xn2, x1, w, _token_tile(n, 256), _token_tile(n, 128))
    return x2


def _prompt_layer(x, w):
    b, t, _ = x.shape
    n = b * t
    x2d = x.reshape(n, D_MODEL)
    tm = _token_tile(t, 512)
    tables = _rope_tables(jnp.arange(t, dtype=I32))
    q, k, v, u = _in_proj(x2d, tables, t // tm, w['norm_mix'], w['w_in'][0], w['w_in'][1], w['b_in'], tm)
    k3 = k.reshape(b, t, KV_WIDTH)
    v3 = v.reshape(b, t, KV_WIDTH)
    (attn_n,) = _attention(w['sinks'], q.reshape(b, t, ATTN_WIDTH), k3, k3, v3, v3, w['norm_attn_out'], False)
    h0 = jnp.zeros((b, 1, SSM_LANES), F32)
    ssm_n, h_re, h_im = _s5(u.reshape(b, t, SSM_WIDTH), h0, h0, w['s5'], _token_tile(t, 256), 1)
    x2 = _mix_and_ffn(x2d, attn_n.reshape(n, ATTN_WIDTH), ssm_n.reshape(n, SSM_WIDTH), w, None)
    k_win = k3[:, -WINDOW:].reshape(b, WINDOW, N_KV_HEADS, HEAD_DIM)
    v_win = v3[:, -WINDOW:].reshape(b, WINDOW, N_KV_HEADS, HEAD_DIM)
    st = lambda h: h.reshape(b, N_SSM_GROUPS, SSM_STATE)
    return x2.reshape(b, t, D_MODEL), k_win, v_win, st(h_re), st(h_im)


def _sample_layer(x, k_buf, v_buf, h0_re, h0_im, w, past_len):
    b, t, _ = x.shape
    n = b * t
    x2d = x.reshape(n, D_MODEL)
    tm = _token_tile(n, 512)
    pos = past_len + jnp.arange(t, dtype=I32)
    tables = tuple(jnp.tile(a, (tm // t, 1)) for a in _rope_tables(pos))
    q, k, v, u = _in_proj(x2d, tables, 1, w['norm_mix'], w['w_in'][0], w['w_in'][1], w['b_in'], tm)
    pad = lambda a: jnp.pad(a.reshape(b, t, KV_WIDTH), ((0, 0), (0, WINDOW - t), (0, 0)))
    attn_n, k_win, v_win = _attention(w['sinks'], q.reshape(b, t, ATTN_WIDTH),
                                      k_buf.reshape(b, WINDOW, KV_WIDTH), pad(k),
                                      v_buf.reshape(b, WINDOW, KV_WIDTH), pad(v), w['norm_attn_out'], True)
    ns = min(b, S5_DECODE_SEQS)
    nbk = b // ns
    u_tm = u.reshape(nbk, ns, t, SSM_WIDTH).transpose(0, 2, 1, 3).reshape(nbk, t * ns, SSM_WIDTH)
    ssm_tm, h_re, h_im = _s5(u_tm, h0_re.reshape(nbk, ns, SSM_LANES), h0_im.reshape(nbk, ns, SSM_LANES),
                             w['s5'], t, ns)
    ssm_n = ssm_tm.reshape(nbk, t, ns, SSM_WIDTH).transpose(0, 2, 1, 3).reshape(n, SSM_WIDTH)
    x2 = _mix_and_ffn(x2d, attn_n.reshape(n, ATTN_WIDTH), ssm_n, w, None)
    win = lambda a: a.reshape(b, WINDOW, N_KV_HEADS, HEAD_DIM)
    st = lambda h: h.reshape(b, N_SSM_GROUPS, SSM_STATE)
    return x2.reshape(b, t, D_MODEL), win(k_win), win(v_win), st(h_re), st(h_im)


PAST_LEN = 16384
S5_DECODE_SEQS = 64


def kernel(x_prompt, x_sample, cache_k, cache_v, state_ssm_re, state_ssm_im, norm_mix, w_in, b_in, attn_sinks, ssm_lam_re, ssm_lam_im, ssm_log_step, ssm_b_re, ssm_b_im, ssm_c_re, ssm_c_im, ssm_d, ssm_w_glu, ssm_b_glu, norm_attn_out, norm_ssm_out, w_out, norm_ffn, peer_w_query, peer_sub_keys, peer_u, peer_v, norm_final):
    depth = norm_mix.shape[0]
    xp, xs = x_prompt, x_sample
    outs = [[] for _ in range(8)]
    for l in range(depth):
        w = _layer_weights(*[a[l] for a in (norm_mix, w_in, b_in, attn_sinks, ssm_lam_re, ssm_lam_im,
                                            ssm_log_step, ssm_b_re, ssm_b_im, ssm_c_re, ssm_c_im, ssm_d,
                                            ssm_w_glu, ssm_b_glu, norm_attn_out, norm_ssm_out, w_out,
                                            norm_ffn, peer_w_query, peer_sub_keys, peer_u, peer_v)])
        xs, k2, v2, r2, i2 = _sample_layer(xs, cache_k[l], cache_v[l], state_ssm_re[l], state_ssm_im[l],
                                           w, PAST_LEN)
        xp, k1, v1, r1, i1 = _prompt_layer(xp, w)
        for lst, a in zip(outs, (k1, v1, r1, i1, k2, v2, r2, i2)):
            lst.append(a)
    g = norm_final.reshape(1, D_MODEL)
    yp = _final_norm(xp.reshape(-1, D_MODEL), g, 512).reshape(xp.shape)
    ys = _final_norm(xs.reshape(-1, D_MODEL), g, 512).reshape(xs.shape)
    return (yp, ys) + tuple(jnp.stack(o) for o in outs)
```

```python
import functools
import math

import jax
import jax.numpy as jnp
from jax import lax
from jax.experimental import pallas as pl
from jax.experimental.pallas import tpu as pltpu

F32 = jnp.float32
BF16 = jnp.bfloat16
I32 = jnp.int32

D_MODEL = 1024
N_HEADS = 8
N_KV_HEADS = 2
HEAD_DIM = 64
ATTN_WIDTH = N_HEADS * HEAD_DIM
KV_WIDTH = N_KV_HEADS * HEAD_DIM
WINDOW = 128
ROT_DIM = HEAD_DIM // 4
ROPE_THETA = 500000.0
NEG_INF = -1e30
SSM_WIDTH = D_MODEL - ATTN_WIDTH
SSM_GROUP = 16
N_SSM_GROUPS = SSM_WIDTH // SSM_GROUP
SSM_STATE = 64
SSM_LANES = N_SSM_GROUPS * SSM_STATE
IN_WIDTH = ATTN_WIDTH + 2 * KV_WIDTH + SSM_WIDTH
PEER_HEADS = 8
N_KEYS = 128
PEER_TOPK = 16
PEER_QDIM = 256
PEER_HALF = PEER_QDIM // 2
N_PAIRS = PEER_HEADS * PEER_TOPK
RMS_EPS = 1e-5

LANES = 128
SUBLANES = 8
VMEM_LIMIT = 56 * 1024 * 1024

HI_MASK = -65536


def _cparams(sem):
    return pltpu.CompilerParams(dimension_semantics=sem, vmem_limit_bytes=VMEM_LIMIT)


def _split(w):
    hi = w.astype(BF16)
    lo = (w - hi.astype(F32)).astype(BF16)
    return hi, lo


def _dot1(a, b):
    return jnp.dot(a.astype(BF16), b, preferred_element_type=F32)


def _dot3(a, b_hi, b_lo):
    a_hi, a_lo = _split(a)
    d = functools.partial(jnp.dot, preferred_element_type=F32)
    return d(a_hi, b_hi) + (d(a_lo, b_hi) + d(a_hi, b_lo))


def _dot3_nt(a_hi, a_lo, b):
    b_hi, b_lo = _split(b)
    d = functools.partial(lax.dot_general, dimension_numbers=(((1,), (1,)), ((), ())),
                          preferred_element_type=F32)
    return d(a_hi, b_hi) + (d(a_lo, b_hi) + d(a_hi, b_lo))


def _gelu(x):
    return 0.5 * x * (1.0 + lax.erf(x * (2.0 ** -0.5)))


def _rms(x, g):
    return x * lax.rsqrt(jnp.mean(x * x, axis=-1, keepdims=True) + RMS_EPS) * g


def _inproj_kernel(x_ref, g_ref, whi_ref, wlo_ref, b_ref, c_ref, sa_ref, sb_ref,
                   q_ref, k_ref, v_ref, u_ref):
    xn = _rms(x_ref[...], g_ref[...])
    proj = _dot3(xn, whi_ref[...], wlo_ref[...]) + b_ref[...]
    c = c_ref[...]
    sa = sa_ref[...]
    sb = sb_ref[...]

    def rope(t):
        return t * c + pltpu.roll(t, LANES - ROT_DIM // 2, 1) * sa + pltpu.roll(t, ROT_DIM // 2, 1) * sb

    for j in range(ATTN_WIDTH // LANES):
        q_ref[:, j * LANES:(j + 1) * LANES] = rope(proj[:, j * LANES:(j + 1) * LANES])
    k_ref[...] = rope(proj[:, ATTN_WIDTH:ATTN_WIDTH + KV_WIDTH])
    v_ref[...] = proj[:, ATTN_WIDTH + KV_WIDTH:ATTN_WIDTH + 2 * KV_WIDTH]
    u_ref[...] = proj[:, ATTN_WIDTH + 2 * KV_WIDTH:]


def _rope_tables(pos):
    half = ROT_DIM // 2
    inv_freq = ROPE_THETA ** (-jnp.arange(half, dtype=F32) * 2.0 / ROT_DIM)
    ang = pos.astype(F32)[:, None] * inv_freq[None, :]
    cos = jnp.cos(ang)
    sin = jnp.sin(ang)
    t = pos.shape[0]
    one = jnp.ones((t, HEAD_DIM - ROT_DIM), F32)
    zero = jnp.zeros((t, HEAD_DIM - ROT_DIM), F32)
    zh = jnp.zeros((t, half), F32)
    c = jnp.concatenate([cos, cos, one], axis=1)
    sa = jnp.concatenate([-sin, zh, zero], axis=1)
    sb = jnp.concatenate([zh, sin, zero], axis=1)
    return tuple(jnp.tile(a, (1, LANES // HEAD_DIM)) for a in (c, sa, sb))


def _in_proj(x2d, pos_tables, n_pos_tiles, g, w_hi, w_lo, b, tm):
    n = x2d.shape[0]
    c, sa, sb = pos_tables
    row = lambda i: (i, 0)
    fixed = lambda i: (0, 0)
    pos_map = lambda i: (i % n_pos_tiles, 0)
    return pl.pallas_call(
        _inproj_kernel,
        grid=(n // tm,),
        in_specs=[pl.BlockSpec((tm, D_MODEL), row), pl.BlockSpec((1, D_MODEL), fixed),
                  pl.BlockSpec((D_MODEL, IN_WIDTH), fixed), pl.BlockSpec((D_MODEL, IN_WIDTH), fixed),
                  pl.BlockSpec((1, IN_WIDTH), fixed),
                  pl.BlockSpec((tm, LANES), pos_map), pl.BlockSpec((tm, LANES), pos_map),
                  pl.BlockSpec((tm, LANES), pos_map)],
        out_specs=[pl.BlockSpec((tm, ATTN_WIDTH), row), pl.BlockSpec((tm, KV_WIDTH), row),
                   pl.BlockSpec((tm, KV_WIDTH), row), pl.BlockSpec((tm, SSM_WIDTH), row)],
        out_shape=[jax.ShapeDtypeStruct((n, ATTN_WIDTH), F32), jax.ShapeDtypeStruct((n, KV_WIDTH), F32),
                   jax.ShapeDtypeStruct((n, KV_WIDTH), F32), jax.ShapeDtypeStruct((n, SSM_WIDTH), F32)],
        compiler_params=_cparams(("arbitrary",)),
        name="in_proj",
    )(x2d, g, w_hi, w_lo, b, c, sa, sb)


def _attn_kernel(sink_ref, q_ref, kp_ref, kc_ref, vp_ref, vc_ref, g_ref, o_ref, *win_refs, tq, decode):
    q = q_ref[0] * (HEAD_DIM ** -0.5)
    kp = kp_ref[0]
    kc = kc_ref[0]
    vp = vp_ref[0]
    vc = vc_ref[0]
    half = LANES // 2
    lane = lax.broadcasted_iota(I32, (1, LANES), 1)
    lo = lane < half
    qi = lax.broadcasted_iota(I32, (tq, WINDOW), 0)
    kj = lax.broadcasted_iota(I32, (tq, WINDOW), 1)
    first_off = 0 if decode else jnp.where(pl.program_id(1) > 0, 0, WINDOW)
    m_prev = kj > qi + first_off
    m_cur = kj <= qi

    def variants(t):
        r = pltpu.roll(t, half, 1)
        return [[t, r], [r, t]]

    kpv = [[a.astype(BF16) for a in row] for row in variants(kp)]
    kcv = [[a.astype(BF16) for a in row] for row in variants(kc)]
    lane_sel = [lo, jnp.logical_not(lo)]
    vpv = [[jnp.where(lane_sel[s], a, 0.0).astype(BF16) for s, a in enumerate(row)] for row in variants(vp)]
    vcv = [[jnp.where(lane_sel[s], a, 0.0).astype(BF16) for s, a in enumerate(row)] for row in variants(vc)]
    nt = functools.partial(lax.dot_general, dimension_numbers=(((1,), (1,)), ((), ())),
                           preferred_element_type=F32)
    outs = []
    for j in range(ATTN_WIDTH // LANES):
        g = (2 * j) // (N_HEADS // N_KV_HEADS)
        qt = q[:, j * LANES:(j + 1) * LANES]
        acc = jnp.zeros((tq, LANES), F32)
        for s in range(2):
            h = 2 * j + s
            qm = jnp.where(lane_sel[s], qt, 0.0).astype(BF16)
            sp = jnp.where(m_prev, nt(qm, kpv[g][s]), NEG_INF)
            sc = jnp.where(m_cur, nt(qm, kcv[g][s]), NEG_INF)
            sink = sink_ref[h]
            m = jnp.maximum(jnp.maximum(jnp.max(sp, axis=-1, keepdims=True),
                                        jnp.max(sc, axis=-1, keepdims=True)), sink)
            pp = jnp.exp(sp - m)
            pc = jnp.exp(sc - m)
            den = (jnp.sum(pp, axis=-1, keepdims=True) + jnp.sum(pc, axis=-1, keepdims=True)
                   + jnp.exp(sink - m))
            o = (jnp.dot(pp.astype(BF16), vpv[g][s], preferred_element_type=F32)
                 + jnp.dot(pc.astype(BF16), vcv[g][s], preferred_element_type=F32))
            acc = acc + o / den
        outs.append(acc)
    attn = jnp.concatenate(outs, axis=1)
    o_ref[0] = _rms(attn, g_ref[...])
    if decode:
        kw_ref, vw_ref = win_refs
        kw_ref[0, :WINDOW - tq] = kp[tq:]
        kw_ref[0, WINDOW - tq:] = kc[:tq]
        vw_ref[0, :WINDOW - tq] = vp[tq:]
        vw_ref[0, WINDOW - tq:] = vc[:tq]


def _attention(sinks, q3, k_prev, k_cur, v_prev, v_cur, g, decode):
    b, t, _ = q3.shape
    tq = t if decode else WINDOW
    nb = t // tq
    cur = lambda i, n: (i, n, 0)
    prev = (lambda i, n: (i, 0, 0)) if decode else (lambda i, n: (i, jnp.maximum(n - 1, 0), 0))
    kvb = (1, WINDOW, KV_WIDTH)
    out_specs = [pl.BlockSpec((1, tq, ATTN_WIDTH), cur)]
    out_shape = [jax.ShapeDtypeStruct((b, t, ATTN_WIDTH), F32)]
    if decode:
        out_specs += [pl.BlockSpec(kvb, cur), pl.BlockSpec(kvb, cur)]
        out_shape += [jax.ShapeDtypeStruct((b, WINDOW, KV_WIDTH), F32)] * 2
    return pl.pallas_call(
        functools.partial(_attn_kernel, tq=tq, decode=decode),
        grid=(b, nb),
        in_specs=[pl.BlockSpec(memory_space=pltpu.SMEM),
                  pl.BlockSpec((1, tq, ATTN_WIDTH), cur),
                  pl.BlockSpec(kvb, prev), pl.BlockSpec(kvb, cur),
                  pl.BlockSpec(kvb, prev), pl.BlockSpec(kvb, cur),
                  pl.BlockSpec((1, ATTN_WIDTH), lambda i, n: (0, 0))],
        out_specs=out_specs,
        out_shape=out_shape,
        compiler_params=_cparams(("arbitrary", "arbitrary")),
        name="attn_decode" if decode else "attn_prompt",
    )(sinks, q3, k_prev, k_cur, v_prev, v_cur, g)


def _s5_param_kernel(lr_ref, li_ref, ls_ref, abr_ref, abi_ref, zr_ref, zi_ref):
    lr = lr_ref[...]
    li = li_ref[...]
    dt = jnp.exp(ls_ref[...])
    mag = jnp.exp(lr * dt)
    ab_re = mag * jnp.cos(li * dt)
    ab_im = mag * jnp.sin(li * dt)
    den = lr * lr + li * li
    abr_ref[...] = ab_re
    abi_ref[...] = ab_im
    zr_ref[...] = ((ab_re - 1.0) * lr + ab_im * li) / den
    zi_ref[...] = (ab_im * lr - (ab_re - 1.0) * li) / den


def _s5_params(lam_re, lam_im, log_step):
    ls = jnp.broadcast_to(log_step[:, None], lam_re.shape)
    outs = pl.pallas_call(
        _s5_param_kernel,
        out_shape=[jax.ShapeDtypeStruct(lam_re.shape, F32)] * 4,
        name="s5_params",
    )(lam_re, lam_im, ls)
    return [o.reshape(1, SSM_LANES) for o in outs]


def _s5_kernel(u_ref, h0r_ref, h0i_ref, brh_ref, brl_ref, bih_ref, bil_ref, zr_ref, zi_ref,
               ar_ref, ai_ref, cr_ref, ci_ref, d_ref, wg_ref, bg_ref, g_ref,
               y_ref, hr_ref, hi_ref, sr_ref, si_ref, *, tt, ns):
    @pl.when(pl.program_id(1) == 0)
    def _():
        hr_ref[0] = h0r_ref[0]
        hi_ref[0] = h0i_ref[0]

    u = u_ref[0]
    pr = _dot3(u, brh_ref[...], brl_ref[...])
    pi = _dot3(u, bih_ref[...], bil_ref[...])
    zr = zr_ref[...]
    zi = zi_ref[...]
    sr_ref[...] = zr * pr - zi * pi
    si_ref[...] = zr * pi + zi * pr
    ar = jnp.broadcast_to(ar_ref[...], (ns, SSM_LANES))
    ai = jnp.broadcast_to(ai_ref[...], (ns, SSM_LANES))

    if ns <= SUBLANES:
        def body(t, carry):
            hr, hi = carry
            rows = pl.ds(t * ns, ns)
            nr = ar * hr - ai * hi + sr_ref[rows, :]
            ni = ar * hi + ai * hr + si_ref[rows, :]
            sr_ref[rows, :] = nr
            si_ref[rows, :] = ni
            return nr, ni

        hr, hi = lax.fori_loop(0, tt, body, (hr_ref[0], hi_ref[0]), unroll=8)
        hr_ref[0] = hr
        hi_ref[0] = hi
    else:
        def body(t, carry):
            rows = pl.ds(pl.multiple_of(t * ns, ns), ns)
            hr = hr_ref[0]
            hi = hi_ref[0]
            nr = ar * hr - ai * hi + sr_ref[rows, :]
            ni = ar * hi + ai * hr + si_ref[rows, :]
            sr_ref[rows, :] = nr
            si_ref[rows, :] = ni
            hr_ref[0] = nr
            hi_ref[0] = ni
            return carry

        lax.fori_loop(0, tt, body, 0)

    y = _dot1(sr_ref[...], cr_ref[...]) - _dot1(si_ref[...], ci_ref[...]) + d_ref[...] * u
    y = _gelu(y)
    gate = _dot1(y, wg_ref[...]) + bg_ref[...]
    y = y * (1.0 / (1.0 + jnp.exp(-gate)))
    y_ref[0] = _rms(y, g_ref[...])


def _s5(u3, h0r, h0i, consts, tt, ns):
    nb, rows, _ = u3.shape
    r = tt * ns
    fixed = lambda b, t: (0, 0)
    tile = lambda b, t: (b, t, 0)
    seq = lambda b, t: (b, 0, 0)
    const_specs = [pl.BlockSpec(c.shape, fixed) for c in consts]
    return pl.pallas_call(
        functools.partial(_s5_kernel, tt=tt, ns=ns),
        grid=(nb, rows // r),
        in_specs=[pl.BlockSpec((1, r, SSM_WIDTH), tile),
                  pl.BlockSpec((1, ns, SSM_LANES), seq), pl.BlockSpec((1, ns, SSM_LANES), seq)] + const_specs,
        out_specs=[pl.BlockSpec((1, r, SSM_WIDTH), tile),
                   pl.BlockSpec((1, ns, SSM_LANES), seq), pl.BlockSpec((1, ns, SSM_LANES), seq)],
        out_shape=[jax.ShapeDtypeStruct((nb, rows, SSM_WIDTH), F32),
                   jax.ShapeDtypeStruct((nb, ns, SSM_LANES), F32),
                   jax.ShapeDtypeStruct((nb, ns, SSM_LANES), F32)],
        scratch_shapes=[pltpu.VMEM((r, SSM_LANES), F32), pltpu.VMEM((r, SSM_LANES), F32)],
        compiler_params=_cparams(("arbitrary", "arbitrary")),
        name="s5_ns%d" % ns,
    )(u3, h0r, h0i, *consts)


def _outproj_kernel(a_ref, s_ref, x_ref, wa_ref, ws_ref, g_ref, x1_ref, xn_ref):
    x1 = x_ref[...] + _dot1(a_ref[...], wa_ref[...]) + _dot1(s_ref[...], ws_ref[...])
    x1_ref[...] = x1
    xn_ref[...] = _rms(x1, g_ref[...])


def _out_proj(attn_n, ssm_n, x2d, w_attn, w_ssm, g, tm):
    n = x2d.shape[0]
    row = lambda i: (i, 0)
    fixed = lambda i: (0, 0)
    return pl.pallas_call(
        _outproj_kernel,
        grid=(n // tm,),
        in_specs=[pl.BlockSpec((tm, ATTN_WIDTH), row), pl.BlockSpec((tm, SSM_WIDTH), row),
                  pl.BlockSpec((tm, D_MODEL), row),
                  pl.BlockSpec((ATTN_WIDTH, D_MODEL), fixed), pl.BlockSpec((SSM_WIDTH, D_MODEL), fixed),
                  pl.BlockSpec((1, D_MODEL), fixed)],
        out_specs=[pl.BlockSpec((tm, D_MODEL), row), pl.BlockSpec((tm, D_MODEL), row)],
        out_shape=[jax.ShapeDtypeStruct((n, D_MODEL), F32)] * 2,
        compiler_params=_cparams(("arbitrary",)),
        name="out_proj",
    )(attn_n, ssm_n, x2d, w_attn, w_ssm, g)


def _topk_rows(s, ids, k):
    vals, picks = [], []
    sentinel = jnp.iinfo(jnp.int32).max
    for _ in range(k):
        m = jnp.max(s, axis=0, keepdims=True)
        pick = jnp.min(jnp.where(s == m, ids, sentinel), axis=0, keepdims=True)
        vals.append(m)
        picks.append(pick)
        s = jnp.where(ids == pick, -jnp.inf, s)
    return vals, picks


def _cand_layout():
    pieces = [(0, 1, 0, PEER_TOPK)]
    pieces += [(a, a + 1, 0, SUBLANES) for a in range(1, SUBLANES)]
    pieces += [(SUBLANES, PEER_TOPK, 0, 1)]
    return pieces


def _topk_kernel(xn_ref, wqh_ref, wql_ref, keys_ref, row_ref, par_ref, gate_ref, *, tm):
    h = pl.program_id(1)
    q = _dot3(xn_ref[...], wqh_ref[h], wql_ref[h])
    key_id = lax.broadcasted_iota(I32, (N_KEYS, tm), 0)
    sub_v, sub_i = [], []
    for c in range(2):
        q_hi, q_lo = _split(q[:, c * PEER_HALF:(c + 1) * PEER_HALF])
        keys = keys_ref[2 * h + c]
        k_hi, k_lo = _split(keys)
        d = functools.partial(lax.dot_general, dimension_numbers=(((1,), (1,)), ((), ())),
                              preferred_element_type=F32)
        s = d(k_hi, q_hi) + (d(k_lo, q_hi) + d(k_hi, q_lo))
        vals, picks = _topk_rows(s, key_id, PEER_TOPK)
        sub_v.append(vals)
        sub_i.append(picks)
    cs, ce, cf = [], [], []

    def rows_of(lst, lo, hi):
        return lst[lo] if hi - lo == 1 else jnp.concatenate(lst[lo:hi], axis=0)

    for a_lo, a_hi, b_lo, b_hi in _cand_layout():
        na, nbb = a_hi - a_lo, b_hi - b_lo
        rows = max(na, nbb)
        cs.append(rows_of(sub_v[0], a_lo, a_hi) + rows_of(sub_v[1], b_lo, b_hi))
        ce.append(rows_of(sub_i[0], a_lo, a_hi) * N_KEYS + rows_of(sub_i[1], b_lo, b_hi))
        r = lax.broadcasted_iota(I32, (rows, tm), 0)
        cf.append((a_lo + r) * PEER_TOPK + b_lo if na > 1 else a_lo * PEER_TOPK + b_lo + r)
    cand_s = jnp.concatenate(cs, axis=0)
    cand_e = jnp.concatenate(ce, axis=0)
    cand_f = jnp.concatenate(cf, axis=0)
    best_v, best_e = [], []
    sentinel = jnp.iinfo(jnp.int32).max
    for _ in range(PEER_TOPK):
        m = jnp.max(cand_s, axis=0, keepdims=True)
        f = jnp.min(jnp.where(cand_s == m, cand_f, sentinel), axis=0, keepdims=True)
        hit = cand_f == f
        best_v.append(m)
        best_e.append(jnp.max(jnp.where(hit, cand_e, -1), axis=0, keepdims=True))
        cand_s = jnp.where(hit, -jnp.inf, cand_s)
    bv = jnp.concatenate(best_v, axis=0)
    be = jnp.concatenate(best_e, axis=0)
    ex = jnp.exp(bv - bv[0:1])
    gate_ref[...] = ex / jnp.sum(ex, axis=0, keepdims=True)
    row_ref[...] = lax.shift_right_logical(be, 1) * SUBLANES
    par_ref[...] = be & 1


def _peer_topk(xn2d, wq_hi, wq_lo, keys, tm):
    n = xn2d.shape[0]
    out = lambda i, h: (h, i)
    return pl.pallas_call(
        functools.partial(_topk_kernel, tm=tm),
        grid=(n // tm, PEER_HEADS),
        in_specs=[pl.BlockSpec((tm, D_MODEL), lambda i, h: (i, 0)),
                  pl.BlockSpec(wq_hi.shape, lambda i, h: (0, 0, 0)),
                  pl.BlockSpec(wq_lo.shape, lambda i, h: (0, 0, 0)),
                  pl.BlockSpec(keys.shape, lambda i, h: (0, 0, 0))],
        out_specs=[pl.BlockSpec((PEER_TOPK, tm), out)] * 3,
        out_shape=[jax.ShapeDtypeStruct((N_PAIRS, n), I32), jax.ShapeDtypeStruct((N_PAIRS, n), I32),
                   jax.ShapeDtypeStruct((N_PAIRS, n), F32)],
        compiler_params=_cparams(("arbitrary", "arbitrary")),
        name="peer_topk",
    )(xn2d, wq_hi, wq_lo, keys)


HALF_SUB = SUBLANES // 2


def _bits(a):
    return pltpu.bitcast(a, I32)


def _pack_pairs_kernel(t_ref, o_ref):
    even = t_ref[:, 0].astype(BF16).astype(F32)
    odd = t_ref[:, 1].astype(BF16).astype(F32)
    o_ref[...] = (_bits(odd) & HI_MASK) | lax.shift_right_logical(_bits(even), 16)


def _pack_halves_kernel(t_ref, o_ref):
    even = t_ref[:, 0].astype(BF16).astype(F32)
    odd = t_ref[:, 1].astype(BF16).astype(F32)
    sub = lax.broadcasted_iota(I32, even.shape, 1)
    first = (_bits(pltpu.roll(even, HALF_SUB, 1)) & HI_MASK) | lax.shift_right_logical(_bits(even), 16)
    second = (_bits(odd) & HI_MASK) | lax.shift_right_logical(_bits(pltpu.roll(odd, HALF_SUB, 1)), 16)
    o_ref[...] = jnp.where(sub < HALF_SUB, first, second)


def _pack_table(table, halves, tb=256):
    e = table.shape[0]
    t4 = table.reshape(e // 2, 2, SUBLANES, LANES)
    return pl.pallas_call(
        _pack_halves_kernel if halves else _pack_pairs_kernel,
        grid=(e // 2 // tb,),
        in_specs=[pl.BlockSpec((tb, 2, SUBLANES, LANES), lambda i: (i, 0, 0, 0))],
        out_specs=pl.BlockSpec((tb, SUBLANES, LANES), lambda i: (i, 0, 0)),
        out_shape=jax.ShapeDtypeStruct((e // 2, SUBLANES, LANES), I32),
        compiler_params=_cparams(("arbitrary",)),
        name="pack_halves" if halves else "pack_pairs",
    )(t4)


def _tile(tbl, offset):
    return tbl[pl.ds(pl.multiple_of(offset, SUBLANES), SUBLANES), :]


def _low_f32(word):
    return pltpu.bitcast(word << 16, F32)


def _high_f32(word):
    return pltpu.bitcast(word & HI_MASK, F32)


def _fold4(p):
    sub = lax.broadcasted_iota(I32, (SUBLANES, LANES), 0)
    step = HALF_SUB // 2
    while step >= 1:
        first = (sub % (2 * step)) < step
        n = len(p) // 2
        p = [jnp.where(first, p[j] + pltpu.roll(p[j], SUBLANES - step, 0),
                       p[j + n] + pltpu.roll(p[j + n], step, 0)) for j in range(n)]
        step //= 2
    return p[0]


def _row_copy(row_hbm, row_s, sems, tile, slot, tm):
    return pltpu.make_async_copy(row_hbm.at[pl.ds(tile * tm, tm)], row_s.at[slot], sems.at[slot])


def _stage_rows(row_hbm, row_s, sems, tm):
    i = pl.program_id(0)
    slot = i % 2

    @pl.when(i == 0)
    def _():
        _row_copy(row_hbm, row_s, sems, 0, 0, tm).start()

    @pl.when(i + 1 < pl.num_programs(0))
    def _():
        _row_copy(row_hbm, row_s, sems, i + 1, 1 - slot, tm).start()

    _row_copy(row_hbm, row_s, sems, i, slot, tm).wait()
    return slot


def _load_table(tbl_hbm, tbl_vmem, sem):
    @pl.when(pl.program_id(0) == 0)
    def _():
        c = pltpu.make_async_copy(tbl_hbm, tbl_vmem, sem)
        c.start()
        c.wait()


def _peer_act_kernel(row_hbm, x_ref, par_ref, gate_ref, tbl_hbm, wlo_ref, whi_ref,
                     tbl, row_s, acc_e, acc_o, sems, *, tm):
    _load_table(tbl_hbm, tbl, sems.at[2])
    slot = _stage_rows(row_hbm, row_s, sems, tm)
    sub = lax.broadcasted_iota(I32, (SUBLANES, LANES), 0)
    low_half = sub < HALF_SUB
    lane = lax.broadcasted_iota(I32, (SUBLANES, tm), 1)
    n_groups = N_PAIRS // SUBLANES
    acc_e[...] = jnp.zeros(acc_e.shape, F32)
    acc_o[...] = jnp.zeros(acc_o.shape, F32)

    def token(t, carry):
        xt = x_ref[t]
        xr = pltpu.roll(xt, HALF_SUB, 0)
        x_lo = jnp.where(low_half, xt, xr)
        x_hi = jnp.where(low_half, xr, xt)
        here = lane == t
        for g in range(n_groups):
            folded = []
            for q in range(2):
                prods = []
                for j in range(HALF_SUB):
                    word = _tile(tbl, row_s[slot, t, g * SUBLANES + q * HALF_SUB + j])
                    prods.append(_low_f32(word) * x_lo + _high_f32(word) * x_hi)
                folded.append(_fold4(prods))
            even = jnp.where(low_half, folded[0], pltpu.roll(folded[1], HALF_SUB, 0))
            odd = jnp.where(low_half, pltpu.roll(folded[0], HALF_SUB, 0), folded[1])
            acc_e[g] = jnp.where(here, jnp.sum(even, axis=1, keepdims=True), acc_e[g])
            acc_o[g] = jnp.where(here, jnp.sum(odd, axis=1, keepdims=True), acc_o[g])
        return carry

    lax.fori_loop(0, tm, token, 0)
    act_e = jnp.concatenate([acc_e[g] for g in range(n_groups)], axis=0)
    act_o = jnp.concatenate([acc_o[g] for g in range(n_groups)], axis=0)
    is_odd = par_ref[...] == 1
    w = gate_ref[...] * _gelu(jnp.where(is_odd, act_o, act_e))
    wlo_ref[...] = jnp.where(is_odd, 0.0, w)
    whi_ref[...] = jnp.where(is_odd, w, 0.0)


def _peer_act(rows_t, xn3, par, gate, tbl, tm):
    n = xn3.shape[0]
    col = pl.BlockSpec((N_PAIRS, tm), lambda i: (0, i))
    return pl.pallas_call(
        functools.partial(_peer_act_kernel, tm=tm),
        grid=(n // tm,),
        in_specs=[pl.BlockSpec(memory_space=pl.ANY),
                  pl.BlockSpec((tm, SUBLANES, LANES), lambda i: (i, 0, 0)),
                  col, col, pl.BlockSpec(memory_space=pl.ANY)],
        out_specs=[col, col],
        out_shape=[jax.ShapeDtypeStruct((N_PAIRS, n), F32)] * 2,
        scratch_shapes=[pltpu.VMEM(tbl.shape, I32), pltpu.SMEM((2, tm, N_PAIRS), I32),
                        pltpu.VMEM((N_PAIRS // SUBLANES, SUBLANES, tm), F32),
                        pltpu.VMEM((N_PAIRS // SUBLANES, SUBLANES, tm), F32),
                        pltpu.SemaphoreType.DMA((3,))],
        compiler_params=_cparams(("arbitrary",)),
        name="peer_act",
    )(rows_t, xn3, par, gate, tbl)


def _peer_out_kernel(row_hbm, wlo_ref, whi_ref, x1_ref, tbl_hbm, o_ref,
                     tbl, row_s, wb_a, wb_b, sems, *, tm):
    _load_table(tbl_hbm, tbl, sems.at[2])
    slot = _stage_rows(row_hbm, row_s, sems, tm)
    lane = lax.broadcasted_iota(I32, (N_PAIRS, tm), 1)
    n_acc = 4

    def expand(t, wb):
        here = lane == t
        lo = jnp.sum(jnp.where(here, wlo_ref[...], 0.0), axis=1, keepdims=True)
        hi = jnp.sum(jnp.where(here, whi_ref[...], 0.0), axis=1, keepdims=True)
        wb[0] = jnp.broadcast_to(lo, (N_PAIRS, LANES))
        wb[1] = jnp.broadcast_to(hi, (N_PAIRS, LANES))

    def gather(t, wb):
        accs = [x1_ref[t]] + [jnp.zeros((SUBLANES, LANES), F32)] * (n_acc - 1)
        for k in range(N_PAIRS):
            word = _tile(tbl, row_s[slot, t, k])
            w_lo = jnp.broadcast_to(wb[0, k:k + 1, :], (SUBLANES, LANES))
            w_hi = jnp.broadcast_to(wb[1, k:k + 1, :], (SUBLANES, LANES))
            accs[k % n_acc] = accs[k % n_acc] + (w_lo * _low_f32(word) + w_hi * _high_f32(word))
        o_ref[t] = (accs[0] + accs[1]) + (accs[2] + accs[3])

    expand(0, wb_a)

    def two_tokens(j, carry):
        t = 2 * j
        gather(t, wb_a)
        expand(t + 1, wb_b)
        gather(t + 1, wb_b)
        expand(jnp.minimum(t + 2, tm - 1), wb_a)
        return carry

    lax.fori_loop(0, tm // 2, two_tokens, 0)


def _peer_out(rows_t, wlo, whi, x1_3, tbl, tm):
    n = x1_3.shape[0]
    blk = pl.BlockSpec((tm, SUBLANES, LANES), lambda i: (i, 0, 0))
    col = pl.BlockSpec((N_PAIRS, tm), lambda i: (0, i))
    return pl.pallas_call(
        functools.partial(_peer_out_kernel, tm=tm),
        grid=(n // tm,),
        in_specs=[pl.BlockSpec(memory_space=pl.ANY), col, col, blk, pl.BlockSpec(memory_space=pl.ANY)],
        out_specs=blk,
        out_shape=jax.ShapeDtypeStruct(x1_3.shape, F32),
        scratch_shapes=[pltpu.VMEM(tbl.shape, I32), pltpu.SMEM((2, tm, N_PAIRS), I32),
                        pltpu.VMEM((2, N_PAIRS, LANES), F32), pltpu.VMEM((2, N_PAIRS, LANES), F32),
                        pltpu.SemaphoreType.DMA((3,))],
        compiler_params=_cparams(("arbitrary",)),
        name="peer_out",
    )(rows_t, wlo, whi, x1_3, tbl)


def _final_kernel(x_ref, g_ref, o_ref):
    o_ref[...] = _rms(x_ref[...], g_ref[...])


def _final_norm(x2d, g, tm):
    n = x2d.shape[0]
    return pl.pallas_call(
        _final_kernel,
        grid=(n // tm,),
        in_specs=[pl.BlockSpec((tm, D_MODEL), lambda i: (i, 0)), pl.BlockSpec((1, D_MODEL), lambda i: (0, 0))],
        out_specs=pl.BlockSpec((tm, D_MODEL), lambda i: (i, 0)),
        out_shape=jax.ShapeDtypeStruct(x2d.shape, F32),
        compiler_params=_cparams(("arbitrary",)),
        name="final_norm",
    )(x2d, g)


def _block_diag_in(b):
    eye = jnp.eye(N_SSM_GROUPS, dtype=F32)
    return jnp.einsum('gnp,gh->gphn', b, eye).reshape(SSM_WIDTH, SSM_LANES)


def _block_diag_out(c):
    eye = jnp.eye(N_SSM_GROUPS, dtype=F32)
    return jnp.einsum('gpn,gh->gnhp', c, eye).reshape(SSM_LANES, SSM_WIDTH)


def _layer_weights(norm_mix, w_in, b_in, attn_sinks, lam_re, lam_im, log_step, b_re, b_im, c_re, c_im,
                   d_skip, w_glu, b_glu, norm_attn_out, norm_ssm_out, w_out, norm_ffn, w_query, sub_keys,
                   u_table, v_table):
    w = {}
    w['norm_mix'] = norm_mix.reshape(1, D_MODEL)
    w['w_in'] = _split(w_in)
    w['b_in'] = b_in.reshape(1, IN_WIDTH)
    w['sinks'] = attn_sinks
    ab_re, ab_im, z_re, z_im = _s5_params(lam_re, lam_im, log_step)
    w['s5'] = (list(_split(_block_diag_in(b_re))) + list(_split(_block_diag_in(b_im)))
               + [z_re, z_im, ab_re, ab_im,
                  _block_diag_out(c_re).astype(BF16), _block_diag_out(c_im).astype(BF16),
                  d_skip.reshape(1, SSM_WIDTH), w_glu.astype(BF16), b_glu.reshape(1, SSM_WIDTH),
                  norm_ssm_out.reshape(1, SSM_WIDTH)])
    w['norm_attn_out'] = norm_attn_out.reshape(1, ATTN_WIDTH)
    w['w_out_attn'] = w_out[:ATTN_WIDTH].astype(BF16)
    w['w_out_ssm'] = w_out[ATTN_WIDTH:].astype(BF16)
    w['norm_ffn'] = norm_ffn.reshape(1, D_MODEL)
    wq = w_query.reshape(D_MODEL, PEER_HEADS, PEER_QDIM).transpose(1, 0, 2)
    w['w_query'] = _split(wq)
    w['keys'] = sub_keys.reshape(PEER_HEADS * 2, N_KEYS, PEER_HALF)
    w['u_tbl'] = _pack_table(u_table, halves=True).reshape(-1, LANES)
    w['v_tbl'] = _pack_table(v_table, halves=False).reshape(-1, LANES)
    return w


def _peer(xn2d, x1_2d, w, tm_topk, tm_gather):
    n = xn2d.shape[0]
    rows, par, gate = _peer_topk(xn2d, w['w_query'][0], w['w_query'][1], w['keys'], tm_topk)
    rows_t = rows.T
    wlo, whi = _peer_act(rows_t, xn2d.reshape(n, SUBLANES, LANES), par, gate, w['u_tbl'], tm_gather)
    out = _peer_out(rows_t, wlo, whi, x1_2d.reshape(n, SUBLANES, LANES), w['v_tbl'], tm_gather)
    return out.reshape(n, D_MODEL)


def _token_tile(n, cap):
    t = cap
    while n % t:
        t //= 2
    return t


def _mix_and_ffn(x2d, attn_n, ssm_n, w, norm_final):
    n = x2d.shape[0]
    x1, xn2 = _out_proj(attn_n, ssm_n, x2d, w['w_out_attn'], w['w_out_ssm'], w['norm_ffn'], _token_tile(n, 512))
    x2 = _peer(xn2, x1, w, _token_tile(n, 256), _token_tile(n, 128))
    return x2


def _prompt_layer(x, w):
    b, t, _ = x.shape
    n = b * t
    x2d = x.reshape(n, D_MODEL)
    tm = _token_tile(t, 512)
    tables = _rope_tables(jnp.arange(t, dtype=I32))
    q, k, v, u = _in_proj(x2d, tables, t // tm, w['norm_mix'], w['w_in'][0], w['w_in'][1], w['b_in'], tm)
    k3 = k.reshape(b, t, KV_WIDTH)
    v3 = v.reshape(b, t, KV_WIDTH)
    (attn_n,) = _attention(w['sinks'], q.reshape(b, t, ATTN_WIDTH), k3, k3, v3, v3, w['norm_attn_out'], False)
    h0 = jnp.zeros((b, 1, SSM_LANES), F32)
    ssm_n, h_re, h_im = _s5(u.reshape(b, t, SSM_WIDTH), h0, h0, w['s5'], _token_tile(t, 256), 1)
    x2 = _mix_and_ffn(x2d, attn_n.reshape(n, ATTN_WIDTH), ssm_n.reshape(n, SSM_WIDTH), w, None)
    k_win = k3[:, -WINDOW:].reshape(b, WINDOW, N_KV_HEADS, HEAD_DIM)
    v_win = v3[:, -WINDOW:].reshape(b, WINDOW, N_KV_HEADS, HEAD_DIM)
    st = lambda h: h.reshape(b, N_SSM_GROUPS, SSM_STATE)
    return x2.reshape(b, t, D_MODEL), k_win, v_win, st(h_re), st(h_im)


def _sample_layer(x, k_buf, v_buf, h0_re, h0_im, w, past_len):
    b, t, _ = x.shape
    n = b * t
    x2d = x.reshape(n, D_MODEL)
    tm = _token_tile(n, 512)
    pos = past_len + jnp.arange(t, dtype=I32)
    tables = tuple(jnp.tile(a, (tm // t, 1)) for a in _rope_tables(pos))
    q, k, v, u = _in_proj(x2d, tables, 1, w['norm_mix'], w['w_in'][0], w['w_in'][1], w['b_in'], tm)
    pad = lambda a: jnp.pad(a.reshape(b, t, KV_WIDTH), ((0, 0), (0, WINDOW - t), (0, 0)))
    attn_n, k_win, v_win = _attention(w['sinks'], q.reshape(b, t, ATTN_WIDTH),
                                      k_buf.reshape(b, WINDOW, KV_WIDTH), pad(k),
                                      v_buf.reshape(b, WINDOW, KV_WIDTH), pad(v), w['norm_attn_out'], True)
    ns = min(b, S5_DECODE_SEQS)
    nbk = b // ns
    u_tm = u.reshape(nbk, ns, t, SSM_WIDTH).transpose(0, 2, 1, 3).reshape(nbk, t * ns, SSM_WIDTH)
    ssm_tm, h_re, h_im = _s5(u_tm, h0_re.reshape(nbk, ns, SSM_LANES), h0_im.reshape(nbk, ns, SSM_LANES),
                             w['s5'], t, ns)
    ssm_n = ssm_tm.reshape(nbk, t, ns, SSM_WIDTH).transpose(0, 2, 1, 3).reshape(n, SSM_WIDTH)
    x2 = _mix_and_ffn(x2d, attn_n.reshape(n, ATTN_WIDTH), ssm_n, w, None)
    win = lambda a: a.reshape(b, WINDOW, N_KV_HEADS, HEAD_DIM)
    st = lambda h: h.reshape(b, N_SSM_GROUPS, SSM_STATE)
    return x2.reshape(b, t, D_MODEL), win(k_win), win(v_win), st(h_re), st(h_im)


PAST_LEN = 16384
S5_DECODE_SEQS = 64


def kernel(x_prompt, x_sample, cache_k, cache_v, state_ssm_re, state_ssm_im, norm_mix, w_in, b_in, attn_sinks, ssm_lam_re, ssm_lam_im, ssm_log_step, ssm_b_re, ssm_b_im, ssm_c_re, ssm_c_im, ssm_d, ssm_w_glu, ssm_b_glu, norm_attn_out, norm_ssm_out, w_out, norm_ffn, peer_w_query, peer_sub_keys, peer_u, peer_v, norm_final):
    depth = norm_mix.shape[0]
    xp, xs = x_prompt, x_sample
    outs = [[] for _ in range(8)]
    for l in range(depth):
        w = _layer_weights(*[a[l] for a in (norm_mix, w_in, b_in, attn_sinks, ssm_lam_re, ssm_lam_im,
                                            ssm_log_step, ssm_b_re, ssm_b_im, ssm_c_re, ssm_c_im, ssm_d,
                                            ssm_w_glu, ssm_b_glu, norm_attn_out, norm_ssm_out, w_out,
                                            norm_ffn, peer_w_query, peer_sub_keys, peer_u, peer_v)])
        xs, k2, v2, r2, i2 = _sample_layer(xs, cache_k[l], cache_v[l], state_ssm_re[l], state_ssm_im[l],
                                           w, PAST_LEN)
        xp, k1, v1, r1, i1 = _prompt_layer(xp, w)
        for lst, a in zip(outs, (k1, v1, r1, i1, k2, v2, r2, i2)):
            lst.append(a)
    g = norm_final.reshape(1, D_MODEL)
    yp = _final_norm(xp.reshape(-1, D_MODEL), g, 512).reshape(xp.shape)
    ys = _final_norm(xs.reshape(-1, D_MODEL), g, 512).reshape(xs.shape)
    return (yp, ys) + tuple(jnp.stack(o) for o in outs)
```

```python
import functools
import math

import jax
import jax.numpy as jnp
from jax import lax
from jax.experimental import pallas as pl
from jax.experimental.pallas import tpu as pltpu

F32 = jnp.float32
BF16 = jnp.bfloat16
I32 = jnp.int32

D_MODEL = 1024
N_HEADS = 8
N_KV_HEADS = 2
HEAD_DIM = 64
ATTN_WIDTH = N_HEADS * HEAD_DIM
KV_WIDTH = N_KV_HEADS * HEAD_DIM
WINDOW = 128
ROT_DIM = HEAD_DIM // 4
ROPE_THETA = 500000.0
NEG_INF = -1e30
SSM_WIDTH = D_MODEL - ATTN_WIDTH
SSM_GROUP = 16
N_SSM_GROUPS = SSM_WIDTH // SSM_GROUP
SSM_STATE = 64
SSM_LANES = N_SSM_GROUPS * SSM_STATE
IN_WIDTH = ATTN_WIDTH + 2 * KV_WIDTH + SSM_WIDTH
PEER_HEADS = 8
N_KEYS = 128
PEER_TOPK = 16
PEER_QDIM = 256
PEER_HALF = PEER_QDIM // 2
N_PAIRS = PEER_HEADS * PEER_TOPK
RMS_EPS = 1e-5

LANES = 128
SUBLANES = 8
VMEM_LIMIT = 56 * 1024 * 1024

HI_MASK = -65536


def _cparams(sem):
    return pltpu.CompilerParams(dimension_semantics=sem, vmem_limit_bytes=VMEM_LIMIT)


def _split(w):
    hi = w.astype(BF16)
    lo = (w - hi.astype(F32)).astype(BF16)
    return hi, lo


def _dot1(a, b):
    return jnp.dot(a.astype(BF16), b, preferred_element_type=F32)


def _dot3(a, b_hi, b_lo):
    a_hi, a_lo = _split(a)
    d = functools.partial(jnp.dot, preferred_element_type=F32)
    return d(a_hi, b_hi) + (d(a_lo, b_hi) + d(a_hi, b_lo))


def _dot3_nt(a_hi, a_lo, b):
    b_hi, b_lo = _split(b)
    d = functools.partial(lax.dot_general, dimension_numbers=(((1,), (1,)), ((), ())),
                          preferred_element_type=F32)
    return d(a_hi, b_hi) + (d(a_lo, b_hi) + d(a_hi, b_lo))


def _gelu(x):
    return 0.5 * x * (1.0 + lax.erf(x * (2.0 ** -0.5)))


def _rms(x, g):
    return x * lax.rsqrt(jnp.mean(x * x, axis=-1, keepdims=True) + RMS_EPS) * g


def _inproj_kernel(x_ref, g_ref, whi_ref, wlo_ref, b_ref, c_ref, sa_ref, sb_ref,
                   q_ref, k_ref, v_ref, u_ref):
    xn = _rms(x_ref[...], g_ref[...])
    proj = _dot3(xn, whi_ref[...], wlo_ref[...]) + b_ref[...]
    c = c_ref[...]
    sa = sa_ref[...]
    sb = sb_ref[...]

    def rope(t):
        return t * c + pltpu.roll(t, LANES - ROT_DIM // 2, 1) * sa + pltpu.roll(t, ROT_DIM // 2, 1) * sb

    for j in range(ATTN_WIDTH // LANES):
        q_ref[:, j * LANES:(j + 1) * LANES] = rope(proj[:, j * LANES:(j + 1) * LANES])
    k_ref[...] = rope(proj[:, ATTN_WIDTH:ATTN_WIDTH + KV_WIDTH])
    v_ref[...] = proj[:, ATTN_WIDTH + KV_WIDTH:ATTN_WIDTH + 2 * KV_WIDTH]
    u_ref[...] = proj[:, ATTN_WIDTH + 2 * KV_WIDTH:]


def _rope_tables(pos):
    half = ROT_DIM // 2
    inv_freq = ROPE_THETA ** (-jnp.arange(half, dtype=F32) * 2.0 / ROT_DIM)
    ang = pos.astype(F32)[:, None] * inv_freq[None, :]
    cos = jnp.cos(ang)
    sin = jnp.sin(ang)
    t = pos.shape[0]
    one = jnp.ones((t, HEAD_DIM - ROT_DIM), F32)
    zero = jnp.zeros((t, HEAD_DIM - ROT_DIM), F32)
    zh = jnp.zeros((t, half), F32)
    c = jnp.concatenate([cos, cos, one], axis=1)
    sa = jnp.concatenate([-sin, zh, zero], axis=1)
    sb = jnp.concatenate([zh, sin, zero], axis=1)
    return tuple(jnp.tile(a, (1, LANES // HEAD_DIM)) for a in (c, sa, sb))


def _in_proj(x2d, pos_tables, n_pos_tiles, g, w_hi, w_lo, b, tm):
    n = x2d.shape[0]
    c, sa, sb = pos_tables
    row = lambda i: (i, 0)
    fixed = lambda i: (0, 0)
    pos_map = lambda i: (i % n_pos_tiles, 0)
    return pl.pallas_call(
        _inproj_kernel,
        grid=(n // tm,),
        in_specs=[pl.BlockSpec((tm, D_MODEL), row), pl.BlockSpec((1, D_MODEL), fixed),
                  pl.BlockSpec((D_MODEL, IN_WIDTH), fixed), pl.BlockSpec((D_MODEL, IN_WIDTH), fixed),
                  pl.BlockSpec((1, IN_WIDTH), fixed),
                  pl.BlockSpec((tm, LANES), pos_map), pl.BlockSpec((tm, LANES), pos_map),
                  pl.BlockSpec((tm, LANES), pos_map)],
        out_specs=[pl.BlockSpec((tm, ATTN_WIDTH), row), pl.BlockSpec((tm, KV_WIDTH), row),
                   pl.BlockSpec((tm, KV_WIDTH), row), pl.BlockSpec((tm, SSM_WIDTH), row)],
        out_shape=[jax.ShapeDtypeStruct((n, ATTN_WIDTH), F32), jax.ShapeDtypeStruct((n, KV_WIDTH), F32),
                   jax.ShapeDtypeStruct((n, KV_WIDTH), F32), jax.ShapeDtypeStruct((n, SSM_WIDTH), F32)],
        compiler_params=_cparams(("arbitrary",)),
        name="in_proj",
    )(x2d, g, w_hi, w_lo, b, c, sa, sb)


def _attn_kernel(sink_ref, q_ref, kp_ref, kc_ref, vp_ref, vc_ref, g_ref, o_ref, *win_refs, tq, decode):
    q = q_ref[0] * (HEAD_DIM ** -0.5)
    kp = kp_ref[0]
    kc = kc_ref[0]
    vp = vp_ref[0]
    vc = vc_ref[0]
    half = LANES // 2
    lane = lax.broadcasted_iota(I32, (1, LANES), 1)
    lo = lane < half
    qi = lax.broadcasted_iota(I32, (tq, WINDOW), 0)
    kj = lax.broadcasted_iota(I32, (tq, WINDOW), 1)
    first_off = 0 if decode else jnp.where(pl.program_id(1) > 0, 0, WINDOW)
    m_prev = kj > qi + first_off
    m_cur = kj <= qi

    def variants(t):
        r = pltpu.roll(t, half, 1)
        return [[t, r], [r, t]]

    kpv = [[a.astype(BF16) for a in row] for row in variants(kp)]
    kcv = [[a.astype(BF16) for a in row] for row in variants(kc)]
    lane_sel = [lo, jnp.logical_not(lo)]
    vpv = [[jnp.where(lane_sel[s], a, 0.0).astype(BF16) for s, a in enumerate(row)] for row in variants(vp)]
    vcv = [[jnp.where(lane_sel[s], a, 0.0).astype(BF16) for s, a in enumerate(row)] for row in variants(vc)]
    nt = functools.partial(lax.dot_general, dimension_numbers=(((1,), (1,)), ((), ())),
                           preferred_element_type=F32)
    outs = []
    for j in range(ATTN_WIDTH // LANES):
        g = (2 * j) // (N_HEADS // N_KV_HEADS)
        qt = q[:, j * LANES:(j + 1) * LANES]
        acc = jnp.zeros((tq, LANES), F32)
        for s in range(2):
            h = 2 * j + s
            qm = jnp.where(lane_sel[s], qt, 0.0).astype(BF16)
            sp = jnp.where(m_prev, nt(qm, kpv[g][s]), NEG_INF)
            sc = jnp.where(m_cur, nt(qm, kcv[g][s]), NEG_INF)
            sink = sink_ref[h]
            m = jnp.maximum(jnp.maximum(jnp.max(sp, axis=-1, keepdims=True),
                                        jnp.max(sc, axis=-1, keepdims=True)), sink)
            pp = jnp.exp(sp - m)
            pc = jnp.exp(sc - m)
            den = (jnp.sum(pp, axis=-1, keepdims=True) + jnp.sum(pc, axis=-1, keepdims=True)
                   + jnp.exp(sink - m))
            o = (jnp.dot(pp.astype(BF16), vpv[g][s], preferred_element_type=F32)
                 + jnp.dot(pc.astype(BF16), vcv[g][s], preferred_element_type=F32))
            acc = acc + o / den
        outs.append(acc)
    attn = jnp.concatenate(outs, axis=1)
    o_ref[0] = _rms(attn, g_ref[...])
    if decode:
        kw_ref, vw_ref = win_refs
        kw_ref[0, :WINDOW - tq] = kp[tq:]
        kw_ref[0, WINDOW - tq:] = kc[:tq]
        vw_ref[0, :WINDOW - tq] = vp[tq:]
        vw_ref[0, WINDOW - tq:] = vc[:tq]


def _attention(sinks, q3, k_prev, k_cur, v_prev, v_cur, g, decode):
    b, t, _ = q3.shape
    tq = t if decode else WINDOW
    nb = t // tq
    cur = lambda i, n: (i, n, 0)
    prev = (lambda i, n: (i, 0, 0)) if decode else (lambda i, n: (i, jnp.maximum(n - 1, 0), 0))
    kvb = (1, WINDOW, KV_WIDTH)
    out_specs = [pl.BlockSpec((1, tq, ATTN_WIDTH), cur)]
    out_shape = [jax.ShapeDtypeStruct((b, t, ATTN_WIDTH), F32)]
    if decode:
        out_specs += [pl.BlockSpec(kvb, cur), pl.BlockSpec(kvb, cur)]
        out_shape += [jax.ShapeDtypeStruct((b, WINDOW, KV_WIDTH), F32)] * 2
    return pl.pallas_call(
        functools.partial(_attn_kernel, tq=tq, decode=decode),
        grid=(b, nb),
        in_specs=[pl.BlockSpec(memory_space=pltpu.SMEM),
                  pl.BlockSpec((1, tq, ATTN_WIDTH), cur),
                  pl.BlockSpec(kvb, prev), pl.BlockSpec(kvb, cur),
                  pl.BlockSpec(kvb, prev), pl.BlockSpec(kvb, cur),
                  pl.BlockSpec((1, ATTN_WIDTH), lambda i, n: (0, 0))],
        out_specs=out_specs,
        out_shape=out_shape,
        compiler_params=_cparams(("arbitrary", "arbitrary")),
        name="attn_decode" if decode else "attn_prompt",
    )(sinks, q3, k_prev, k_cur, v_prev, v_cur, g)


def _s5_param_kernel(lr_ref, li_ref, ls_ref, abr_ref, abi_ref, zr_ref, zi_ref):
    lr = lr_ref[...]
    li = li_ref[...]
    dt = jnp.exp(ls_ref[...])
    mag = jnp.exp(lr * dt)
    ab_re = mag * jnp.cos(li * dt)
    ab_im = mag * jnp.sin(li * dt)
    den = lr * lr + li * li
    abr_ref[...] = ab_re
    abi_ref[...] = ab_im
    zr_ref[...] = ((ab_re - 1.0) * lr + ab_im * li) / den
    zi_ref[...] = (ab_im * lr - (ab_re - 1.0) * li) / den


def _s5_params(lam_re, lam_im, log_step):
    ls = jnp.broadcast_to(log_step[:, None], lam_re.shape)
    outs = pl.pallas_call(
        _s5_param_kernel,
        out_shape=[jax.ShapeDtypeStruct(lam_re.shape, F32)] * 4,
        name="s5_params",
    )(lam_re, lam_im, ls)
    return [o.reshape(1, SSM_LANES) for o in outs]


def _s5_kernel(u_ref, h0r_ref, h0i_ref, brh_ref, brl_ref, bih_ref, bil_ref, zr_ref, zi_ref,
               ar_ref, ai_ref, cr_ref, ci_ref, d_ref, wg_ref, bg_ref, g_ref,
               y_ref, hr_ref, hi_ref, sr_ref, si_ref, *, tt, ns):
    @pl.when(pl.program_id(1) == 0)
    def _():
        hr_ref[0] = h0r_ref[0]
        hi_ref[0] = h0i_ref[0]

    u = u_ref[0]
    pr = _dot3(u, brh_ref[...], brl_ref[...])
    pi = _dot3(u, bih_ref[...], bil_ref[...])
    zr = zr_ref[...]
    zi = zi_ref[...]
    sr_ref[...] = zr * pr - zi * pi
    si_ref[...] = zr * pi + zi * pr
    ar = jnp.broadcast_to(ar_ref[...], (ns, SSM_LANES))
    ai = jnp.broadcast_to(ai_ref[...], (ns, SSM_LANES))

    if ns <= SUBLANES:
        def body(t, carry):
            hr, hi = carry
            rows = pl.ds(t * ns, ns)
            nr = ar * hr - ai * hi + sr_ref[rows, :]
            ni = ar * hi + ai * hr + si_ref[rows, :]
            sr_ref[rows, :] = nr
            si_ref[rows, :] = ni
            return nr, ni

        hr, hi = lax.fori_loop(0, tt, body, (hr_ref[0], hi_ref[0]), unroll=8)
        hr_ref[0] = hr
        hi_ref[0] = hi
    else:
        def body(t, carry):
            rows = pl.ds(pl.multiple_of(t * ns, ns), ns)
            hr = hr_ref[0]
            hi = hi_ref[0]
            nr = ar * hr - ai * hi + sr_ref[rows, :]
            ni = ar * hi + ai * hr + si_ref[rows, :]
            sr_ref[rows, :] = nr
            si_ref[rows, :] = ni
            hr_ref[0] = nr
            hi_ref[0] = ni
            return carry

        lax.fori_loop(0, tt, body, 0)

    y = _dot1(sr_ref[...], cr_ref[...]) - _dot1(si_ref[...], ci_ref[...]) + d_ref[...] * u
    y = _gelu(y)
    gate = _dot1(y, wg_ref[...]) + bg_ref[...]
    y = y * (1.0 / (1.0 + jnp.exp(-gate)))
    y_ref[0] = _rms(y, g_ref[...])


def _s5(u3, h0r, h0i, consts, tt, ns):
    nb, rows, _ = u3.shape
    r = tt * ns
    fixed = lambda b, t: (0, 0)
    tile = lambda b, t: (b, t, 0)
    seq = lambda b, t: (b, 0, 0)
    const_specs = [pl.BlockSpec(c.shape, fixed) for c in consts]
    return pl.pallas_call(
        functools.partial(_s5_kernel, tt=tt, ns=ns),
        grid=(nb, rows // r),
        in_specs=[pl.BlockSpec((1, r, SSM_WIDTH), tile),
                  pl.BlockSpec((1, ns, SSM_LANES), seq), pl.BlockSpec((1, ns, SSM_LANES), seq)] + const_specs,
        out_specs=[pl.BlockSpec((1, r, SSM_WIDTH), tile),
                   pl.BlockSpec((1, ns, SSM_LANES), seq), pl.BlockSpec((1, ns, SSM_LANES), seq)],
        out_shape=[jax.ShapeDtypeStruct((nb, rows, SSM_WIDTH), F32),
                   jax.ShapeDtypeStruct((nb, ns, SSM_LANES), F32),
                   jax.ShapeDtypeStruct((nb, ns, SSM_LANES), F32)],
        scratch_shapes=[pltpu.VMEM((r, SSM_LANES), F32), pltpu.VMEM((r, SSM_LANES), F32)],
        compiler_params=_cparams(("arbitrary", "arbitrary")),
        name="s5_ns%d" % ns,
    )(u3, h0r, h0i, *consts)


def _outproj_kernel(a_ref, s_ref, x_ref, wa_ref, ws_ref, g_ref, x1_ref, xn_ref):
    x1 = x_ref[...] + _dot1(a_ref[...], wa_ref[...]) + _dot1(s_ref[...], ws_ref[...])
    x1_ref[...] = x1
    xn_ref[...] = _rms(x1, g_ref[...])


def _out_proj(attn_n, ssm_n, x2d, w_attn, w_ssm, g, tm):
    n = x2d.shape[0]
    row = lambda i: (i, 0)
    fixed = lambda i: (0, 0)
    return pl.pallas_call(
        _outproj_kernel,
        grid=(n // tm,),
        in_specs=[pl.BlockSpec((tm, ATTN_WIDTH), row), pl.BlockSpec((tm, SSM_WIDTH), row),
                  pl.BlockSpec((tm, D_MODEL), row),
                  pl.BlockSpec((ATTN_WIDTH, D_MODEL), fixed), pl.BlockSpec((SSM_WIDTH, D_MODEL), fixed),
                  pl.BlockSpec((1, D_MODEL), fixed)],
        out_specs=[pl.BlockSpec((tm, D_MODEL), row), pl.BlockSpec((tm, D_MODEL), row)],
        out_shape=[jax.ShapeDtypeStruct((n, D_MODEL), F32)] * 2,
        compiler_params=_cparams(("arbitrary",)),
        name="out_proj",
    )(attn_n, ssm_n, x2d, w_attn, w_ssm, g)


def _topk_rows(s, ids, k):
    vals, picks = [], []
    sentinel = jnp.iinfo(jnp.int32).max
    for _ in range(k):
        m = jnp.max(s, axis=0, keepdims=True)
        pick = jnp.min(jnp.where(s == m, ids, sentinel), axis=0, keepdims=True)
        vals.append(m)
        picks.append(pick)
        s = jnp.where(ids == pick, -jnp.inf, s)
    return vals, picks


def _cand_layout():
    pieces = [(0, 1, 0, PEER_TOPK)]
    pieces += [(a, a + 1, 0, SUBLANES) for a in range(1, SUBLANES)]
    pieces += [(SUBLANES, PEER_TOPK, 0, 1)]
    return pieces


def _topk_kernel(xn_ref, wqh_ref, wql_ref, keys_ref, row_ref, shift_ref, gate_ref, *, tm):
    h = pl.program_id(1)
    q = _dot3(xn_ref[...], wqh_ref[h], wql_ref[h])
    key_id = lax.broadcasted_iota(I32, (N_KEYS, tm), 0)
    sub_v, sub_i = [], []
    for c in range(2):
        q_hi, q_lo = _split(q[:, c * PEER_HALF:(c + 1) * PEER_HALF])
        keys = keys_ref[2 * h + c]
        k_hi, k_lo = _split(keys)
        d = functools.partial(lax.dot_general, dimension_numbers=(((1,), (1,)), ((), ())),
                              preferred_element_type=F32)
        s = d(k_hi, q_hi) + (d(k_lo, q_hi) + d(k_hi, q_lo))
        vals, picks = _topk_rows(s, key_id, PEER_TOPK)
        sub_v.append(vals)
        sub_i.append(picks)
    cs, ce, cf = [], [], []

    def rows_of(lst, lo, hi):
        return lst[lo] if hi - lo == 1 else jnp.concatenate(lst[lo:hi], axis=0)

    for a_lo, a_hi, b_lo, b_hi in _cand_layout():
        na, nbb = a_hi - a_lo, b_hi - b_lo
        rows = max(na, nbb)
        cs.append(rows_of(sub_v[0], a_lo, a_hi) + rows_of(sub_v[1], b_lo, b_hi))
        ce.append(rows_of(sub_i[0], a_lo, a_hi) * N_KEYS + rows_of(sub_i[1], b_lo, b_hi))
        r = lax.broadcasted_iota(I32, (rows, tm), 0)
        cf.append((a_lo + r) * PEER_TOPK + b_lo if na > 1 else a_lo * PEER_TOPK + b_lo + r)
    cand_s = jnp.concatenate(cs, axis=0)
    cand_e = jnp.concatenate(ce, axis=0)
    cand_f = jnp.concatenate(cf, axis=0)
    best_v, best_e = [], []
    sentinel = jnp.iinfo(jnp.int32).max
    for _ in range(PEER_TOPK):
        m = jnp.max(cand_s, axis=0, keepdims=True)
        f = jnp.min(jnp.where(cand_s == m, cand_f, sentinel), axis=0, keepdims=True)
        hit = cand_f == f
        best_v.append(m)
        best_e.append(jnp.max(jnp.where(hit, cand_e, -1), axis=0, keepdims=True))
        cand_s = jnp.where(hit, -jnp.inf, cand_s)
    bv = jnp.concatenate(best_v, axis=0)
    be = jnp.concatenate(best_e, axis=0)
    ex = jnp.exp(bv - bv[0:1])
    gate_ref[...] = ex / jnp.sum(ex, axis=0, keepdims=True)
    row_ref[...] = lax.shift_right_logical(be, 1) * SUBLANES
    shift_ref[...] = ((1 - (be & 1)) * 16).astype(F32)


def _peer_topk(xn2d, wq_hi, wq_lo, keys, tm):
    n = xn2d.shape[0]
    out = lambda i, h: (h, i)
    return pl.pallas_call(
        functools.partial(_topk_kernel, tm=tm),
        grid=(n // tm, PEER_HEADS),
        in_specs=[pl.BlockSpec((tm, D_MODEL), lambda i, h: (i, 0)),
                  pl.BlockSpec(wq_hi.shape, lambda i, h: (0, 0, 0)),
                  pl.BlockSpec(wq_lo.shape, lambda i, h: (0, 0, 0)),
                  pl.BlockSpec(keys.shape, lambda i, h: (0, 0, 0))],
        out_specs=[pl.BlockSpec((PEER_TOPK, tm), out)] * 3,
        out_shape=[jax.ShapeDtypeStruct((N_PAIRS, n), I32), jax.ShapeDtypeStruct((N_PAIRS, n), F32),
                   jax.ShapeDtypeStruct((N_PAIRS, n), F32)],
        compiler_params=_cparams(("arbitrary", "arbitrary")),
        name="peer_topk",
    )(xn2d, wq_hi, wq_lo, keys)


def _pack_kernel(t_ref, o_ref):
    even = t_ref[:, 0].astype(BF16).astype(F32)
    odd = t_ref[:, 1].astype(BF16).astype(F32)
    hi = pltpu.bitcast(odd, I32) & HI_MASK
    lo = lax.shift_right_logical(pltpu.bitcast(even, I32), 16)
    o_ref[...] = hi | lo


def _pack_table(table, tb=256):
    e = table.shape[0]
    t4 = table.reshape(e // 2, 2, SUBLANES, LANES)
    packed = pl.pallas_call(
        _pack_kernel,
        grid=(e // 2 // tb,),
        in_specs=[pl.BlockSpec((tb, 2, SUBLANES, LANES), lambda i: (i, 0, 0, 0))],
        out_specs=pl.BlockSpec((tb, SUBLANES, LANES), lambda i: (i, 0, 0)),
        out_shape=jax.ShapeDtypeStruct((e // 2, SUBLANES, LANES), I32),
        compiler_params=_cparams(("arbitrary",)),
        name="pack_table",
    )(t4)
    return packed.reshape(e // 2 * SUBLANES, LANES)


def _tile(tbl, offset):
    return tbl[pl.ds(pl.multiple_of(offset, SUBLANES), SUBLANES), :]


def _expert_row(word, shift):
    return pltpu.bitcast((word << shift) & HI_MASK, F32)


def _row8(ref, k):
    return jnp.broadcast_to(ref[k:k + 1, :], (SUBLANES, LANES))


def _fold8(p):
    sub = lax.broadcasted_iota(I32, (SUBLANES, LANES), 0)
    step = SUBLANES // 2
    while step >= 1:
        first = (sub % (2 * step)) < step
        n = len(p) // 2
        p = [jnp.where(first, p[j] + pltpu.roll(p[j], SUBLANES - step, 0),
                       p[j + n] + pltpu.roll(p[j + n], step, 0)) for j in range(n)]
        step //= 2
    return p[0]


def _column(block_ref, tile, tm, t):
    blk = block_ref[:, tile * tm:(tile + 1) * tm]
    lane = lax.broadcasted_iota(I32, blk.shape, 1)
    col = jnp.sum(jnp.where(lane == t, blk, 0.0), axis=1, keepdims=True)
    return jnp.broadcast_to(col, (N_PAIRS, LANES))


def _rows_copy(row_hbm, buf, sem, tile, tm):
    return pltpu.make_async_copy(row_hbm.at[:, pl.ds(tile * tm, tm)], buf, sem)


def _for_each_tile(row_hbm, bufs, sems, tm, body):
    i = pl.program_id(0)
    tps = len(bufs)
    first = i * tps
    total = pl.num_programs(0) * tps

    def copy(j, tile):
        return _rows_copy(row_hbm, bufs[j], sems.at[j], tile, tm)

    @pl.when(i == 0)
    def _():
        copy(0, 0).start()

    for j in range(tps):
        nxt = (j + 1) % tps
        if tps == 1:
            copy(0, first).wait()
            body(0, bufs[0])

            @pl.when(first + 1 < total)
            def _():
                copy(0, first + 1).start()
        else:
            @pl.when(first + j + 1 < total)
            def _():
                copy(nxt, first + j + 1).start()

            copy(j, first + j).wait()
            body(j, bufs[j])


def _load_table(tbl_hbm, tbl_vmem, sem):
    @pl.when(pl.program_id(0) == 0)
    def _():
        c = pltpu.make_async_copy(tbl_hbm, tbl_vmem, sem)
        c.start()
        c.wait()


def _tiles_per_step(n, tm):
    return 2 if (n // tm) % 2 == 0 else 1


def _peer_act_kernel(row_hbm, x_ref, shift_ref, gate_ref, tbl_hbm, w_ref, tbl, *scratch, tm, tps):
    bufs = scratch[:tps]
    sh_a, sh_b, acc, sems = scratch[tps:]
    _load_table(tbl_hbm, tbl, sems.at[tps])
    lane = lax.broadcasted_iota(I32, (SUBLANES, tm), 1)
    n_groups = N_PAIRS // SUBLANES

    def tile_body(tile, rows):
        def expand(t, sh):
            sh[...] = _column(shift_ref, tile, tm, t).astype(I32)

        def gather(t, sh):
            xt = x_ref[tile * tm + t]
            here = lane == t
            for g in range(n_groups):
                prods = []
                for j in range(SUBLANES):
                    k = g * SUBLANES + j
                    prods.append(_expert_row(_tile(tbl, rows.at[k][t]), _row8(sh, k)) * xt)
                col = jnp.sum(_fold8(prods), axis=1, keepdims=True)
                acc[g] = jnp.where(here, col, acc[g])

        acc[...] = jnp.zeros(acc.shape, F32)
        expand(0, sh_a)

        def two_tokens(j, carry):
            t = 2 * j
            gather(t, sh_a)
            expand(t + 1, sh_b)
            gather(t + 1, sh_b)
            expand(jnp.minimum(t + 2, tm - 1), sh_a)
            return carry

        lax.fori_loop(0, tm // 2, two_tokens, 0)
        act = jnp.concatenate([acc[g] for g in range(n_groups)], axis=0)
        cols = slice(tile * tm, (tile + 1) * tm)
        w_ref[:, cols] = gate_ref[:, cols] * _gelu(act)

    _for_each_tile(row_hbm, bufs, sems, tm, tile_body)


def _peer_act(rows, xn3, shift, gate, tbl, tm):
    n = xn3.shape[0]
    tps = _tiles_per_step(n, tm)
    step = tps * tm
    col = pl.BlockSpec((N_PAIRS, step), lambda i: (0, i))
    return pl.pallas_call(
        functools.partial(_peer_act_kernel, tm=tm, tps=tps),
        grid=(n // step,),
        in_specs=[pl.BlockSpec(memory_space=pl.ANY),
                  pl.BlockSpec((step, SUBLANES, LANES), lambda i: (i, 0, 0)),
                  col, col, pl.BlockSpec(memory_space=pl.ANY)],
        out_specs=col,
        out_shape=jax.ShapeDtypeStruct((N_PAIRS, n), F32),
        scratch_shapes=([pltpu.VMEM(tbl.shape, I32)] + [pltpu.SMEM((N_PAIRS, tm), I32)] * tps
                        + [pltpu.VMEM((N_PAIRS, LANES), I32), pltpu.VMEM((N_PAIRS, LANES), I32),
                           pltpu.VMEM((N_PAIRS // SUBLANES, SUBLANES, tm), F32),
                           pltpu.SemaphoreType.DMA((tps + 1,))]),
        compiler_params=_cparams(("arbitrary",)),
        name="peer_act",
    )(rows, xn3, shift, gate, tbl)


def _peer_out_kernel(row_hbm, shift_ref, w_ref, x1_ref, tbl_hbm, o_ref, tbl, *scratch, tm, tps):
    bufs = scratch[:tps]
    sh_a, sh_b, wb_a, wb_b, sems = scratch[tps:]
    _load_table(tbl_hbm, tbl, sems.at[tps])
    n_acc = 4

    def tile_body(tile, rows):
        def expand(t, sh, wb):
            sh[...] = _column(shift_ref, tile, tm, t).astype(I32)
            wb[...] = _column(w_ref, tile, tm, t)

        def gather(t, sh, wb):
            accs = [x1_ref[tile * tm + t]] + [jnp.zeros((SUBLANES, LANES), F32)] * (n_acc - 1)
            for k in range(N_PAIRS):
                v_row = _expert_row(_tile(tbl, rows.at[k][t]), _row8(sh, k))
                accs[k % n_acc] = accs[k % n_acc] + _row8(wb, k) * v_row
            o_ref[tile * tm + t] = (accs[0] + accs[1]) + (accs[2] + accs[3])

        expand(0, sh_a, wb_a)

        def two_tokens(j, carry):
            t = 2 * j
            gather(t, sh_a, wb_a)
            expand(t + 1, sh_b, wb_b)
            gather(t + 1, sh_b, wb_b)
            expand(jnp.minimum(t + 2, tm - 1), sh_a, wb_a)
            return carry

        lax.fori_loop(0, tm // 2, two_tokens, 0)

    _for_each_tile(row_hbm, bufs, sems, tm, tile_body)


def _peer_out(rows, shift, w, x1_3, tbl, tm):
    n = x1_3.shape[0]
    tps = _tiles_per_step(n, tm)
    step = tps * tm
    blk = pl.BlockSpec((step, SUBLANES, LANES), lambda i: (i, 0, 0))
    col = pl.BlockSpec((N_PAIRS, step), lambda i: (0, i))
    return pl.pallas_call(
        functools.partial(_peer_out_kernel, tm=tm, tps=tps),
        grid=(n // step,),
        in_specs=[pl.BlockSpec(memory_space=pl.ANY), col, col, blk, pl.BlockSpec(memory_space=pl.ANY)],
        out_specs=blk,
        out_shape=jax.ShapeDtypeStruct(x1_3.shape, F32),
        scratch_shapes=([pltpu.VMEM(tbl.shape, I32)] + [pltpu.SMEM((N_PAIRS, tm), I32)] * tps
                        + [pltpu.VMEM((N_PAIRS, LANES), I32), pltpu.VMEM((N_PAIRS, LANES), I32),
                           pltpu.VMEM((N_PAIRS, LANES), F32), pltpu.VMEM((N_PAIRS, LANES), F32),
                           pltpu.SemaphoreType.DMA((tps + 1,))]),
        compiler_params=_cparams(("arbitrary",)),
        name="peer_out",
    )(rows, shift, w, x1_3, tbl)


def _final_kernel(x_ref, g_ref, o_ref):
    o_ref[...] = _rms(x_ref[...], g_ref[...])


def _final_norm(x2d, g, tm):
    n = x2d.shape[0]
    return pl.pallas_call(
        _final_kernel,
        grid=(n // tm,),
        in_specs=[pl.BlockSpec((tm, D_MODEL), lambda i: (i, 0)), pl.BlockSpec((1, D_MODEL), lambda i: (0, 0))],
        out_specs=pl.BlockSpec((tm, D_MODEL), lambda i: (i, 0)),
        out_shape=jax.ShapeDtypeStruct(x2d.shape, F32),
        compiler_params=_cparams(("arbitrary",)),
        name="final_norm",
    )(x2d, g)


def _block_diag_in(b):
    eye = jnp.eye(N_SSM_GROUPS, dtype=F32)
    return jnp.einsum('gnp,gh->gphn', b, eye).reshape(SSM_WIDTH, SSM_LANES)


def _block_diag_out(c):
    eye = jnp.eye(N_SSM_GROUPS, dtype=F32)
    return jnp.einsum('gpn,gh->gnhp', c, eye).reshape(SSM_LANES, SSM_WIDTH)


def _layer_weights(norm_mix, w_in, b_in, attn_sinks, lam_re, lam_im, log_step, b_re, b_im, c_re, c_im,
                   d_skip, w_glu, b_glu, norm_attn_out, norm_ssm_out, w_out, norm_ffn, w_query, sub_keys,
                   u_table, v_table):
    w = {}
    w['norm_mix'] = norm_mix.reshape(1, D_MODEL)
    w['w_in'] = _split(w_in)
    w['b_in'] = b_in.reshape(1, IN_WIDTH)
    w['sinks'] = attn_sinks
    ab_re, ab_im, z_re, z_im = _s5_params(lam_re, lam_im, log_step)
    w['s5'] = (list(_split(_block_diag_in(b_re))) + list(_split(_block_diag_in(b_im)))
               + [z_re, z_im, ab_re, ab_im,
                  _block_diag_out(c_re).astype(BF16), _block_diag_out(c_im).astype(BF16),
                  d_skip.reshape(1, SSM_WIDTH), w_glu.astype(BF16), b_glu.reshape(1, SSM_WIDTH),
                  norm_ssm_out.reshape(1, SSM_WIDTH)])
    w['norm_attn_out'] = norm_attn_out.reshape(1, ATTN_WIDTH)
    w['w_out_attn'] = w_out[:ATTN_WIDTH].astype(BF16)
    w['w_out_ssm'] = w_out[ATTN_WIDTH:].astype(BF16)
    w['norm_ffn'] = norm_ffn.reshape(1, D_MODEL)
    wq = w_query.reshape(D_MODEL, PEER_HEADS, PEER_QDIM).transpose(1, 0, 2)
    w['w_query'] = _split(wq)
    w['keys'] = sub_keys.reshape(PEER_HEADS * 2, N_KEYS, PEER_HALF)
    w['u_tbl'] = _pack_table(u_table)
    w['v_tbl'] = _pack_table(v_table)
    return w


def _peer(xn2d, x1_2d, w, tm_topk, tm_gather):
    n = xn2d.shape[0]
    rows, shift, gate = _peer_topk(xn2d, w['w_query'][0], w['w_query'][1], w['keys'], tm_topk)
    wts = _peer_act(rows, xn2d.reshape(n, SUBLANES, LANES), shift, gate, w['u_tbl'], tm_gather)
    out = _peer_out(rows, shift, wts, x1_2d.reshape(n, SUBLANES, LANES), w['v_tbl'], tm_gather)
    return out.reshape(n, D_MODEL)


def _token_tile(n, cap):
    t = cap
    while n % t:
        t //= 2
    return t


def _mix_and_ffn(x2d, attn_n, ssm_n, w, norm_final):
    n = x2d.shape[0]
    x1, xn2 = _out_proj(attn_n, ssm_n, x2d, w['w_out_attn'], w['w_out_ssm'], w['norm_ffn'], _token_tile(n, 512))
    x2 = _peer(xn2, x1, w, _token_tile(n, 256), _token_tile(n, 128))
    return x2


def _prompt_layer(x, w):
    b, t, _ = x.shape
    n = b * t
    x2d = x.reshape(n, D_MODEL)
    tm = _token_tile(t, 512)
    tables = _rope_tables(jnp.arange(t, dtype=I32))
    q, k, v, u = _in_proj(x2d, tables, t // tm, w['norm_mix'], w['w_in'][0], w['w_in'][1], w['b_in'], tm)
    k3 = k.reshape(b, t, KV_WIDTH)
    v3 = v.reshape(b, t, KV_WIDTH)
    (attn_n,) = _attention(w['sinks'], q.reshape(b, t, ATTN_WIDTH), k3, k3, v3, v3, w['norm_attn_out'], False)
    h0 = jnp.zeros((b, 1, SSM_LANES), F32)
    ssm_n, h_re, h_im = _s5(u.reshape(b, t, SSM_WIDTH), h0, h0, w['s5'], _token_tile(t, 256), 1)
    x2 = _mix_and_ffn(x2d, attn_n.reshape(n, ATTN_WIDTH), ssm_n.reshape(n, SSM_WIDTH), w, None)
    k_win = k3[:, -WINDOW:].reshape(b, WINDOW, N_KV_HEADS, HEAD_DIM)
    v_win = v3[:, -WINDOW:].reshape(b, WINDOW, N_KV_HEADS, HEAD_DIM)
    st = lambda h: h.reshape(b, N_SSM_GROUPS, SSM_STATE)
    return x2.reshape(b, t, D_MODEL), k_win, v_win, st(h_re), st(h_im)


def _sample_layer(x, k_buf, v_buf, h0_re, h0_im, w, past_len):
    b, t, _ = x.shape
    n = b * t
    x2d = x.reshape(n, D_MODEL)
    tm = _token_tile(n, 512)
    pos = past_len + jnp.arange(t, dtype=I32)
    tables = tuple(jnp.tile(a, (tm // t, 1)) for a in _rope_tables(pos))
    q, k, v, u = _in_proj(x2d, tables, 1, w['norm_mix'], w['w_in'][0], w['w_in'][1], w['b_in'], tm)
    pad = lambda a: jnp.pad(a.reshape(b, t, KV_WIDTH), ((0, 0), (0, WINDOW - t), (0, 0)))
    attn_n, k_win, v_win = _attention(w['sinks'], q.reshape(b, t, ATTN_WIDTH),
                                      k_buf.reshape(b, WINDOW, KV_WIDTH), pad(k),
                                      v_buf.reshape(b, WINDOW, KV_WIDTH), pad(v), w['norm_attn_out'], True)
    ns = min(b, S5_DECODE_SEQS)
    nbk = b // ns
    u_tm = u.reshape(nbk, ns, t, SSM_WIDTH).transpose(0, 2, 1, 3).reshape(nbk, t * ns, SSM_WIDTH)
    ssm_tm, h_re, h_im = _s5(u_tm, h0_re.reshape(nbk, ns, SSM_LANES), h0_im.reshape(nbk, ns, SSM_LANES),
                             w['s5'], t, ns)
    ssm_n = ssm_tm.reshape(nbk, t, ns, SSM_WIDTH).transpose(0, 2, 1, 3).reshape(n, SSM_WIDTH)
    x2 = _mix_and_ffn(x2d, attn_n.reshape(n, ATTN_WIDTH), ssm_n, w, None)
    win = lambda a: a.reshape(b, WINDOW, N_KV_HEADS, HEAD_DIM)
    st = lambda h: h.reshape(b, N_SSM_GROUPS, SSM_STATE)
    return x2.reshape(b, t, D_MODEL), win(k_win), win(v_win), st(h_re), st(h_im)


PAST_LEN = 16384
S5_DECODE_SEQS = 64


def kernel(x_prompt, x_sample, cache_k, cache_v, state_ssm_re, state_ssm_im, norm_mix, w_in, b_in, attn_sinks, ssm_lam_re, ssm_lam_im, ssm_log_step, ssm_b_re, ssm_b_im, ssm_c_re, ssm_c_im, ssm_d, ssm_w_glu, ssm_b_glu, norm_attn_out, norm_ssm_out, w_out, norm_ffn, peer_w_query, peer_sub_keys, peer_u, peer_v, norm_final):
    depth = norm_mix.shape[0]
    xp, xs = x_prompt, x_sample
    outs = [[] for _ in range(8)]
    for l in range(depth):
        w = _layer_weights(*[a[l] for a in (norm_mix, w_in, b_in, attn_sinks, ssm_lam_re, ssm_lam_im,
                                            ssm_log_step, ssm_b_re, ssm_b_im, ssm_c_re, ssm_c_im, ssm_d,
                                            ssm_w_glu, ssm_b_glu, norm_attn_out, norm_ssm_out, w_out,
                                            norm_ffn, peer_w_query, peer_sub_keys, peer_u, peer_v)])
        xs, k2, v2, r2, i2 = _sample_layer(xs, cache_k[l], cache_v[l], state_ssm_re[l], state_ssm_im[l],
                                           w, PAST_LEN)
        xp, k1, v1, r1, i1 = _prompt_layer(xp, w)
        for lst, a in zip(outs, (k1, v1, r1, i1, k2, v2, r2, i2)):
            lst.append(a)
    g = norm_final.reshape(1, D_MODEL)
    yp = _final_norm(xp.reshape(-1, D_MODEL), g, 512).reshape(xp.shape)
    ys = _final_norm(xs.reshape(-1, D_MODEL), g, 512).reshape(xs.shape)
    return (yp, ys) + tuple(jnp.stack(o) for o in outs)
```

```python
import functools
import math

import jax
import jax.numpy as jnp
from jax import lax
from jax.experimental import pallas as pl
from jax.experimental.pallas import tpu as pltpu

F32 = jnp.float32
BF16 = jnp.bfloat16
I32 = jnp.int32

D_MODEL = 1024
N_HEADS = 8
N_KV_HEADS = 2
HEAD_DIM = 64
ATTN_WIDTH = N_HEADS * HEAD_DIM
KV_WIDTH = N_KV_HEADS * HEAD_DIM
WINDOW = 128
ROT_DIM = HEAD_DIM // 4
ROPE_THETA = 500000.0
NEG_INF = -1e30
SSM_WIDTH = D_MODEL - ATTN_WIDTH
SSM_GROUP = 16
N_SSM_GROUPS = SSM_WIDTH // SSM_GROUP
SSM_STATE = 64
SSM_LANES = N_SSM_GROUPS * SSM_STATE
IN_WIDTH = ATTN_WIDTH + 2 * KV_WIDTH + SSM_WIDTH
PEER_HEADS = 8
N_KEYS = 128
PEER_TOPK = 16
PEER_QDIM = 256
PEER_HALF = PEER_QDIM // 2
N_PAIRS = PEER_HEADS * PEER_TOPK
RMS_EPS = 1e-5

LANES = 128
SUBLANES = 8
VMEM_LIMIT = 56 * 1024 * 1024

HI_MASK = -65536

S5_CHUNK_GROUPS = LANES // SSM_GROUP
S5_CHUNKS = N_SSM_GROUPS // S5_CHUNK_GROUPS
S5_CHUNK_LANES = S5_CHUNK_GROUPS * SSM_STATE


def _cparams(sem):
    return pltpu.CompilerParams(dimension_semantics=sem, vmem_limit_bytes=VMEM_LIMIT)


def _split(w):
    hi = w.astype(BF16)
    lo = (w - hi.astype(F32)).astype(BF16)
    return hi, lo


def _dot1(a, b):
    return jnp.dot(a.astype(BF16), b, preferred_element_type=F32)


def _dot3(a, b_hi, b_lo):
    a_hi, a_lo = _split(a)
    d = functools.partial(jnp.dot, preferred_element_type=F32)
    return d(a_hi, b_hi) + (d(a_lo, b_hi) + d(a_hi, b_lo))


def _dot3_nt(a_hi, a_lo, b):
    b_hi, b_lo = _split(b)
    d = functools.partial(lax.dot_general, dimension_numbers=(((1,), (1,)), ((), ())),
                          preferred_element_type=F32)
    return d(a_hi, b_hi) + (d(a_lo, b_hi) + d(a_hi, b_lo))


def _gelu(x):
    return 0.5 * x * (1.0 + lax.erf(x * (2.0 ** -0.5)))


def _rms(x, g):
    return x * lax.rsqrt(jnp.mean(x * x, axis=-1, keepdims=True) + RMS_EPS) * g


def _inproj_kernel(x_ref, g_ref, whi_ref, wlo_ref, b_ref, c_ref, sa_ref, sb_ref,
                   q_ref, k_ref, v_ref, u_ref):
    xn = _rms(x_ref[...], g_ref[...])
    proj = _dot3(xn, whi_ref[...], wlo_ref[...]) + b_ref[...]
    c = c_ref[...]
    sa = sa_ref[...]
    sb = sb_ref[...]

    def rope(t):
        return t * c + pltpu.roll(t, LANES - ROT_DIM // 2, 1) * sa + pltpu.roll(t, ROT_DIM // 2, 1) * sb

    for j in range(ATTN_WIDTH // LANES):
        q_ref[:, j * LANES:(j + 1) * LANES] = rope(proj[:, j * LANES:(j + 1) * LANES])
    k_ref[...] = rope(proj[:, ATTN_WIDTH:ATTN_WIDTH + KV_WIDTH])
    v_ref[...] = proj[:, ATTN_WIDTH + KV_WIDTH:ATTN_WIDTH + 2 * KV_WIDTH]
    u_ref[...] = proj[:, ATTN_WIDTH + 2 * KV_WIDTH:]


def _rope_tables(pos):
    half = ROT_DIM // 2
    inv_freq = ROPE_THETA ** (-jnp.arange(half, dtype=F32) * 2.0 / ROT_DIM)
    ang = pos.astype(F32)[:, None] * inv_freq[None, :]
    cos = jnp.cos(ang)
    sin = jnp.sin(ang)
    t = pos.shape[0]
    one = jnp.ones((t, HEAD_DIM - ROT_DIM), F32)
    zero = jnp.zeros((t, HEAD_DIM - ROT_DIM), F32)
    zh = jnp.zeros((t, half), F32)
    c = jnp.concatenate([cos, cos, one], axis=1)
    sa = jnp.concatenate([-sin, zh, zero], axis=1)
    sb = jnp.concatenate([zh, sin, zero], axis=1)
    return tuple(jnp.tile(a, (1, LANES // HEAD_DIM)) for a in (c, sa, sb))


def _in_proj(x2d, pos_tables, n_pos_tiles, g, w_hi, w_lo, b, tm):
    n = x2d.shape[0]
    c, sa, sb = pos_tables
    row = lambda i: (i, 0)
    fixed = lambda i: (0, 0)
    pos_map = lambda i: (i % n_pos_tiles, 0)
    return pl.pallas_call(
        _inproj_kernel,
        grid=(n // tm,),
        in_specs=[pl.BlockSpec((tm, D_MODEL), row), pl.BlockSpec((1, D_MODEL), fixed),
                  pl.BlockSpec((D_MODEL, IN_WIDTH), fixed), pl.BlockSpec((D_MODEL, IN_WIDTH), fixed),
                  pl.BlockSpec((1, IN_WIDTH), fixed),
                  pl.BlockSpec((tm, LANES), pos_map), pl.BlockSpec((tm, LANES), pos_map),
                  pl.BlockSpec((tm, LANES), pos_map)],
        out_specs=[pl.BlockSpec((tm, ATTN_WIDTH), row), pl.BlockSpec((tm, KV_WIDTH), row),
                   pl.BlockSpec((tm, KV_WIDTH), row), pl.BlockSpec((tm, SSM_WIDTH), row)],
        out_shape=[jax.ShapeDtypeStruct((n, ATTN_WIDTH), F32), jax.ShapeDtypeStruct((n, KV_WIDTH), F32),
                   jax.ShapeDtypeStruct((n, KV_WIDTH), F32), jax.ShapeDtypeStruct((n, SSM_WIDTH), F32)],
        compiler_params=_cparams(("arbitrary",)),
        name="in_proj",
    )(x2d, g, w_hi, w_lo, b, c, sa, sb)


def _attn_kernel(sink_ref, q_ref, kp_ref, kc_ref, vp_ref, vc_ref, g_ref, o_ref, *win_refs, tq, decode, nseq):
    for b in range(nseq):
        _attn_one(sink_ref, q_ref.at[b], kp_ref.at[b], kc_ref.at[b], vp_ref.at[b], vc_ref.at[b], g_ref,
                  o_ref.at[b], *[w.at[b] for w in win_refs], tq=tq, decode=decode)


def _attn_one(sink_ref, q_ref, kp_ref, kc_ref, vp_ref, vc_ref, g_ref, o_ref, *win_refs, tq, decode):
    q = q_ref[...] * (HEAD_DIM ** -0.5)
    kp = kp_ref[...]
    kc = kc_ref[...]
    vp = vp_ref[...]
    vc = vc_ref[...]
    half = LANES // 2
    lane = lax.broadcasted_iota(I32, (1, LANES), 1)
    lo = lane < half
    qi = lax.broadcasted_iota(I32, (tq, WINDOW), 0)
    kj = lax.broadcasted_iota(I32, (tq, WINDOW), 1)
    first_off = 0 if decode else jnp.where(pl.program_id(1) > 0, 0, WINDOW)
    m_prev = kj > qi + first_off
    m_cur = kj <= qi

    def variants(t):
        r = pltpu.roll(t, half, 1)
        return [[t, r], [r, t]]

    kpv = [[a.astype(BF16) for a in row] for row in variants(kp)]
    kcv = [[a.astype(BF16) for a in row] for row in variants(kc)]
    lane_sel = [lo, jnp.logical_not(lo)]
    vpv = [[jnp.where(lane_sel[s], a, 0.0).astype(BF16) for s, a in enumerate(row)] for row in variants(vp)]
    vcv = [[jnp.where(lane_sel[s], a, 0.0).astype(BF16) for s, a in enumerate(row)] for row in variants(vc)]
    nt = functools.partial(lax.dot_general, dimension_numbers=(((1,), (1,)), ((), ())),
                           preferred_element_type=F32)
    outs = []
    for j in range(ATTN_WIDTH // LANES):
        g = (2 * j) // (N_HEADS // N_KV_HEADS)
        qt = q[:, j * LANES:(j + 1) * LANES]
        acc = jnp.zeros((tq, LANES), F32)
        for s in range(2):
            h = 2 * j + s
            qm = jnp.where(lane_sel[s], qt, 0.0).astype(BF16)
            sp = jnp.where(m_prev, nt(qm, kpv[g][s]), NEG_INF)
            sc = jnp.where(m_cur, nt(qm, kcv[g][s]), NEG_INF)
            sink = sink_ref[h]
            m = jnp.maximum(jnp.maximum(jnp.max(sp, axis=-1, keepdims=True),
                                        jnp.max(sc, axis=-1, keepdims=True)), sink)
            pp = jnp.exp(sp - m)
            pc = jnp.exp(sc - m)
            den = (jnp.sum(pp, axis=-1, keepdims=True) + jnp.sum(pc, axis=-1, keepdims=True)
                   + jnp.exp(sink - m))
            o = (jnp.dot(pp.astype(BF16), vpv[g][s], preferred_element_type=F32)
                 + jnp.dot(pc.astype(BF16), vcv[g][s], preferred_element_type=F32))
            acc = acc + o / den
        outs.append(acc)
    attn = jnp.concatenate(outs, axis=1)
    o_ref[...] = _rms(attn, g_ref[...])
    if decode:
        kw_ref, vw_ref = win_refs
        kw_ref[:WINDOW - tq] = kp[tq:]
        kw_ref[WINDOW - tq:] = kc[:tq]
        vw_ref[:WINDOW - tq] = vp[tq:]
        vw_ref[WINDOW - tq:] = vc[:tq]


def _attention(sinks, q3, k_prev, k_cur, v_prev, v_cur, g, decode):
    b, t, _ = q3.shape
    tq = t if decode else WINDOW
    nb = t // tq
    nseq = math.gcd(b, DECODE_SEQS_PER_STEP) if decode else 1
    cur = lambda i, n: (i, n, 0)
    prev = (lambda i, n: (i, 0, 0)) if decode else (lambda i, n: (i, jnp.maximum(n - 1, 0), 0))
    kvb = (nseq, WINDOW, KV_WIDTH)
    out_specs = [pl.BlockSpec((nseq, tq, ATTN_WIDTH), cur)]
    out_shape = [jax.ShapeDtypeStruct((b, t, ATTN_WIDTH), F32)]
    if decode:
        out_specs += [pl.BlockSpec(kvb, cur), pl.BlockSpec(kvb, cur)]
        out_shape += [jax.ShapeDtypeStruct((b, WINDOW, KV_WIDTH), F32)] * 2
    return pl.pallas_call(
        functools.partial(_attn_kernel, tq=tq, decode=decode, nseq=nseq),
        grid=(b // nseq, nb),
        in_specs=[pl.BlockSpec(memory_space=pltpu.SMEM),
                  pl.BlockSpec((nseq, tq, ATTN_WIDTH), cur),
                  pl.BlockSpec(kvb, prev), pl.BlockSpec(kvb, cur),
                  pl.BlockSpec(kvb, prev), pl.BlockSpec(kvb, cur),
                  pl.BlockSpec((1, ATTN_WIDTH), lambda i, n: (0, 0))],
        out_specs=out_specs,
        out_shape=out_shape,
        compiler_params=_cparams(("arbitrary", "arbitrary")),
        name="attn_decode" if decode else "attn_prompt",
    )(sinks, q3, k_prev, k_cur, v_prev, v_cur, g)


def _s5_param_kernel(lr_ref, li_ref, ls_ref, abr_ref, abi_ref, zr_ref, zi_ref):
    lr = lr_ref[...]
    li = li_ref[...]
    dt = jnp.exp(ls_ref[...])
    mag = jnp.exp(lr * dt)
    ab_re = mag * jnp.cos(li * dt)
    ab_im = mag * jnp.sin(li * dt)
    den = lr * lr + li * li
    abr_ref[...] = ab_re
    abi_ref[...] = ab_im
    zr_ref[...] = ((ab_re - 1.0) * lr + ab_im * li) / den
    zi_ref[...] = (ab_im * lr - (ab_re - 1.0) * li) / den


def _s5_params(lam_re, lam_im, log_step):
    ls = jnp.broadcast_to(log_step[:, None], lam_re.shape)
    outs = pl.pallas_call(
        _s5_param_kernel,
        out_shape=[jax.ShapeDtypeStruct(lam_re.shape, F32)] * 4,
        name="s5_params",
    )(lam_re, lam_im, ls)
    return [o.reshape(1, SSM_LANES) for o in outs]


def _s5_kernel(u_ref, h0r_ref, h0i_ref, brh_ref, brl_ref, bih_ref, bil_ref, zr_ref, zi_ref,
               ar_ref, ai_ref, cr_ref, ci_ref, d_ref, wg_ref, bg_ref, g_ref,
               y_ref, hr_ref, hi_ref, sr_ref, si_ref, *, tt, ns):
    @pl.when(pl.program_id(1) == 0)
    def _():
        hr_ref[0] = h0r_ref[0]
        hi_ref[0] = h0i_ref[0]

    u = u_ref[0]
    for c in range(S5_CHUNKS):
        uc = u[:, c * LANES:(c + 1) * LANES]
        lanes = slice(c * S5_CHUNK_LANES, (c + 1) * S5_CHUNK_LANES)
        pr = _dot3(uc, brh_ref[c], brl_ref[c])
        pi = _dot3(uc, bih_ref[c], bil_ref[c])
        zr = zr_ref[:, lanes]
        zi = zi_ref[:, lanes]
        sr_ref[:, lanes] = zr * pr - zi * pi
        si_ref[:, lanes] = zr * pi + zi * pr
    ar = jnp.broadcast_to(ar_ref[...], (ns, SSM_LANES))
    ai = jnp.broadcast_to(ai_ref[...], (ns, SSM_LANES))

    if ns <= SUBLANES:
        def body(t, carry):
            hr, hi = carry
            rows = pl.ds(t * ns, ns)
            nr = ar * hr - ai * hi + sr_ref[rows, :]
            ni = ar * hi + ai * hr + si_ref[rows, :]
            sr_ref[rows, :] = nr
            si_ref[rows, :] = ni
            return nr, ni

        hr, hi = lax.fori_loop(0, tt, body, (hr_ref[0], hi_ref[0]), unroll=8)
        hr_ref[0] = hr
        hi_ref[0] = hi
    else:
        def body(t, carry):
            rows = pl.ds(pl.multiple_of(t * ns, ns), ns)
            hr = hr_ref[0]
            hi = hi_ref[0]
            nr = ar * hr - ai * hi + sr_ref[rows, :]
            ni = ar * hi + ai * hr + si_ref[rows, :]
            sr_ref[rows, :] = nr
            si_ref[rows, :] = ni
            hr_ref[0] = nr
            hi_ref[0] = ni
            return carry

        lax.fori_loop(0, tt, body, 0)

    ch = []
    for c in range(S5_CHUNKS):
        lanes = slice(c * S5_CHUNK_LANES, (c + 1) * S5_CHUNK_LANES)
        ch.append(_dot1(sr_ref[:, lanes], cr_ref[c]) - _dot1(si_ref[:, lanes], ci_ref[c]))
    y = jnp.concatenate(ch, axis=1) + d_ref[...] * u
    y = _gelu(y)
    gate = _dot1(y, wg_ref[...]) + bg_ref[...]
    y = y * (1.0 / (1.0 + jnp.exp(-gate)))
    y_ref[0] = _rms(y, g_ref[...])


def _s5(u3, h0r, h0i, consts, tt, ns):
    nb, rows, _ = u3.shape
    r = tt * ns
    fixed = lambda b, t: (0, 0)
    tile = lambda b, t: (b, t, 0)
    seq = lambda b, t: (b, 0, 0)
    const_specs = [pl.BlockSpec(c.shape, lambda b, t, nd=c.ndim: (0,) * nd) for c in consts]
    return pl.pallas_call(
        functools.partial(_s5_kernel, tt=tt, ns=ns),
        grid=(nb, rows // r),
        in_specs=[pl.BlockSpec((1, r, SSM_WIDTH), tile),
                  pl.BlockSpec((1, ns, SSM_LANES), seq), pl.BlockSpec((1, ns, SSM_LANES), seq)] + const_specs,
        out_specs=[pl.BlockSpec((1, r, SSM_WIDTH), tile),
                   pl.BlockSpec((1, ns, SSM_LANES), seq), pl.BlockSpec((1, ns, SSM_LANES), seq)],
        out_shape=[jax.ShapeDtypeStruct((nb, rows, SSM_WIDTH), F32),
                   jax.ShapeDtypeStruct((nb, ns, SSM_LANES), F32),
                   jax.ShapeDtypeStruct((nb, ns, SSM_LANES), F32)],
        scratch_shapes=[pltpu.VMEM((r, SSM_LANES), F32), pltpu.VMEM((r, SSM_LANES), F32)],
        compiler_params=_cparams(("arbitrary", "arbitrary")),
        name="s5_ns%d" % ns,
    )(u3, h0r, h0i, *consts)


def _outproj_kernel(a_ref, s_ref, x_ref, wa_ref, ws_ref, g_ref, x1_ref, xn_ref):
    x1 = x_ref[...] + _dot1(a_ref[...], wa_ref[...]) + _dot1(s_ref[...], ws_ref[...])
    x1_ref[...] = x1
    xn_ref[...] = _rms(x1, g_ref[...])


def _out_proj(attn_n, ssm_n, x2d, w_attn, w_ssm, g, tm):
    n = x2d.shape[0]
    row = lambda i: (i, 0)
    fixed = lambda i: (0, 0)
    return pl.pallas_call(
        _outproj_kernel,
        grid=(n // tm,),
        in_specs=[pl.BlockSpec((tm, ATTN_WIDTH), row), pl.BlockSpec((tm, SSM_WIDTH), row),
                  pl.BlockSpec((tm, D_MODEL), row),
                  pl.BlockSpec((ATTN_WIDTH, D_MODEL), fixed), pl.BlockSpec((SSM_WIDTH, D_MODEL), fixed),
                  pl.BlockSpec((1, D_MODEL), fixed)],
        out_specs=[pl.BlockSpec((tm, D_MODEL), row), pl.BlockSpec((tm, D_MODEL), row)],
        out_shape=[jax.ShapeDtypeStruct((n, D_MODEL), F32)] * 2,
        compiler_params=_cparams(("arbitrary",)),
        name="out_proj",
    )(attn_n, ssm_n, x2d, w_attn, w_ssm, g)


def _topk_rows(s, ids, k):
    vals, picks = [], []
    sentinel = jnp.iinfo(jnp.int32).max
    for _ in range(k):
        m = jnp.max(s, axis=0, keepdims=True)
        pick = jnp.min(jnp.where(s == m, ids, sentinel), axis=0, keepdims=True)
        vals.append(m)
        picks.append(pick)
        s = jnp.where(ids == pick, -jnp.inf, s)
    return vals, picks


def _cand_layout():
    pieces = [(0, 1, 0, PEER_TOPK)]
    pieces += [(a, a + 1, 0, SUBLANES) for a in range(1, SUBLANES)]
    pieces += [(SUBLANES, PEER_TOPK, 0, 1)]
    return pieces


def _topk_kernel(xn_ref, wqh_ref, wql_ref, keys_ref, row_ref, shift_ref, gate_ref, *, tm):
    h = pl.program_id(1)
    q = _dot3(xn_ref[...], wqh_ref[h], wql_ref[h])
    key_id = lax.broadcasted_iota(I32, (N_KEYS, tm), 0)
    sub_v, sub_i = [], []
    for c in range(2):
        q_hi, q_lo = _split(q[:, c * PEER_HALF:(c + 1) * PEER_HALF])
        keys = keys_ref[2 * h + c]
        k_hi, k_lo = _split(keys)
        d = functools.partial(lax.dot_general, dimension_numbers=(((1,), (1,)), ((), ())),
                              preferred_element_type=F32)
        s = d(k_hi, q_hi) + (d(k_lo, q_hi) + d(k_hi, q_lo))
        vals, picks = _topk_rows(s, key_id, PEER_TOPK)
        sub_v.append(vals)
        sub_i.append(picks)
    cs, ce, cf = [], [], []

    def rows_of(lst, lo, hi):
        return lst[lo] if hi - lo == 1 else jnp.concatenate(lst[lo:hi], axis=0)

    for a_lo, a_hi, b_lo, b_hi in _cand_layout():
        na, nbb = a_hi - a_lo, b_hi - b_lo
        rows = max(na, nbb)
        cs.append(rows_of(sub_v[0], a_lo, a_hi) + rows_of(sub_v[1], b_lo, b_hi))
        ce.append(rows_of(sub_i[0], a_lo, a_hi) * N_KEYS + rows_of(sub_i[1], b_lo, b_hi))
        r = lax.broadcasted_iota(I32, (rows, tm), 0)
        cf.append((a_lo + r) * PEER_TOPK + b_lo if na > 1 else a_lo * PEER_TOPK + b_lo + r)
    cand_s = jnp.concatenate(cs, axis=0)
    cand_e = jnp.concatenate(ce, axis=0)
    cand_f = jnp.concatenate(cf, axis=0)
    best_v, best_e = [], []
    sentinel = jnp.iinfo(jnp.int32).max
    for _ in range(PEER_TOPK):
        m = jnp.max(cand_s, axis=0, keepdims=True)
        f = jnp.min(jnp.where(cand_s == m, cand_f, sentinel), axis=0, keepdims=True)
        hit = cand_f == f
        best_v.append(m)
        best_e.append(jnp.max(jnp.where(hit, cand_e, -1), axis=0, keepdims=True))
        cand_s = jnp.where(hit, -jnp.inf, cand_s)
    bv = jnp.concatenate(best_v, axis=0)
    be = jnp.concatenate(best_e, axis=0)
    ex = jnp.exp(bv - bv[0:1])
    gate_ref[...] = ex / jnp.sum(ex, axis=0, keepdims=True)
    row_ref[...] = lax.shift_right_logical(be, 1) * SUBLANES
    shift_ref[...] = ((1 - (be & 1)) * 16).astype(F32)


def _peer_topk(xn2d, wq_hi, wq_lo, keys, tm):
    n = xn2d.shape[0]
    out = lambda i, h: (h, i)
    return pl.pallas_call(
        functools.partial(_topk_kernel, tm=tm),
        grid=(n // tm, PEER_HEADS),
        in_specs=[pl.BlockSpec((tm, D_MODEL), lambda i, h: (i, 0)),
                  pl.BlockSpec(wq_hi.shape, lambda i, h: (0, 0, 0)),
                  pl.BlockSpec(wq_lo.shape, lambda i, h: (0, 0, 0)),
                  pl.BlockSpec(keys.shape, lambda i, h: (0, 0, 0))],
        out_specs=[pl.BlockSpec((PEER_TOPK, tm), out)] * 3,
        out_shape=[jax.ShapeDtypeStruct((N_PAIRS, n), I32), jax.ShapeDtypeStruct((N_PAIRS, n), F32),
                   jax.ShapeDtypeStruct((N_PAIRS, n), F32)],
        compiler_params=_cparams(("arbitrary", "arbitrary")),
        name="peer_topk",
    )(xn2d, wq_hi, wq_lo, keys)


def _pack_kernel(t_ref, o_ref):
    even = t_ref[:, 0].astype(BF16).astype(F32)
    odd = t_ref[:, 1].astype(BF16).astype(F32)
    hi = pltpu.bitcast(odd, I32) & HI_MASK
    lo = lax.shift_right_logical(pltpu.bitcast(even, I32), 16)
    o_ref[...] = hi | lo


def _pack_table(table, tb=256):
    e = table.shape[0]
    t4 = table.reshape(e // 2, 2, SUBLANES, LANES)
    packed = pl.pallas_call(
        _pack_kernel,
        grid=(e // 2 // tb,),
        in_specs=[pl.BlockSpec((tb, 2, SUBLANES, LANES), lambda i: (i, 0, 0, 0))],
        out_specs=pl.BlockSpec((tb, SUBLANES, LANES), lambda i: (i, 0, 0)),
        out_shape=jax.ShapeDtypeStruct((e // 2, SUBLANES, LANES), I32),
        compiler_params=_cparams(("arbitrary",)),
        name="pack_table",
    )(t4)
    return packed.reshape(e // 2 * SUBLANES, LANES)


def _tile(tbl, offset):
    return tbl[pl.ds(pl.multiple_of(offset, SUBLANES), SUBLANES), :]


def _expert_row(word, shift):
    return pltpu.bitcast((word << shift) & HI_MASK, F32)


def _row8(ref, k):
    return jnp.broadcast_to(ref[k:k + 1, :], (SUBLANES, LANES))


def _fold(p, steps):
    sub = lax.broadcasted_iota(I32, p[0].shape, 0)
    for step in steps:
        first = (sub % (2 * step)) < step
        n = len(p) // 2
        p = [jnp.where(first, p[j] + pltpu.roll(p[j], SUBLANES - step, 0),
                       p[j + n] + pltpu.roll(p[j + n], step, 0)) for j in range(n)]
    return p


def _column(block_ref, tile, tm, t):
    blk = block_ref[:, tile * tm:(tile + 1) * tm]
    lane = lax.broadcasted_iota(I32, blk.shape, 1)
    col = jnp.sum(jnp.where(lane == t, blk, 0.0), axis=1, keepdims=True)
    return jnp.broadcast_to(col, (N_PAIRS, LANES))


def _rows_copy(row_hbm, buf, sem, tile, tm):
    return pltpu.make_async_copy(row_hbm.at[:, pl.ds(tile * tm, tm)], buf, sem)


def _for_each_tile(row_hbm, bufs, sems, tm, body):
    i = pl.program_id(0)
    tps = len(bufs)
    first = i * tps
    total = pl.num_programs(0) * tps

    def copy(j, tile):
        return _rows_copy(row_hbm, bufs[j], sems.at[j], tile, tm)

    @pl.when(i == 0)
    def _():
        copy(0, 0).start()

    for j in range(tps):
        nxt = (j + 1) % tps
        if tps == 1:
            copy(0, first).wait()
            body(0, bufs[0])

            @pl.when(first + 1 < total)
            def _():
                copy(0, first + 1).start()
        else:
            @pl.when(first + j + 1 < total)
            def _():
                copy(nxt, first + j + 1).start()

            copy(j, first + j).wait()
            body(j, bufs[j])


def _load_table(tbl_hbm, tbl_vmem, sem):
    @pl.when(pl.program_id(0) == 0)
    def _():
        c = pltpu.make_async_copy(tbl_hbm, tbl_vmem, sem)
        c.start()
        c.wait()


def _tiles_per_step(n, tm):
    return 2 if (n // tm) % 2 == 0 else 1


def _peer_act_kernel(row_hbm, x_ref, shift_ref, gate_ref, tbl_hbm, w_ref, tbl, *scratch, tm, tps):
    bufs = scratch[:tps]
    sh_a, sh_b, acc, sems = scratch[tps:]
    _load_table(tbl_hbm, tbl, sems.at[tps])
    lane = lax.broadcasted_iota(I32, (SUBLANES, tm), 1)
    n_groups = N_PAIRS // SUBLANES

    def tile_body(tile, rows):
        def expand(t, sh):
            sh[...] = _column(shift_ref, tile, tm, t).astype(I32)

        def gather(t, sh):
            xt = x_ref[tile * tm + t]
            here = lane == t
            for g in range(n_groups):
                prods = []
                for j in range(SUBLANES):
                    k = g * SUBLANES + j
                    prods.append(_expert_row(_tile(tbl, rows.at[k][t]), _row8(sh, k)) * xt)
                folded = _fold(prods, (4, 2, 1))[0]
                acc[g] = jnp.where(here, jnp.sum(folded, axis=1, keepdims=True), acc[g])

        acc[...] = jnp.zeros(acc.shape, F32)
        expand(0, sh_a)

        def two_tokens(j, carry):
            t = 2 * j
            gather(t, sh_a)
            expand(t + 1, sh_b)
            gather(t + 1, sh_b)
            expand(jnp.minimum(t + 2, tm - 1), sh_a)
            return carry

        lax.fori_loop(0, tm // 2, two_tokens, 0)
        act = jnp.concatenate([acc[g] for g in range(n_groups)], axis=0)
        cols = slice(tile * tm, (tile + 1) * tm)
        w_ref[:, cols] = gate_ref[:, cols] * _gelu(act)

    _for_each_tile(row_hbm, bufs, sems, tm, tile_body)


def _peer_act(rows, xn3, shift, gate, tbl, tm):
    n = xn3.shape[0]
    tps = _tiles_per_step(n, tm)
    step = tps * tm
    col = pl.BlockSpec((N_PAIRS, step), lambda i: (0, i))
    return pl.pallas_call(
        functools.partial(_peer_act_kernel, tm=tm, tps=tps),
        grid=(n // step,),
        in_specs=[pl.BlockSpec(memory_space=pl.ANY),
                  pl.BlockSpec((step, SUBLANES, LANES), lambda i: (i, 0, 0)),
                  col, col, pl.BlockSpec(memory_space=pl.ANY)],
        out_specs=col,
        out_shape=jax.ShapeDtypeStruct((N_PAIRS, n), F32),
        scratch_shapes=([pltpu.VMEM(tbl.shape, I32)] + [pltpu.SMEM((N_PAIRS, tm), I32)] * tps
                        + [pltpu.VMEM((N_PAIRS, LANES), I32), pltpu.VMEM((N_PAIRS, LANES), I32),
                           pltpu.VMEM((N_PAIRS // SUBLANES, SUBLANES, tm), F32),
                           pltpu.SemaphoreType.DMA((tps + 1,))]),
        compiler_params=_cparams(("arbitrary",)),
        name="peer_act",
    )(rows, xn3, shift, gate, tbl)


def _peer_out_kernel(row_hbm, shift_ref, w_ref, x1_ref, tbl_hbm, o_ref, tbl, *scratch, tm, tps):
    bufs = scratch[:tps]
    sh_a, sh_b, wb_a, wb_b, sems = scratch[tps:]
    _load_table(tbl_hbm, tbl, sems.at[tps])
    n_acc = 4

    def tile_body(tile, rows):
        def expand(t, sh, wb):
            sh[...] = _column(shift_ref, tile, tm, t).astype(I32)
            wb[...] = _column(w_ref, tile, tm, t)

        def gather(t, sh, wb):
            accs = [x1_ref[tile * tm + t]] + [jnp.zeros((SUBLANES, LANES), F32)] * (n_acc - 1)
            for k in range(N_PAIRS):
                v_row = _expert_row(_tile(tbl, rows.at[k][t]), _row8(sh, k))
                accs[k % n_acc] = accs[k % n_acc] + _row8(wb, k) * v_row
            o_ref[tile * tm + t] = (accs[0] + accs[1]) + (accs[2] + accs[3])

        expand(0, sh_a, wb_a)

        def two_tokens(j, carry):
            t = 2 * j
            gather(t, sh_a, wb_a)
            expand(t + 1, sh_b, wb_b)
            gather(t + 1, sh_b, wb_b)
            expand(jnp.minimum(t + 2, tm - 1), sh_a, wb_a)
            return carry

        lax.fori_loop(0, tm // 2, two_tokens, 0)

    _for_each_tile(row_hbm, bufs, sems, tm, tile_body)


def _peer_out(rows, shift, w, x1_3, tbl, tm):
    n = x1_3.shape[0]
    tps = _tiles_per_step(n, tm)
    step = tps * tm
    blk = pl.BlockSpec((step, SUBLANES, LANES), lambda i: (i, 0, 0))
    col = pl.BlockSpec((N_PAIRS, step), lambda i: (0, i))
    return pl.pallas_call(
        functools.partial(_peer_out_kernel, tm=tm, tps=tps),
        grid=(n // step,),
        in_specs=[pl.BlockSpec(memory_space=pl.ANY), col, col, blk, pl.BlockSpec(memory_space=pl.ANY)],
        out_specs=blk,
        out_shape=jax.ShapeDtypeStruct(x1_3.shape, F32),
        scratch_shapes=([pltpu.VMEM(tbl.shape, I32)] + [pltpu.SMEM((N_PAIRS, tm), I32)] * tps
                        + [pltpu.VMEM((N_PAIRS, LANES), I32), pltpu.VMEM((N_PAIRS, LANES), I32),
                           pltpu.VMEM((N_PAIRS, LANES), F32), pltpu.VMEM((N_PAIRS, LANES), F32),
                           pltpu.SemaphoreType.DMA((tps + 1,))]),
        compiler_params=_cparams(("arbitrary",)),
        name="peer_out",
    )(rows, shift, w, x1_3, tbl)


def _final_kernel(x_ref, g_ref, o_ref):
    o_ref[...] = _rms(x_ref[...], g_ref[...])


def _final_norm(x2d, g, tm):
    n = x2d.shape[0]
    return pl.pallas_call(
        _final_kernel,
        grid=(n // tm,),
        in_specs=[pl.BlockSpec((tm, D_MODEL), lambda i: (i, 0)), pl.BlockSpec((1, D_MODEL), lambda i: (0, 0))],
        out_specs=pl.BlockSpec((tm, D_MODEL), lambda i: (i, 0)),
        out_shape=jax.ShapeDtypeStruct(x2d.shape, F32),
        compiler_params=_cparams(("arbitrary",)),
        name="final_norm",
    )(x2d, g)


def _block_diag_in(b):
    eye = jnp.eye(S5_CHUNK_GROUPS, dtype=F32)
    b4 = b.reshape(S5_CHUNKS, S5_CHUNK_GROUPS, SSM_STATE, SSM_GROUP)
    return jnp.einsum('cgnp,gh->cgphn', b4, eye).reshape(S5_CHUNKS, LANES, S5_CHUNK_LANES)


def _block_diag_out(c):
    eye = jnp.eye(S5_CHUNK_GROUPS, dtype=F32)
    c4 = c.reshape(S5_CHUNKS, S5_CHUNK_GROUPS, SSM_GROUP, SSM_STATE)
    return jnp.einsum('cgpn,gh->cgnhp', c4, eye).reshape(S5_CHUNKS, S5_CHUNK_LANES, LANES)


def _layer_weights(norm_mix, w_in, b_in, attn_sinks, lam_re, lam_im, log_step, b_re, b_im, c_re, c_im,
                   d_skip, w_glu, b_glu, norm_attn_out, norm_ssm_out, w_out, norm_ffn, w_query, sub_keys,
                   u_table, v_table):
    w = {}
    w['norm_mix'] = norm_mix.reshape(1, D_MODEL)
    w['w_in'] = _split(w_in)
    w['b_in'] = b_in.reshape(1, IN_WIDTH)
    w['sinks'] = attn_sinks
    ab_re, ab_im, z_re, z_im = _s5_params(lam_re, lam_im, log_step)
    w['s5'] = (list(_split(_block_diag_in(b_re))) + list(_split(_block_diag_in(b_im)))
               + [z_re, z_im, ab_re, ab_im,
                  _block_diag_out(c_re).astype(BF16), _block_diag_out(c_im).astype(BF16),
                  d_skip.reshape(1, SSM_WIDTH), w_glu.astype(BF16), b_glu.reshape(1, SSM_WIDTH),
                  norm_ssm_out.reshape(1, SSM_WIDTH)])
    w['norm_attn_out'] = norm_attn_out.reshape(1, ATTN_WIDTH)
    w['w_out_attn'] = w_out[:ATTN_WIDTH].astype(BF16)
    w['w_out_ssm'] = w_out[ATTN_WIDTH:].astype(BF16)
    w['norm_ffn'] = norm_ffn.reshape(1, D_MODEL)
    wq = w_query.reshape(D_MODEL, PEER_HEADS, PEER_QDIM).transpose(1, 0, 2)
    w['w_query'] = _split(wq)
    w['keys'] = sub_keys.reshape(PEER_HEADS * 2, N_KEYS, PEER_HALF)
    w['u_tbl'] = _pack_table(u_table)
    w['v_tbl'] = _pack_table(v_table)
    return w


def _peer(xn2d, x1_2d, w, tm_topk, tm_gather):
    n = xn2d.shape[0]
    rows, shift, gate = _peer_topk(xn2d, w['w_query'][0], w['w_query'][1], w['keys'], tm_topk)
    wts = _peer_act(rows, xn2d.reshape(n, SUBLANES, LANES), shift, gate, w['u_tbl'], tm_gather)
    out = _peer_out(rows, shift, wts, x1_2d.reshape(n, SUBLANES, LANES), w['v_tbl'], tm_gather)
    return out.reshape(n, D_MODEL)


def _token_tile(n, cap):
    t = cap
    while n % t:
        t //= 2
    return t


def _mix_and_ffn(x2d, attn_n, ssm_n, w, norm_final):
    n = x2d.shape[0]
    x1, xn2 = _out_proj(attn_n, ssm_n, x2d, w['w_out_attn'], w['w_out_ssm'], w['norm_ffn'], _token_tile(n, 512))
    x2 = _peer(xn2, x1, w, _token_tile(n, 256), _token_tile(n, 128))
    return x2


def _prompt_layer(x, w):
    b, t, _ = x.shape
    n = b * t
    x2d = x.reshape(n, D_MODEL)
    tm = _token_tile(t, 512)
    tables = _rope_tables(jnp.arange(t, dtype=I32))
    q, k, v, u = _in_proj(x2d, tables, t // tm, w['norm_mix'], w['w_in'][0], w['w_in'][1], w['b_in'], tm)
    k3 = k.reshape(b, t, KV_WIDTH)
    v3 = v.reshape(b, t, KV_WIDTH)
    (attn_n,) = _attention(w['sinks'], q.reshape(b, t, ATTN_WIDTH), k3, k3, v3, v3, w['norm_attn_out'], False)
    h0 = jnp.zeros((b, 1, SSM_LANES), F32)
    ssm_n, h_re, h_im = _s5(u.reshape(b, t, SSM_WIDTH), h0, h0, w['s5'], _token_tile(t, 256), 1)
    x2 = _mix_and_ffn(x2d, attn_n.reshape(n, ATTN_WIDTH), ssm_n.reshape(n, SSM_WIDTH), w, None)
    k_win = k3[:, -WINDOW:].reshape(b, WINDOW, N_KV_HEADS, HEAD_DIM)
    v_win = v3[:, -WINDOW:].reshape(b, WINDOW, N_KV_HEADS, HEAD_DIM)
    st = lambda h: h.reshape(b, N_SSM_GROUPS, SSM_STATE)
    return x2.reshape(b, t, D_MODEL), k_win, v_win, st(h_re), st(h_im)


def _sample_layer(x, k_buf, v_buf, h0_re, h0_im, w, past_len):
    b, t, _ = x.shape
    n = b * t
    x2d = x.reshape(n, D_MODEL)
    tm = _token_tile(n, 512)
    pos = past_len + jnp.arange(t, dtype=I32)
    tables = tuple(jnp.tile(a, (tm // t, 1)) for a in _rope_tables(pos))
    q, k, v, u = _in_proj(x2d, tables, 1, w['norm_mix'], w['w_in'][0], w['w_in'][1], w['b_in'], tm)
    pad = lambda a: jnp.pad(a.reshape(b, t, KV_WIDTH), ((0, 0), (0, WINDOW - t), (0, 0)))
    attn_n, k_win, v_win = _attention(w['sinks'], q.reshape(b, t, ATTN_WIDTH),
                                      k_buf.reshape(b, WINDOW, KV_WIDTH), pad(k),
                                      v_buf.reshape(b, WINDOW, KV_WIDTH), pad(v), w['norm_attn_out'], True)
    ns = min(b, S5_DECODE_SEQS)
    nbk = b // ns
    u_tm = u.reshape(nbk, ns, t, SSM_WIDTH).transpose(0, 2, 1, 3).reshape(nbk, t * ns, SSM_WIDTH)
    ssm_tm, h_re, h_im = _s5(u_tm, h0_re.reshape(nbk, ns, SSM_LANES), h0_im.reshape(nbk, ns, SSM_LANES),
                             w['s5'], t, ns)
    ssm_n = ssm_tm.reshape(nbk, t, ns, SSM_WIDTH).transpose(0, 2, 1, 3).reshape(n, SSM_WIDTH)
    x2 = _mix_and_ffn(x2d, attn_n.reshape(n, ATTN_WIDTH), ssm_n, w, None)
    win = lambda a: a.reshape(b, WINDOW, N_KV_HEADS, HEAD_DIM)
    st = lambda h: h.reshape(b, N_SSM_GROUPS, SSM_STATE)
    return x2.reshape(b, t, D_MODEL), win(k_win), win(v_win), st(h_re), st(h_im)


PAST_LEN = 16384
S5_DECODE_SEQS = 64
DECODE_SEQS_PER_STEP = 8


def kernel(x_prompt, x_sample, cache_k, cache_v, state_ssm_re, state_ssm_im, norm_mix, w_in, b_in, attn_sinks, ssm_lam_re, ssm_lam_im, ssm_log_step, ssm_b_re, ssm_b_im, ssm_c_re, ssm_c_im, ssm_d, ssm_w_glu, ssm_b_glu, norm_attn_out, norm_ssm_out, w_out, norm_ffn, peer_w_query, peer_sub_keys, peer_u, peer_v, norm_final):
    depth = norm_mix.shape[0]
    xp, xs = x_prompt, x_sample
    outs = [[] for _ in range(8)]
    for l in range(depth):
        w = _layer_weights(*[a[l] for a in (norm_mix, w_in, b_in, attn_sinks, ssm_lam_re, ssm_lam_im,
                                            ssm_log_step, ssm_b_re, ssm_b_im, ssm_c_re, ssm_c_im, ssm_d,
                                            ssm_w_glu, ssm_b_glu, norm_attn_out, norm_ssm_out, w_out,
                                            norm_ffn, peer_w_query, peer_sub_keys, peer_u, peer_v)])
        xs, k2, v2, r2, i2 = _sample_layer(xs, cache_k[l], cache_v[l], state_ssm_re[l], state_ssm_im[l],
                                           w, PAST_LEN)
        xp, k1, v1, r1, i1 = _prompt_layer(xp, w)
        for lst, a in zip(outs, (k1, v1, r1, i1, k2, v2, r2, i2)):
            lst.append(a)
    g = norm_final.reshape(1, D_MODEL)
    yp = _final_norm(xp.reshape(-1, D_MODEL), g, 512).reshape(xp.shape)
    ys = _final_norm(xs.reshape(-1, D_MODEL), g, 512).reshape(xs.shape)
    return (yp, ys) + tuple(jnp.stack(o) for o in outs)
```

```python
import functools
import math

import jax
import jax.numpy as jnp
from jax import lax
from jax.experimental import pallas as pl
from jax.experimental.pallas import tpu as pltpu

F32 = jnp.float32
BF16 = jnp.bfloat16
I32 = jnp.int32

D_MODEL = 1024
N_HEADS = 8
N_KV_HEADS = 2
HEAD_DIM = 64
ATTN_WIDTH = N_HEADS * HEAD_DIM
KV_WIDTH = N_KV_HEADS * HEAD_DIM
WINDOW = 128
ROT_DIM = HEAD_DIM // 4
ROPE_THETA = 500000.0
NEG_INF = -1e30
SSM_WIDTH = D_MODEL - ATTN_WIDTH
SSM_GROUP = 16
N_SSM_GROUPS = SSM_WIDTH // SSM_GROUP
SSM_STATE = 64
SSM_LANES = N_SSM_GROUPS * SSM_STATE
IN_WIDTH = ATTN_WIDTH + 2 * KV_WIDTH + SSM_WIDTH
PEER_HEADS = 8
N_KEYS = 128
PEER_TOPK = 16
PEER_QDIM = 256
PEER_HALF = PEER_QDIM // 2
N_PAIRS = PEER_HEADS * PEER_TOPK
RMS_EPS = 1e-5

LANES = 128
SUBLANES = 8
VMEM_LIMIT = 56 * 1024 * 1024

HI_MASK = -65536

S5_CHUNK_GROUPS = LANES // SSM_GROUP
S5_CHUNKS = N_SSM_GROUPS // S5_CHUNK_GROUPS
S5_CHUNK_LANES = S5_CHUNK_GROUPS * SSM_STATE


def _cparams(sem):
    return pltpu.CompilerParams(dimension_semantics=sem, vmem_limit_bytes=VMEM_LIMIT)


def _split(w):
    hi = w.astype(BF16)
    lo = (w - hi.astype(F32)).astype(BF16)
    return hi, lo


def _dot1(a, b):
    return jnp.dot(a.astype(BF16), b, preferred_element_type=F32)


def _dot3(a, b_hi, b_lo):
    a_hi, a_lo = _split(a)
    d = functools.partial(jnp.dot, preferred_element_type=F32)
    return d(a_hi, b_hi) + (d(a_lo, b_hi) + d(a_hi, b_lo))


def _dot3_nt(a_hi, a_lo, b):
    b_hi, b_lo = _split(b)
    d = functools.partial(lax.dot_general, dimension_numbers=(((1,), (1,)), ((), ())),
                          preferred_element_type=F32)
    return d(a_hi, b_hi) + (d(a_lo, b_hi) + d(a_hi, b_lo))


def _gelu(x):
    return 0.5 * x * (1.0 + lax.erf(x * (2.0 ** -0.5)))


def _rms(x, g):
    return x * lax.rsqrt(jnp.mean(x * x, axis=-1, keepdims=True) + RMS_EPS) * g


def _inproj_kernel(x_ref, g_ref, whi_ref, wlo_ref, b_ref, c_ref, sa_ref, sb_ref,
                   q_ref, k_ref, v_ref, u_ref):
    xn = _rms(x_ref[...], g_ref[...])
    proj = _dot3(xn, whi_ref[...], wlo_ref[...]) + b_ref[...]
    c = c_ref[...]
    sa = sa_ref[...]
    sb = sb_ref[...]

    def rope(t):
        return t * c + pltpu.roll(t, LANES - ROT_DIM // 2, 1) * sa + pltpu.roll(t, ROT_DIM // 2, 1) * sb

    for j in range(ATTN_WIDTH // LANES):
        q_ref[:, j * LANES:(j + 1) * LANES] = rope(proj[:, j * LANES:(j + 1) * LANES])
    k_ref[...] = rope(proj[:, ATTN_WIDTH:ATTN_WIDTH + KV_WIDTH])
    v_ref[...] = proj[:, ATTN_WIDTH + KV_WIDTH:ATTN_WIDTH + 2 * KV_WIDTH]
    u_ref[...] = proj[:, ATTN_WIDTH + 2 * KV_WIDTH:]


def _rope_tables(pos):
    half = ROT_DIM // 2
    inv_freq = ROPE_THETA ** (-jnp.arange(half, dtype=F32) * 2.0 / ROT_DIM)
    ang = pos.astype(F32)[:, None] * inv_freq[None, :]
    cos = jnp.cos(ang)
    sin = jnp.sin(ang)
    t = pos.shape[0]
    one = jnp.ones((t, HEAD_DIM - ROT_DIM), F32)
    zero = jnp.zeros((t, HEAD_DIM - ROT_DIM), F32)
    zh = jnp.zeros((t, half), F32)
    c = jnp.concatenate([cos, cos, one], axis=1)
    sa = jnp.concatenate([-sin, zh, zero], axis=1)
    sb = jnp.concatenate([zh, sin, zero], axis=1)
    return tuple(jnp.tile(a, (1, LANES // HEAD_DIM)) for a in (c, sa, sb))


def _in_proj(x2d, pos_tables, n_pos_tiles, g, w_hi, w_lo, b, tm):
    n = x2d.shape[0]
    c, sa, sb = pos_tables
    row = lambda i: (i, 0)
    fixed = lambda i: (0, 0)
    pos_map = lambda i: (i % n_pos_tiles, 0)
    return pl.pallas_call(
        _inproj_kernel,
        grid=(n // tm,),
        in_specs=[pl.BlockSpec((tm, D_MODEL), row), pl.BlockSpec((1, D_MODEL), fixed),
                  pl.BlockSpec((D_MODEL, IN_WIDTH), fixed), pl.BlockSpec((D_MODEL, IN_WIDTH), fixed),
                  pl.BlockSpec((1, IN_WIDTH), fixed),
                  pl.BlockSpec((tm, LANES), pos_map), pl.BlockSpec((tm, LANES), pos_map),
                  pl.BlockSpec((tm, LANES), pos_map)],
        out_specs=[pl.BlockSpec((tm, ATTN_WIDTH), row), pl.BlockSpec((tm, KV_WIDTH), row),
                   pl.BlockSpec((tm, KV_WIDTH), row), pl.BlockSpec((tm, SSM_WIDTH), row)],
        out_shape=[jax.ShapeDtypeStruct((n, ATTN_WIDTH), F32), jax.ShapeDtypeStruct((n, KV_WIDTH), F32),
                   jax.ShapeDtypeStruct((n, KV_WIDTH), F32), jax.ShapeDtypeStruct((n, SSM_WIDTH), F32)],
        compiler_params=_cparams(("arbitrary",)),
        name="in_proj",
    )(x2d, g, w_hi, w_lo, b, c, sa, sb)


def _attn_kernel(sink_ref, q_ref, kp_ref, kc_ref, vp_ref, vc_ref, g_ref, o_ref, *win_refs, tq, decode, nseq):
    for b in range(nseq):
        _attn_one(sink_ref, q_ref.at[b], kp_ref.at[b], kc_ref.at[b], vp_ref.at[b], vc_ref.at[b], g_ref,
                  o_ref.at[b], *[w.at[b] for w in win_refs], tq=tq, decode=decode)


def _attn_one(sink_ref, q_ref, kp_ref, kc_ref, vp_ref, vc_ref, g_ref, o_ref, *win_refs, tq, decode):
    q = q_ref[...] * (HEAD_DIM ** -0.5)
    kp = kp_ref[...]
    kc = kc_ref[...]
    vp = vp_ref[...]
    vc = vc_ref[...]
    half = LANES // 2
    lane = lax.broadcasted_iota(I32, (1, LANES), 1)
    lo = lane < half
    qi = lax.broadcasted_iota(I32, (tq, WINDOW), 0)
    kj = lax.broadcasted_iota(I32, (tq, WINDOW), 1)
    first_off = 0 if decode else jnp.where(pl.program_id(1) > 0, 0, WINDOW)
    m_prev = kj > qi + first_off
    m_cur = kj <= qi

    def variants(t):
        r = pltpu.roll(t, half, 1)
        return [[t, r], [r, t]]

    kpv = [[a.astype(BF16) for a in row] for row in variants(kp)]
    kcv = [[a.astype(BF16) for a in row] for row in variants(kc)]
    lane_sel = [lo, jnp.logical_not(lo)]
    vpv = [[jnp.where(lane_sel[s], a, 0.0).astype(BF16) for s, a in enumerate(row)] for row in variants(vp)]
    vcv = [[jnp.where(lane_sel[s], a, 0.0).astype(BF16) for s, a in enumerate(row)] for row in variants(vc)]
    nt = functools.partial(lax.dot_general, dimension_numbers=(((1,), (1,)), ((), ())),
                           preferred_element_type=F32)
    outs = []
    for j in range(ATTN_WIDTH // LANES):
        g = (2 * j) // (N_HEADS // N_KV_HEADS)
        qt = q[:, j * LANES:(j + 1) * LANES]
        acc = jnp.zeros((tq, LANES), F32)
        for s in range(2):
            h = 2 * j + s
            qm = jnp.where(lane_sel[s], qt, 0.0).astype(BF16)
            sp = jnp.where(m_prev, nt(qm, kpv[g][s]), NEG_INF)
            sc = jnp.where(m_cur, nt(qm, kcv[g][s]), NEG_INF)
            sink = sink_ref[h]
            m = jnp.maximum(jnp.maximum(jnp.max(sp, axis=-1, keepdims=True),
                                        jnp.max(sc, axis=-1, keepdims=True)), sink)
            pp = jnp.exp(sp - m)
            pc = jnp.exp(sc - m)
            den = (jnp.sum(pp, axis=-1, keepdims=True) + jnp.sum(pc, axis=-1, keepdims=True)
                   + jnp.exp(sink - m))
            o = (jnp.dot(pp.astype(BF16), vpv[g][s], preferred_element_type=F32)
                 + jnp.dot(pc.astype(BF16), vcv[g][s], preferred_element_type=F32))
            acc = acc + o / den
        outs.append(acc)
    attn = jnp.concatenate(outs, axis=1)
    o_ref[...] = _rms(attn, g_ref[...])
    if decode:
        kw_ref, vw_ref = win_refs
        kw_ref[:WINDOW - tq] = kp[tq:]
        kw_ref[WINDOW - tq:] = kc[:tq]
        vw_ref[:WINDOW - tq] = vp[tq:]
        vw_ref[WINDOW - tq:] = vc[:tq]


def _attention(sinks, q3, k_prev, k_cur, v_prev, v_cur, g, decode):
    b, t, _ = q3.shape
    tq = t if decode else WINDOW
    nb = t // tq
    nseq = math.gcd(b, DECODE_SEQS_PER_STEP) if decode else 1
    cur = lambda i, n: (i, n, 0)
    prev = (lambda i, n: (i, 0, 0)) if decode else (lambda i, n: (i, jnp.maximum(n - 1, 0), 0))
    kvb = (nseq, WINDOW, KV_WIDTH)
    out_specs = [pl.BlockSpec((nseq, tq, ATTN_WIDTH), cur)]
    out_shape = [jax.ShapeDtypeStruct((b, t, ATTN_WIDTH), F32)]
    if decode:
        out_specs += [pl.BlockSpec(kvb, cur), pl.BlockSpec(kvb, cur)]
        out_shape += [jax.ShapeDtypeStruct((b, WINDOW, KV_WIDTH), F32)] * 2
    return pl.pallas_call(
        functools.partial(_attn_kernel, tq=tq, decode=decode, nseq=nseq),
        grid=(b // nseq, nb),
        in_specs=[pl.BlockSpec(memory_space=pltpu.SMEM),
                  pl.BlockSpec((nseq, tq, ATTN_WIDTH), cur),
                  pl.BlockSpec(kvb, prev), pl.BlockSpec(kvb, cur),
                  pl.BlockSpec(kvb, prev), pl.BlockSpec(kvb, cur),
                  pl.BlockSpec((1, ATTN_WIDTH), lambda i, n: (0, 0))],
        out_specs=out_specs,
        out_shape=out_shape,
        compiler_params=_cparams(("arbitrary", "arbitrary")),
        name="attn_decode" if decode else "attn_prompt",
    )(sinks, q3, k_prev, k_cur, v_prev, v_cur, g)


def _s5_param_kernel(lr_ref, li_ref, ls_ref, abr_ref, abi_ref, zr_ref, zi_ref):
    lr = lr_ref[...]
    li = li_ref[...]
    dt = jnp.exp(ls_ref[...])
    mag = jnp.exp(lr * dt)
    ab_re = mag * jnp.cos(li * dt)
    ab_im = mag * jnp.sin(li * dt)
    den = lr * lr + li * li
    abr_ref[...] = ab_re
    abi_ref[...] = ab_im
    zr_ref[...] = ((ab_re - 1.0) * lr + ab_im * li) / den
    zi_ref[...] = (ab_im * lr - (ab_re - 1.0) * li) / den


def _s5_params(lam_re, lam_im, log_step):
    ls = jnp.broadcast_to(log_step[:, None], lam_re.shape)
    outs = pl.pallas_call(
        _s5_param_kernel,
        out_shape=[jax.ShapeDtypeStruct(lam_re.shape, F32)] * 4,
        name="s5_params",
    )(lam_re, lam_im, ls)
    return [o.reshape(1, SSM_LANES) for o in outs]


def _s5_kernel(u_ref, h0r_ref, h0i_ref, brh_ref, brl_ref, bih_ref, bil_ref, zr_ref, zi_ref,
               ar_ref, ai_ref, cr_ref, ci_ref, d_ref, wg_ref, bg_ref, g_ref,
               y_ref, hr_ref, hi_ref, sr_ref, si_ref, *, tt, ns):
    @pl.when(pl.program_id(1) == 0)
    def _():
        hr_ref[0] = h0r_ref[0]
        hi_ref[0] = h0i_ref[0]

    u = u_ref[0]
    for c in range(S5_CHUNKS):
        uc = u[:, c * LANES:(c + 1) * LANES]
        lanes = slice(c * S5_CHUNK_LANES, (c + 1) * S5_CHUNK_LANES)
        pr = _dot3(uc, brh_ref[c], brl_ref[c])
        pi = _dot3(uc, bih_ref[c], bil_ref[c])
        zr = zr_ref[:, lanes]
        zi = zi_ref[:, lanes]
        sr_ref[:, lanes] = zr * pr - zi * pi
        si_ref[:, lanes] = zr * pi + zi * pr
    ar = jnp.broadcast_to(ar_ref[...], (ns, SSM_LANES))
    ai = jnp.broadcast_to(ai_ref[...], (ns, SSM_LANES))

    if ns <= SUBLANES:
        def body(t, carry):
            hr, hi = carry
            rows = pl.ds(t * ns, ns)
            nr = ar * hr - ai * hi + sr_ref[rows, :]
            ni = ar * hi + ai * hr + si_ref[rows, :]
            sr_ref[rows, :] = nr
            si_ref[rows, :] = ni
            return nr, ni

        hr, hi = lax.fori_loop(0, tt, body, (hr_ref[0], hi_ref[0]), unroll=8)
        hr_ref[0] = hr
        hi_ref[0] = hi
    else:
        def body(t, carry):
            rows = pl.ds(pl.multiple_of(t * ns, ns), ns)
            hr = hr_ref[0]
            hi = hi_ref[0]
            nr = ar * hr - ai * hi + sr_ref[rows, :]
            ni = ar * hi + ai * hr + si_ref[rows, :]
            sr_ref[rows, :] = nr
            si_ref[rows, :] = ni
            hr_ref[0] = nr
            hi_ref[0] = ni
            return carry

        lax.fori_loop(0, tt, body, 0)

    ch = []
    for c in range(S5_CHUNKS):
        lanes = slice(c * S5_CHUNK_LANES, (c + 1) * S5_CHUNK_LANES)
        ch.append(_dot1(sr_ref[:, lanes], cr_ref[c]) - _dot1(si_ref[:, lanes], ci_ref[c]))
    y = jnp.concatenate(ch, axis=1) + d_ref[...] * u
    y = _gelu(y)
    gate = _dot1(y, wg_ref[...]) + bg_ref[...]
    y = y * (1.0 / (1.0 + jnp.exp(-gate)))
    y_ref[0] = _rms(y, g_ref[...])


def _s5(u3, h0r, h0i, consts, tt, ns):
    nb, rows, _ = u3.shape
    r = tt * ns
    fixed = lambda b, t: (0, 0)
    tile = lambda b, t: (b, t, 0)
    seq = lambda b, t: (b, 0, 0)
    const_specs = [pl.BlockSpec(c.shape, lambda b, t, nd=c.ndim: (0,) * nd) for c in consts]
    return pl.pallas_call(
        functools.partial(_s5_kernel, tt=tt, ns=ns),
        grid=(nb, rows // r),
        in_specs=[pl.BlockSpec((1, r, SSM_WIDTH), tile),
                  pl.BlockSpec((1, ns, SSM_LANES), seq), pl.BlockSpec((1, ns, SSM_LANES), seq)] + const_specs,
        out_specs=[pl.BlockSpec((1, r, SSM_WIDTH), tile),
                   pl.BlockSpec((1, ns, SSM_LANES), seq), pl.BlockSpec((1, ns, SSM_LANES), seq)],
        out_shape=[jax.ShapeDtypeStruct((nb, rows, SSM_WIDTH), F32),
                   jax.ShapeDtypeStruct((nb, ns, SSM_LANES), F32),
                   jax.ShapeDtypeStruct((nb, ns, SSM_LANES), F32)],
        scratch_shapes=[pltpu.VMEM((r, SSM_LANES), F32), pltpu.VMEM((r, SSM_LANES), F32)],
        compiler_params=_cparams(("arbitrary", "arbitrary")),
        name="s5_ns%d" % ns,
    )(u3, h0r, h0i, *consts)


def _outproj_kernel(a_ref, s_ref, x_ref, wa_ref, ws_ref, g_ref, x1_ref, xn_ref):
    x1 = x_ref[...] + _dot1(a_ref[...], wa_ref[...]) + _dot1(s_ref[...], ws_ref[...])
    x1_ref[...] = x1
    xn_ref[...] = _rms(x1, g_ref[...])


def _out_proj(attn_n, ssm_n, x2d, w_attn, w_ssm, g, tm):
    n = x2d.shape[0]
    row = lambda i: (i, 0)
    fixed = lambda i: (0, 0)
    return pl.pallas_call(
        _outproj_kernel,
        grid=(n // tm,),
        in_specs=[pl.BlockSpec((tm, ATTN_WIDTH), row), pl.BlockSpec((tm, SSM_WIDTH), row),
                  pl.BlockSpec((tm, D_MODEL), row),
                  pl.BlockSpec((ATTN_WIDTH, D_MODEL), fixed), pl.BlockSpec((SSM_WIDTH, D_MODEL), fixed),
                  pl.BlockSpec((1, D_MODEL), fixed)],
        out_specs=[pl.BlockSpec((tm, D_MODEL), row), pl.BlockSpec((tm, D_MODEL), row)],
        out_shape=[jax.ShapeDtypeStruct((n, D_MODEL), F32)] * 2,
        compiler_params=_cparams(("arbitrary",)),
        name="out_proj",
    )(attn_n, ssm_n, x2d, w_attn, w_ssm, g)


def _topk_rows(s, ids, k):
    vals, picks = [], []
    sentinel = jnp.iinfo(jnp.int32).max
    for _ in range(k):
        m = jnp.max(s, axis=0, keepdims=True)
        pick = jnp.min(jnp.where(s == m, ids, sentinel), axis=0, keepdims=True)
        vals.append(m)
        picks.append(pick)
        s = jnp.where(ids == pick, -jnp.inf, s)
    return vals, picks


def _cand_layout():
    pieces = [(0, 1, 0, PEER_TOPK)]
    pieces += [(a, a + 1, 0, SUBLANES) for a in range(1, SUBLANES)]
    pieces += [(SUBLANES, PEER_TOPK, 0, 1)]
    return pieces


def _topk_kernel(xn_ref, wqh_ref, wql_ref, keys_ref, row_ref, shift_ref, gate_ref, *, tm):
    h = pl.program_id(1)
    q = _dot3(xn_ref[...], wqh_ref[h], wql_ref[h])
    key_id = lax.broadcasted_iota(I32, (N_KEYS, tm), 0)
    sub_v, sub_i = [], []
    for c in range(2):
        q_hi, q_lo = _split(q[:, c * PEER_HALF:(c + 1) * PEER_HALF])
        keys = keys_ref[2 * h + c]
        k_hi, k_lo = _split(keys)
        d = functools.partial(lax.dot_general, dimension_numbers=(((1,), (1,)), ((), ())),
                              preferred_element_type=F32)
        s = d(k_hi, q_hi) + (d(k_lo, q_hi) + d(k_hi, q_lo))
        vals, picks = _topk_rows(s, key_id, PEER_TOPK)
        sub_v.append(vals)
        sub_i.append(picks)
    cs, ce, cf = [], [], []

    def rows_of(lst, lo, hi):
        return lst[lo] if hi - lo == 1 else jnp.concatenate(lst[lo:hi], axis=0)

    for a_lo, a_hi, b_lo, b_hi in _cand_layout():
        na, nbb = a_hi - a_lo, b_hi - b_lo
        rows = max(na, nbb)
        cs.append(rows_of(sub_v[0], a_lo, a_hi) + rows_of(sub_v[1], b_lo, b_hi))
        ce.append(rows_of(sub_i[0], a_lo, a_hi) * N_KEYS + rows_of(sub_i[1], b_lo, b_hi))
        r = lax.broadcasted_iota(I32, (rows, tm), 0)
        cf.append((a_lo + r) * PEER_TOPK + b_lo if na > 1 else a_lo * PEER_TOPK + b_lo + r)
    cand_s = jnp.concatenate(cs, axis=0)
    cand_e = jnp.concatenate(ce, axis=0)
    cand_f = jnp.concatenate(cf, axis=0)
    best_v, best_e = [], []
    sentinel = jnp.iinfo(jnp.int32).max
    for _ in range(PEER_TOPK):
        m = jnp.max(cand_s, axis=0, keepdims=True)
        f = jnp.min(jnp.where(cand_s == m, cand_f, sentinel), axis=0, keepdims=True)
        hit = cand_f == f
        best_v.append(m)
        best_e.append(jnp.max(jnp.where(hit, cand_e, -1), axis=0, keepdims=True))
        cand_s = jnp.where(hit, -jnp.inf, cand_s)
    bv = jnp.concatenate(best_v, axis=0)
    be = jnp.concatenate(best_e, axis=0)
    ex = jnp.exp(bv - bv[0:1])
    gate_ref[...] = ex / jnp.sum(ex, axis=0, keepdims=True)
    row_ref[...] = lax.shift_right_logical(be, 1) * SUBLANES
    shift_ref[...] = ((1 - (be & 1)) * 16).astype(F32)


def _peer_topk(xn2d, wq_hi, wq_lo, keys, tm):
    n = xn2d.shape[0]
    out = lambda i, h: (h, i)
    return pl.pallas_call(
        functools.partial(_topk_kernel, tm=tm),
        grid=(n // tm, PEER_HEADS),
        in_specs=[pl.BlockSpec((tm, D_MODEL), lambda i, h: (i, 0)),
                  pl.BlockSpec(wq_hi.shape, lambda i, h: (0, 0, 0)),
                  pl.BlockSpec(wq_lo.shape, lambda i, h: (0, 0, 0)),
                  pl.BlockSpec(keys.shape, lambda i, h: (0, 0, 0))],
        out_specs=[pl.BlockSpec((PEER_TOPK, tm), out)] * 3,
        out_shape=[jax.ShapeDtypeStruct((N_PAIRS, n), I32), jax.ShapeDtypeStruct((N_PAIRS, n), F32),
                   jax.ShapeDtypeStruct((N_PAIRS, n), F32)],
        compiler_params=_cparams(("arbitrary", "arbitrary")),
        name="peer_topk",
    )(xn2d, wq_hi, wq_lo, keys)


def _pack_kernel(t_ref, o_ref):
    even = t_ref[:, 0].astype(BF16).astype(F32)
    odd = t_ref[:, 1].astype(BF16).astype(F32)
    hi = pltpu.bitcast(odd, I32) & HI_MASK
    lo = lax.shift_right_logical(pltpu.bitcast(even, I32), 16)
    o_ref[...] = hi | lo


def _pack_table(table, tb=256):
    e = table.shape[0]
    t4 = table.reshape(e // 2, 2, SUBLANES, LANES)
    packed = pl.pallas_call(
        _pack_kernel,
        grid=(e // 2 // tb,),
        in_specs=[pl.BlockSpec((tb, 2, SUBLANES, LANES), lambda i: (i, 0, 0, 0))],
        out_specs=pl.BlockSpec((tb, SUBLANES, LANES), lambda i: (i, 0, 0)),
        out_shape=jax.ShapeDtypeStruct((e // 2, SUBLANES, LANES), I32),
        compiler_params=_cparams(("arbitrary",)),
        name="pack_table",
    )(t4)
    return packed.reshape(e // 2 * SUBLANES, LANES)


def _tile(tbl, offset):
    return tbl[pl.ds(pl.multiple_of(offset, SUBLANES), SUBLANES), :]


def _expert_row(word, shift):
    return pltpu.bitcast((word << shift) & HI_MASK, F32)


def _row8(ref, k):
    return jnp.broadcast_to(ref[k:k + 1, :], (SUBLANES, LANES))


def _fold(p, steps):
    sub = lax.broadcasted_iota(I32, p[0].shape, 0)
    for step in steps:
        first = (sub % (2 * step)) < step
        n = len(p) // 2
        p = [jnp.where(first, p[j] + pltpu.roll(p[j], SUBLANES - step, 0),
                       p[j + n] + pltpu.roll(p[j + n], step, 0)) for j in range(n)]
    return p


def _column(block_ref, tile, tm, t):
    blk = block_ref[:, tile * tm:(tile + 1) * tm]
    lane = lax.broadcasted_iota(I32, blk.shape, 1)
    col = jnp.sum(jnp.where(lane == t, blk, 0.0), axis=1, keepdims=True)
    return jnp.broadcast_to(col, (N_PAIRS, LANES))


def _rows_copy(row_hbm, buf, sem, tile, tm):
    return pltpu.make_async_copy(row_hbm.at[:, pl.ds(tile * tm, tm)], buf, sem)


def _for_each_tile(row_hbm, bufs, sems, tm, body):
    i = pl.program_id(0)
    tps = len(bufs)
    first = i * tps
    total = pl.num_programs(0) * tps

    def copy(j, tile):
        return _rows_copy(row_hbm, bufs[j], sems.at[j], tile, tm)

    @pl.when(i == 0)
    def _():
        copy(0, 0).start()

    for j in range(tps):
        nxt = (j + 1) % tps
        if tps == 1:
            copy(0, first).wait()
            body(0, bufs[0])

            @pl.when(first + 1 < total)
            def _():
                copy(0, first + 1).start()
        else:
            @pl.when(first + j + 1 < total)
            def _():
                copy(nxt, first + j + 1).start()

            copy(j, first + j).wait()
            body(j, bufs[j])


def _load_table(tbl_hbm, tbl_vmem, sem):
    @pl.when(pl.program_id(0) == 0)
    def _():
        c = pltpu.make_async_copy(tbl_hbm, tbl_vmem, sem)
        c.start()
        c.wait()


def _tag_parity(w, is_odd):
    bits = (pltpu.bitcast(w, I32) & -2) | jnp.where(is_odd, 1, 0)
    return pltpu.bitcast(bits, F32)


def _tiles_per_step(n, tm):
    return 2 if (n // tm) % 2 == 0 else 1


def _peer_act_kernel(row_hbm, x_ref, shift_ref, gate_ref, tbl_hbm, w_ref, tbl, *scratch, tm, tps):
    bufs = scratch[:tps]
    sh_a, sh_b, acc, sems = scratch[tps:]
    _load_table(tbl_hbm, tbl, sems.at[tps])
    lane = lax.broadcasted_iota(I32, (SUBLANES, tm), 1)
    n_groups = N_PAIRS // SUBLANES

    def tile_body(tile, rows):
        def expand(t, sh):
            sh[...] = _column(shift_ref, tile, tm, t).astype(I32)

        def gather(t, sh):
            xt = x_ref[tile * tm + t]
            here = lane == t
            for g in range(n_groups):
                prods = []
                for j in range(SUBLANES):
                    k = g * SUBLANES + j
                    prods.append(_expert_row(_tile(tbl, rows.at[k][t]), _row8(sh, k)) * xt)
                folded = _fold(prods, (4, 2, 1))[0]
                acc[g] = jnp.where(here, jnp.sum(folded, axis=1, keepdims=True), acc[g])

        acc[...] = jnp.zeros(acc.shape, F32)
        expand(0, sh_a)

        def two_tokens(j, carry):
            t = 2 * j
            gather(t, sh_a)
            expand(t + 1, sh_b)
            gather(t + 1, sh_b)
            expand(jnp.minimum(t + 2, tm - 1), sh_a)
            return carry

        lax.fori_loop(0, tm // 2, two_tokens, 0)
        act = jnp.concatenate([acc[g] for g in range(n_groups)], axis=0)
        cols = slice(tile * tm, (tile + 1) * tm)
        w_ref[:, cols] = _tag_parity(gate_ref[:, cols] * _gelu(act), shift_ref[:, cols] == 0.0)

    _for_each_tile(row_hbm, bufs, sems, tm, tile_body)


def _peer_act(rows, xn3, shift, gate, tbl, tm):
    n = xn3.shape[0]
    tps = _tiles_per_step(n, tm)
    step = tps * tm
    col = pl.BlockSpec((N_PAIRS, step), lambda i: (0, i))
    return pl.pallas_call(
        functools.partial(_peer_act_kernel, tm=tm, tps=tps),
        grid=(n // step,),
        in_specs=[pl.BlockSpec(memory_space=pl.ANY),
                  pl.BlockSpec((step, SUBLANES, LANES), lambda i: (i, 0, 0)),
                  col, col, pl.BlockSpec(memory_space=pl.ANY)],
        out_specs=col,
        out_shape=jax.ShapeDtypeStruct((N_PAIRS, n), F32),
        scratch_shapes=([pltpu.VMEM(tbl.shape, I32)] + [pltpu.SMEM((N_PAIRS, tm), I32)] * tps
                        + [pltpu.VMEM((N_PAIRS, LANES), I32), pltpu.VMEM((N_PAIRS, LANES), I32),
                           pltpu.VMEM((N_PAIRS // SUBLANES, SUBLANES, tm), F32),
                           pltpu.SemaphoreType.DMA((tps + 1,))]),
        compiler_params=_cparams(("arbitrary",)),
        name="peer_act",
    )(rows, xn3, shift, gate, tbl)


def _peer_out_kernel(row_hbm, w_ref, x1_ref, tbl_hbm, o_ref, tbl, *scratch, tm, tps):
    bufs = scratch[:tps]
    sh_a, sh_b, wb_a, wb_b, sems = scratch[tps:]
    _load_table(tbl_hbm, tbl, sems.at[tps])
    n_acc = 4

    def tile_body(tile, rows):
        def expand(t, sh, wb):
            col = _column(w_ref, tile, tm, t)
            wb[...] = col
            sh[...] = ((pltpu.bitcast(col, I32) & 1) ^ 1) << 4

        def gather(t, sh, wb):
            accs = [x1_ref[tile * tm + t]] + [jnp.zeros((SUBLANES, LANES), F32)] * (n_acc - 1)
            for k in range(N_PAIRS):
                v_row = _expert_row(_tile(tbl, rows.at[k][t]), _row8(sh, k))
                accs[k % n_acc] = accs[k % n_acc] + _row8(wb, k) * v_row
            o_ref[tile * tm + t] = (accs[0] + accs[1]) + (accs[2] + accs[3])

        expand(0, sh_a, wb_a)

        def two_tokens(j, carry):
            t = 2 * j
            gather(t, sh_a, wb_a)
            expand(t + 1, sh_b, wb_b)
            gather(t + 1, sh_b, wb_b)
            expand(jnp.minimum(t + 2, tm - 1), sh_a, wb_a)
            return carry

        lax.fori_loop(0, tm // 2, two_tokens, 0)

    _for_each_tile(row_hbm, bufs, sems, tm, tile_body)


def _peer_out(rows, w, x1_3, tbl, tm):
    n = x1_3.shape[0]
    tps = _tiles_per_step(n, tm)
    step = tps * tm
    blk = pl.BlockSpec((step, SUBLANES, LANES), lambda i: (i, 0, 0))
    col = pl.BlockSpec((N_PAIRS, step), lambda i: (0, i))
    return pl.pallas_call(
        functools.partial(_peer_out_kernel, tm=tm, tps=tps),
        grid=(n // step,),
        in_specs=[pl.BlockSpec(memory_space=pl.ANY), col, blk, pl.BlockSpec(memory_space=pl.ANY)],
        out_specs=blk,
        out_shape=jax.ShapeDtypeStruct(x1_3.shape, F32),
        scratch_shapes=([pltpu.VMEM(tbl.shape, I32)] + [pltpu.SMEM((N_PAIRS, tm), I32)] * tps
                        + [pltpu.VMEM((N_PAIRS, LANES), I32), pltpu.VMEM((N_PAIRS, LANES), I32),
                           pltpu.VMEM((N_PAIRS, LANES), F32), pltpu.VMEM((N_PAIRS, LANES), F32),
                           pltpu.SemaphoreType.DMA((tps + 1,))]),
        compiler_params=_cparams(("arbitrary",)),
        name="peer_out",
    )(rows, w, x1_3, tbl)


def _final_kernel(x_ref, g_ref, o_ref):
    o_ref[...] = _rms(x_ref[...], g_ref[...])


def _final_norm(x2d, g, tm):
    n = x2d.shape[0]
    return pl.pallas_call(
        _final_kernel,
        grid=(n // tm,),
        in_specs=[pl.BlockSpec((tm, D_MODEL), lambda i: (i, 0)), pl.BlockSpec((1, D_MODEL), lambda i: (0, 0))],
        out_specs=pl.BlockSpec((tm, D_MODEL), lambda i: (i, 0)),
        out_shape=jax.ShapeDtypeStruct(x2d.shape, F32),
        compiler_params=_cparams(("arbitrary",)),
        name="final_norm",
    )(x2d, g)


def _block_diag_in(b):
    eye = jnp.eye(S5_CHUNK_GROUPS, dtype=F32)
    b4 = b.reshape(S5_CHUNKS, S5_CHUNK_GROUPS, SSM_STATE, SSM_GROUP)
    return jnp.einsum('cgnp,gh->cgphn', b4, eye).reshape(S5_CHUNKS, LANES, S5_CHUNK_LANES)


def _block_diag_out(c):
    eye = jnp.eye(S5_CHUNK_GROUPS, dtype=F32)
    c4 = c.reshape(S5_CHUNKS, S5_CHUNK_GROUPS, SSM_GROUP, SSM_STATE)
    return jnp.einsum('cgpn,gh->cgnhp', c4, eye).reshape(S5_CHUNKS, S5_CHUNK_LANES, LANES)


def _layer_weights(norm_mix, w_in, b_in, attn_sinks, lam_re, lam_im, log_step, b_re, b_im, c_re, c_im,
                   d_skip, w_glu, b_glu, norm_attn_out, norm_ssm_out, w_out, norm_ffn, w_query, sub_keys,
                   u_table, v_table):
    w = {}
    w['norm_mix'] = norm_mix.reshape(1, D_MODEL)
    w['w_in'] = _split(w_in)
    w['b_in'] = b_in.reshape(1, IN_WIDTH)
    w['sinks'] = attn_sinks
    ab_re, ab_im, z_re, z_im = _s5_params(lam_re, lam_im, log_step)
    w['s5'] = (list(_split(_block_diag_in(b_re))) + list(_split(_block_diag_in(b_im)))
               + [z_re, z_im, ab_re, ab_im,
                  _block_diag_out(c_re).astype(BF16), _block_diag_out(c_im).astype(BF16),
                  d_skip.reshape(1, SSM_WIDTH), w_glu.astype(BF16), b_glu.reshape(1, SSM_WIDTH),
                  norm_ssm_out.reshape(1, SSM_WIDTH)])
    w['norm_attn_out'] = norm_attn_out.reshape(1, ATTN_WIDTH)
    w['w_out_attn'] = w_out[:ATTN_WIDTH].astype(BF16)
    w['w_out_ssm'] = w_out[ATTN_WIDTH:].astype(BF16)
    w['norm_ffn'] = norm_ffn.reshape(1, D_MODEL)
    wq = w_query.reshape(D_MODEL, PEER_HEADS, PEER_QDIM).transpose(1, 0, 2)
    w['w_query'] = _split(wq)
    w['keys'] = sub_keys.reshape(PEER_HEADS * 2, N_KEYS, PEER_HALF)
    w['u_tbl'] = _pack_table(u_table)
    w['v_tbl'] = _pack_table(v_table)
    return w


def _peer(xn2d, x1_2d, w, tm_topk, tm_gather):
    n = xn2d.shape[0]
    rows, shift, gate = _peer_topk(xn2d, w['w_query'][0], w['w_query'][1], w['keys'], tm_topk)
    wts = _peer_act(rows, xn2d.reshape(n, SUBLANES, LANES), shift, gate, w['u_tbl'], tm_gather)
    out = _peer_out(rows, wts, x1_2d.reshape(n, SUBLANES, LANES), w['v_tbl'], tm_gather)
    return out.reshape(n, D_MODEL)


def _token_tile(n, cap):
    t = cap
    while n % t:
        t //= 2
    return t


def _mix_and_ffn(x2d, attn_n, ssm_n, w, norm_final):
    n = x2d.shape[0]
    x1, xn2 = _out_proj(attn_n, ssm_n, x2d, w['w_out_attn'], w['w_out_ssm'], w['norm_ffn'], _token_tile(n, 512))
    x2 = _peer(xn2, x1, w, _token_tile(n, 256), _token_tile(n, 128))
    return x2


def _prompt_layer(x, w):
    b, t, _ = x.shape
    n = b * t
    x2d = x.reshape(n, D_MODEL)
    tm = _token_tile(t, 512)
    tables = _rope_tables(jnp.arange(t, dtype=I32))
    q, k, v, u = _in_proj(x2d, tables, t // tm, w['norm_mix'], w['w_in'][0], w['w_in'][1], w['b_in'], tm)
    k3 = k.reshape(b, t, KV_WIDTH)
    v3 = v.reshape(b, t, KV_WIDTH)
    (attn_n,) = _attention(w['sinks'], q.reshape(b, t, ATTN_WIDTH), k3, k3, v3, v3, w['norm_attn_out'], False)
    h0 = jnp.zeros((b, 1, SSM_LANES), F32)
    ssm_n, h_re, h_im = _s5(u.reshape(b, t, SSM_WIDTH), h0, h0, w['s5'], _token_tile(t, 256), 1)
    x2 = _mix_and_ffn(x2d, attn_n.reshape(n, ATTN_WIDTH), ssm_n.reshape(n, SSM_WIDTH), w, None)
    k_win = k3[:, -WINDOW:].reshape(b, WINDOW, N_KV_HEADS, HEAD_DIM)
    v_win = v3[:, -WINDOW:].reshape(b, WINDOW, N_KV_HEADS, HEAD_DIM)
    st = lambda h: h.reshape(b, N_SSM_GROUPS, SSM_STATE)
    return x2.reshape(b, t, D_MODEL), k_win, v_win, st(h_re), st(h_im)


def _sample_layer(x, k_buf, v_buf, h0_re, h0_im, w, past_len):
    b, t, _ = x.shape
    n = b * t
    x2d = x.reshape(n, D_MODEL)
    tm = _token_tile(n, 512)
    pos = past_len + jnp.arange(t, dtype=I32)
    tables = tuple(jnp.tile(a, (tm // t, 1)) for a in _rope_tables(pos))
    q, k, v, u = _in_proj(x2d, tables, 1, w['norm_mix'], w['w_in'][0], w['w_in'][1], w['b_in'], tm)
    pad = lambda a: jnp.pad(a.reshape(b, t, KV_WIDTH), ((0, 0), (0, WINDOW - t), (0, 0)))
    attn_n, k_win, v_win = _attention(w['sinks'], q.reshape(b, t, ATTN_WIDTH),
                                      k_buf.reshape(b, WINDOW, KV_WIDTH), pad(k),
                                      v_buf.reshape(b, WINDOW, KV_WIDTH), pad(v), w['norm_attn_out'], True)
    ns = min(b, S5_DECODE_SEQS)
    nbk = b // ns
    u_tm = u.reshape(nbk, ns, t, SSM_WIDTH).transpose(0, 2, 1, 3).reshape(nbk, t * ns, SSM_WIDTH)
    ssm_tm, h_re, h_im = _s5(u_tm, h0_re.reshape(nbk, ns, SSM_LANES), h0_im.reshape(nbk, ns, SSM_LANES),
                             w['s5'], t, ns)
    ssm_n = ssm_tm.reshape(nbk, t, ns, SSM_WIDTH).transpose(0, 2, 1, 3).reshape(n, SSM_WIDTH)
    x2 = _mix_and_ffn(x2d, attn_n.reshape(n, ATTN_WIDTH), ssm_n, w, None)
    win = lambda a: a.reshape(b, WINDOW, N_KV_HEADS, HEAD_DIM)
    st = lambda h: h.reshape(b, N_SSM_GROUPS, SSM_STATE)
    return x2.reshape(b, t, D_MODEL), win(k_win), win(v_win), st(h_re), st(h_im)


PAST_LEN = 16384
S5_DECODE_SEQS = 64
DECODE_SEQS_PER_STEP = 8


def kernel(x_prompt, x_sample, cache_k, cache_v, state_ssm_re, state_ssm_im, norm_mix, w_in, b_in, attn_sinks, ssm_lam_re, ssm_lam_im, ssm_log_step, ssm_b_re, ssm_b_im, ssm_c_re, ssm_c_im, ssm_d, ssm_w_glu, ssm_b_glu, norm_attn_out, norm_ssm_out, w_out, norm_ffn, peer_w_query, peer_sub_keys, peer_u, peer_v, norm_final):
    depth = norm_mix.shape[0]
    xp, xs = x_prompt, x_sample
    outs = [[] for _ in range(8)]
    for l in range(depth):
        w = _layer_weights(*[a[l] for a in (norm_mix, w_in, b_in, attn_sinks, ssm_lam_re, ssm_lam_im,
                                            ssm_log_step, ssm_b_re, ssm_b_im, ssm_c_re, ssm_c_im, ssm_d,
                                            ssm_w_glu, ssm_b_glu, norm_attn_out, norm_ssm_out, w_out,
                                            norm_ffn, peer_w_query, peer_sub_keys, peer_u, peer_v)])
        xs, k2, v2, r2, i2 = _sample_layer(xs, cache_k[l], cache_v[l], state_ssm_re[l], state_ssm_im[l],
                                           w, PAST_LEN)
        xp, k1, v1, r1, i1 = _prompt_layer(xp, w)
        for lst, a in zip(outs, (k1, v1, r1, i1, k2, v2, r2, i2)):
            lst.append(a)
    g = norm_final.reshape(1, D_MODEL)
    yp = _final_norm(xp.reshape(-1, D_MODEL), g, 512).reshape(xp.shape)
    ys = _final_norm(xs.reshape(-1, D_MODEL), g, 512).reshape(xs.shape)
    return (yp, ys) + tuple(jnp.stack(o) for o in outs)
```

```python
import functools
import math

import jax
import jax.numpy as jnp
from jax import lax
from jax.experimental import pallas as pl
from jax.experimental.pallas import tpu as pltpu

F32 = jnp.float32
BF16 = jnp.bfloat16
I32 = jnp.int32

D_MODEL = 1024
N_HEADS = 8
N_KV_HEADS = 2
HEAD_DIM = 64
ATTN_WIDTH = N_HEADS * HEAD_DIM
KV_WIDTH = N_KV_HEADS * HEAD_DIM
WINDOW = 128
ROT_DIM = HEAD_DIM // 4
ROPE_THETA = 500000.0
NEG_INF = -1e30
SSM_WIDTH = D_MODEL - ATTN_WIDTH
SSM_GROUP = 16
N_SSM_GROUPS = SSM_WIDTH // SSM_GROUP
SSM_STATE = 64
SSM_LANES = N_SSM_GROUPS * SSM_STATE
IN_WIDTH = ATTN_WIDTH + 2 * KV_WIDTH + SSM_WIDTH
PEER_HEADS = 8
N_KEYS = 128
PEER_TOPK = 16
PEER_QDIM = 256
PEER_HALF = PEER_QDIM // 2
N_PAIRS = PEER_HEADS * PEER_TOPK
RMS_EPS = 1e-5

LANES = 128
SUBLANES = 8
VMEM_LIMIT = 56 * 1024 * 1024

HI_MASK = -65536

S5_CHUNK_GROUPS = LANES // SSM_GROUP
S5_CHUNKS = N_SSM_GROUPS // S5_CHUNK_GROUPS
S5_CHUNK_LANES = S5_CHUNK_GROUPS * SSM_STATE


def _cparams(sem):
    return pltpu.CompilerParams(dimension_semantics=sem, vmem_limit_bytes=VMEM_LIMIT)


def _split(w):
    hi = w.astype(BF16)
    lo = (w - hi.astype(F32)).astype(BF16)
    return hi, lo


def _dot1(a, b):
    return jnp.dot(a.astype(BF16), b, preferred_element_type=F32)


def _dot3(a, b_hi, b_lo):
    a_hi, a_lo = _split(a)
    d = functools.partial(jnp.dot, preferred_element_type=F32)
    return d(a_hi, b_hi) + (d(a_lo, b_hi) + d(a_hi, b_lo))


def _dot3_nt(a_hi, a_lo, b):
    b_hi, b_lo = _split(b)
    d = functools.partial(lax.dot_general, dimension_numbers=(((1,), (1,)), ((), ())),
                          preferred_element_type=F32)
    return d(a_hi, b_hi) + (d(a_lo, b_hi) + d(a_hi, b_lo))


def _gelu(x):
    return 0.5 * x * (1.0 + lax.erf(x * (2.0 ** -0.5)))


def _rms(x, g):
    return x * lax.rsqrt(jnp.mean(x * x, axis=-1, keepdims=True) + RMS_EPS) * g


def _inproj_kernel(x_ref, g_ref, whi_ref, wlo_ref, b_ref, c_ref, sa_ref, sb_ref,
                   q_ref, k_ref, v_ref, u_ref):
    xn = _rms(x_ref[...], g_ref[...])
    proj = _dot3(xn, whi_ref[...], wlo_ref[...]) + b_ref[...]
    c = c_ref[...]
    sa = sa_ref[...]
    sb = sb_ref[...]

    def rope(t):
        return t * c + pltpu.roll(t, LANES - ROT_DIM // 2, 1) * sa + pltpu.roll(t, ROT_DIM // 2, 1) * sb

    for j in range(ATTN_WIDTH // LANES):
        q_ref[:, j * LANES:(j + 1) * LANES] = rope(proj[:, j * LANES:(j + 1) * LANES])
    k_ref[...] = rope(proj[:, ATTN_WIDTH:ATTN_WIDTH + KV_WIDTH])
    v_ref[...] = proj[:, ATTN_WIDTH + KV_WIDTH:ATTN_WIDTH + 2 * KV_WIDTH]
    u_ref[...] = proj[:, ATTN_WIDTH + 2 * KV_WIDTH:]


def _rope_tables(pos):
    half = ROT_DIM // 2
    inv_freq = ROPE_THETA ** (-jnp.arange(half, dtype=F32) * 2.0 / ROT_DIM)
    ang = pos.astype(F32)[:, None] * inv_freq[None, :]
    cos = jnp.cos(ang)
    sin = jnp.sin(ang)
    t = pos.shape[0]
    one = jnp.ones((t, HEAD_DIM - ROT_DIM), F32)
    zero = jnp.zeros((t, HEAD_DIM - ROT_DIM), F32)
    zh = jnp.zeros((t, half), F32)
    c = jnp.concatenate([cos, cos, one], axis=1)
    sa = jnp.concatenate([-sin, zh, zero], axis=1)
    sb = jnp.concatenate([zh, sin, zero], axis=1)
    return tuple(jnp.tile(a, (1, LANES // HEAD_DIM)) for a in (c, sa, sb))


def _in_proj(x2d, pos_tables, n_pos_tiles, g, w_hi, w_lo, b, tm):
    n = x2d.shape[0]
    c, sa, sb = pos_tables
    row = lambda i: (i, 0)
    fixed = lambda i: (0, 0)
    pos_map = lambda i: (i % n_pos_tiles, 0)
    return pl.pallas_call(
        _inproj_kernel,
        grid=(n // tm,),
        in_specs=[pl.BlockSpec((tm, D_MODEL), row), pl.BlockSpec((1, D_MODEL), fixed),
                  pl.BlockSpec((D_MODEL, IN_WIDTH), fixed), pl.BlockSpec((D_MODEL, IN_WIDTH), fixed),
                  pl.BlockSpec((1, IN_WIDTH), fixed),
                  pl.BlockSpec((tm, LANES), pos_map), pl.BlockSpec((tm, LANES), pos_map),
                  pl.BlockSpec((tm, LANES), pos_map)],
        out_specs=[pl.BlockSpec((tm, ATTN_WIDTH), row), pl.BlockSpec((tm, KV_WIDTH), row),
                   pl.BlockSpec((tm, KV_WIDTH), row), pl.BlockSpec((tm, SSM_WIDTH), row)],
        out_shape=[jax.ShapeDtypeStruct((n, ATTN_WIDTH), F32), jax.ShapeDtypeStruct((n, KV_WIDTH), F32),
                   jax.ShapeDtypeStruct((n, KV_WIDTH), F32), jax.ShapeDtypeStruct((n, SSM_WIDTH), F32)],
        compiler_params=_cparams(("arbitrary",)),
        name="in_proj",
    )(x2d, g, w_hi, w_lo, b, c, sa, sb)


def _attn_kernel(sink_ref, q_ref, kp_ref, kc_ref, vp_ref, vc_ref, g_ref, o_ref, *win_refs, tq, decode, nseq):
    for b in range(nseq):
        _attn_one(sink_ref, q_ref.at[b], kp_ref.at[b], kc_ref.at[b], vp_ref.at[b], vc_ref.at[b], g_ref,
                  o_ref.at[b], *[w.at[b] for w in win_refs], tq=tq, decode=decode)


def _attn_one(sink_ref, q_ref, kp_ref, kc_ref, vp_ref, vc_ref, g_ref, o_ref, *win_refs, tq, decode):
    q = q_ref[...] * (HEAD_DIM ** -0.5)
    kp = kp_ref[...]
    kc = kc_ref[...]
    vp = vp_ref[...]
    vc = vc_ref[...]
    half = LANES // 2
    lane = lax.broadcasted_iota(I32, (1, LANES), 1)
    lo = lane < half
    qi = lax.broadcasted_iota(I32, (tq, WINDOW), 0)
    kj = lax.broadcasted_iota(I32, (tq, WINDOW), 1)
    first_off = 0 if decode else jnp.where(pl.program_id(1) > 0, 0, WINDOW)
    m_prev = kj > qi + first_off
    m_cur = kj <= qi

    def variants(t):
        r = pltpu.roll(t, half, 1)
        return [[t, r], [r, t]]

    kpv = [[a.astype(BF16) for a in row] for row in variants(kp)]
    kcv = [[a.astype(BF16) for a in row] for row in variants(kc)]
    lane_sel = [lo, jnp.logical_not(lo)]
    vpv = [[jnp.where(lane_sel[s], a, 0.0).astype(BF16) for s, a in enumerate(row)] for row in variants(vp)]
    vcv = [[jnp.where(lane_sel[s], a, 0.0).astype(BF16) for s, a in enumerate(row)] for row in variants(vc)]
    nt = functools.partial(lax.dot_general, dimension_numbers=(((1,), (1,)), ((), ())),
                           preferred_element_type=F32)
    outs = []
    for j in range(ATTN_WIDTH // LANES):
        g = (2 * j) // (N_HEADS // N_KV_HEADS)
        qt = q[:, j * LANES:(j + 1) * LANES]
        acc = jnp.zeros((tq, LANES), F32)
        for s in range(2):
            h = 2 * j + s
            qm = jnp.where(lane_sel[s], qt, 0.0).astype(BF16)
            sp = jnp.where(m_prev, nt(qm, kpv[g][s]), NEG_INF)
            sc = jnp.where(m_cur, nt(qm, kcv[g][s]), NEG_INF)
            sink = sink_ref[h]
            m = jnp.maximum(jnp.maximum(jnp.max(sp, axis=-1, keepdims=True),
                                        jnp.max(sc, axis=-1, keepdims=True)), sink)
            pp = jnp.exp(sp - m)
            pc = jnp.exp(sc - m)
            den = (jnp.sum(pp, axis=-1, keepdims=True) + jnp.sum(pc, axis=-1, keepdims=True)
                   + jnp.exp(sink - m))
            o = (jnp.dot(pp.astype(BF16), vpv[g][s], preferred_element_type=F32)
                 + jnp.dot(pc.astype(BF16), vcv[g][s], preferred_element_type=F32))
            acc = acc + o / den
        outs.append(acc)
    attn = jnp.concatenate(outs, axis=1)
    o_ref[...] = _rms(attn, g_ref[...])
    if decode:
        kw_ref, vw_ref = win_refs
        kw_ref[:WINDOW - tq] = kp[tq:]
        kw_ref[WINDOW - tq:] = kc[:tq]
        vw_ref[:WINDOW - tq] = vp[tq:]
        vw_ref[WINDOW - tq:] = vc[:tq]


def _attention(sinks, q3, k_prev, k_cur, v_prev, v_cur, g, decode):
    b, t, _ = q3.shape
    tq = t if decode else WINDOW
    nb = t // tq
    nseq = math.gcd(b, DECODE_SEQS_PER_STEP) if decode else 1
    cur = lambda i, n: (i, n, 0)
    prev = (lambda i, n: (i, 0, 0)) if decode else (lambda i, n: (i, jnp.maximum(n - 1, 0), 0))
    kvb = (nseq, WINDOW, KV_WIDTH)
    out_specs = [pl.BlockSpec((nseq, tq, ATTN_WIDTH), cur)]
    out_shape = [jax.ShapeDtypeStruct((b, t, ATTN_WIDTH), F32)]
    if decode:
        out_specs += [pl.BlockSpec(kvb, cur), pl.BlockSpec(kvb, cur)]
        out_shape += [jax.ShapeDtypeStruct((b, WINDOW, KV_WIDTH), F32)] * 2
    return pl.pallas_call(
        functools.partial(_attn_kernel, tq=tq, decode=decode, nseq=nseq),
        grid=(b // nseq, nb),
        in_specs=[pl.BlockSpec(memory_space=pltpu.SMEM),
                  pl.BlockSpec((nseq, tq, ATTN_WIDTH), cur),
                  pl.BlockSpec(kvb, prev), pl.BlockSpec(kvb, cur),
                  pl.BlockSpec(kvb, prev), pl.BlockSpec(kvb, cur),
                  pl.BlockSpec((1, ATTN_WIDTH), lambda i, n: (0, 0))],
        out_specs=out_specs,
        out_shape=out_shape,
        compiler_params=_cparams(("arbitrary", "arbitrary")),
        name="attn_decode" if decode else "attn_prompt",
    )(sinks, q3, k_prev, k_cur, v_prev, v_cur, g)


def _s5_param_kernel(lr_ref, li_ref, ls_ref, abr_ref, abi_ref, zr_ref, zi_ref):
    lr = lr_ref[...]
    li = li_ref[...]
    dt = jnp.exp(ls_ref[...])
    mag = jnp.exp(lr * dt)
    ab_re = mag * jnp.cos(li * dt)
    ab_im = mag * jnp.sin(li * dt)
    den = lr * lr + li * li
    abr_ref[...] = ab_re
    abi_ref[...] = ab_im
    zr_ref[...] = ((ab_re - 1.0) * lr + ab_im * li) / den
    zi_ref[...] = (ab_im * lr - (ab_re - 1.0) * li) / den


def _s5_params(lam_re, lam_im, log_step):
    ls = jnp.broadcast_to(log_step[:, None], lam_re.shape)
    outs = pl.pallas_call(
        _s5_param_kernel,
        out_shape=[jax.ShapeDtypeStruct(lam_re.shape, F32)] * 4,
        name="s5_params",
    )(lam_re, lam_im, ls)
    return [o.reshape(1, SSM_LANES) for o in outs]


def _s5_kernel(u_ref, h0r_ref, h0i_ref, brh_ref, brl_ref, bih_ref, bil_ref, zr_ref, zi_ref,
               ar_ref, ai_ref, cr_ref, ci_ref, d_ref, wg_ref, bg_ref, g_ref,
               y_ref, hr_ref, hi_ref, sr_ref, si_ref, *, tt, ns):
    @pl.when(pl.program_id(1) == 0)
    def _():
        hr_ref[0] = h0r_ref[0]
        hi_ref[0] = h0i_ref[0]

    u = u_ref[0]
    for c in range(S5_CHUNKS):
        uc = u[:, c * LANES:(c + 1) * LANES]
        lanes = slice(c * S5_CHUNK_LANES, (c + 1) * S5_CHUNK_LANES)
        pr = _dot3(uc, brh_ref[c], brl_ref[c])
        pi = _dot3(uc, bih_ref[c], bil_ref[c])
        zr = zr_ref[:, lanes]
        zi = zi_ref[:, lanes]
        sr_ref[:, lanes] = zr * pr - zi * pi
        si_ref[:, lanes] = zr * pi + zi * pr
    ar = jnp.broadcast_to(ar_ref[...], (ns, SSM_LANES))
    ai = jnp.broadcast_to(ai_ref[...], (ns, SSM_LANES))

    if ns <= SUBLANES:
        def body(t, carry):
            hr, hi = carry
            rows = pl.ds(t * ns, ns)
            nr = ar * hr - ai * hi + sr_ref[rows, :]
            ni = ar * hi + ai * hr + si_ref[rows, :]
            sr_ref[rows, :] = nr
            si_ref[rows, :] = ni
            return nr, ni

        hr, hi = lax.fori_loop(0, tt, body, (hr_ref[0], hi_ref[0]), unroll=8)
        hr_ref[0] = hr
        hi_ref[0] = hi
    else:
        def body(t, carry):
            rows = pl.ds(pl.multiple_of(t * ns, ns), ns)
            hr = hr_ref[0]
            hi = hi_ref[0]
            nr = ar * hr - ai * hi + sr_ref[rows, :]
            ni = ar * hi + ai * hr + si_ref[rows, :]
            sr_ref[rows, :] = nr
            si_ref[rows, :] = ni
            hr_ref[0] = nr
            hi_ref[0] = ni
            return carry

        lax.fori_loop(0, tt, body, 0)

    ch = []
    for c in range(S5_CHUNKS):
        lanes = slice(c * S5_CHUNK_LANES, (c + 1) * S5_CHUNK_LANES)
        ch.append(_dot1(sr_ref[:, lanes], cr_ref[c]) - _dot1(si_ref[:, lanes], ci_ref[c]))
    y = jnp.concatenate(ch, axis=1) + d_ref[...] * u
    y = _gelu(y)
    gate = _dot1(y, wg_ref[...]) + bg_ref[...]
    y = y * (1.0 / (1.0 + jnp.exp(-gate)))
    y_ref[0] = _rms(y, g_ref[...])


def _s5(u3, h0r, h0i, consts, tt, ns):
    nb, rows, _ = u3.shape
    r = tt * ns
    fixed = lambda b, t: (0, 0)
    tile = lambda b, t: (b, t, 0)
    seq = lambda b, t: (b, 0, 0)
    const_specs = [pl.BlockSpec(c.shape, lambda b, t, nd=c.ndim: (0,) * nd) for c in consts]
    return pl.pallas_call(
        functools.partial(_s5_kernel, tt=tt, ns=ns),
        grid=(nb, rows // r),
        in_specs=[pl.BlockSpec((1, r, SSM_WIDTH), tile),
                  pl.BlockSpec((1, ns, SSM_LANES), seq), pl.BlockSpec((1, ns, SSM_LANES), seq)] + const_specs,
        out_specs=[pl.BlockSpec((1, r, SSM_WIDTH), tile),
                   pl.BlockSpec((1, ns, SSM_LANES), seq), pl.BlockSpec((1, ns, SSM_LANES), seq)],
        out_shape=[jax.ShapeDtypeStruct((nb, rows, SSM_WIDTH), F32),
                   jax.ShapeDtypeStruct((nb, ns, SSM_LANES), F32),
                   jax.ShapeDtypeStruct((nb, ns, SSM_LANES), F32)],
        scratch_shapes=[pltpu.VMEM((r, SSM_LANES), F32), pltpu.VMEM((r, SSM_LANES), F32)],
        compiler_params=_cparams(("arbitrary", "arbitrary")),
        name="s5_ns%d" % ns,
    )(u3, h0r, h0i, *consts)


def _outproj_kernel(a_ref, s_ref, x_ref, wa_ref, ws_ref, g_ref, x1_ref, xn_ref):
    x1 = x_ref[...] + _dot1(a_ref[...], wa_ref[...]) + _dot1(s_ref[...], ws_ref[...])
    x1_ref[...] = x1
    xn_ref[...] = _rms(x1, g_ref[...])


def _out_proj(attn_n, ssm_n, x2d, w_attn, w_ssm, g, tm):
    n = x2d.shape[0]
    row = lambda i: (i, 0)
    fixed = lambda i: (0, 0)
    return pl.pallas_call(
        _outproj_kernel,
        grid=(n // tm,),
        in_specs=[pl.BlockSpec((tm, ATTN_WIDTH), row), pl.BlockSpec((tm, SSM_WIDTH), row),
                  pl.BlockSpec((tm, D_MODEL), row),
                  pl.BlockSpec((ATTN_WIDTH, D_MODEL), fixed), pl.BlockSpec((SSM_WIDTH, D_MODEL), fixed),
                  pl.BlockSpec((1, D_MODEL), fixed)],
        out_specs=[pl.BlockSpec((tm, D_MODEL), row), pl.BlockSpec((tm, D_MODEL), row)],
        out_shape=[jax.ShapeDtypeStruct((n, D_MODEL), F32)] * 2,
        compiler_params=_cparams(("arbitrary",)),
        name="out_proj",
    )(attn_n, ssm_n, x2d, w_attn, w_ssm, g)


def _topk_rows(s, ids, k):
    vals, picks = [], []
    sentinel = jnp.iinfo(jnp.int32).max
    for _ in range(k):
        m = jnp.max(s, axis=0, keepdims=True)
        pick = jnp.min(jnp.where(s == m, ids, sentinel), axis=0, keepdims=True)
        vals.append(m)
        picks.append(pick)
        s = jnp.where(ids == pick, -jnp.inf, s)
    return vals, picks


def _cand_layout():
    pieces = [(0, 1, 0, PEER_TOPK)]
    pieces += [(a, a + 1, 0, SUBLANES) for a in range(1, SUBLANES)]
    pieces += [(SUBLANES, PEER_TOPK, 0, 1)]
    return pieces


def _topk_kernel(xn_ref, wqh_ref, wql_ref, keys_ref, row_ref, shift_ref, gate_ref, *, tm):
    h = pl.program_id(1)
    q = _dot3(xn_ref[...], wqh_ref[h], wql_ref[h])
    key_id = lax.broadcasted_iota(I32, (N_KEYS, tm), 0)
    sub_v, sub_i = [], []
    for c in range(2):
        q_hi, q_lo = _split(q[:, c * PEER_HALF:(c + 1) * PEER_HALF])
        keys = keys_ref[2 * h + c]
        k_hi, k_lo = _split(keys)
        d = functools.partial(lax.dot_general, dimension_numbers=(((1,), (1,)), ((), ())),
                              preferred_element_type=F32)
        s = d(k_hi, q_hi) + (d(k_lo, q_hi) + d(k_hi, q_lo))
        vals, picks = _topk_rows(s, key_id, PEER_TOPK)
        sub_v.append(vals)
        sub_i.append(picks)
    cs, ce, cf = [], [], []

    def rows_of(lst, lo, hi):
        return lst[lo] if hi - lo == 1 else jnp.concatenate(lst[lo:hi], axis=0)

    for a_lo, a_hi, b_lo, b_hi in _cand_layout():
        na, nbb = a_hi - a_lo, b_hi - b_lo
        rows = max(na, nbb)
        cs.append(rows_of(sub_v[0], a_lo, a_hi) + rows_of(sub_v[1], b_lo, b_hi))
        ce.append(rows_of(sub_i[0], a_lo, a_hi) * N_KEYS + rows_of(sub_i[1], b_lo, b_hi))
        r = lax.broadcasted_iota(I32, (rows, tm), 0)
        cf.append((a_lo + r) * PEER_TOPK + b_lo if na > 1 else a_lo * PEER_TOPK + b_lo + r)
    cand_s = jnp.concatenate(cs, axis=0)
    cand_e = jnp.concatenate(ce, axis=0)
    cand_f = jnp.concatenate(cf, axis=0)
    best_v, best_e = [], []
    sentinel = jnp.iinfo(jnp.int32).max
    for _ in range(PEER_TOPK):
        m = jnp.max(cand_s, axis=0, keepdims=True)
        f = jnp.min(jnp.where(cand_s == m, cand_f, sentinel), axis=0, keepdims=True)
        hit = cand_f == f
        best_v.append(m)
        best_e.append(jnp.max(jnp.where(hit, cand_e, -1), axis=0, keepdims=True))
        cand_s = jnp.where(hit, -jnp.inf, cand_s)
    bv = jnp.concatenate(best_v, axis=0)
    be = jnp.concatenate(best_e, axis=0)
    ex = jnp.exp(bv - bv[0:1])
    gate_ref[...] = ex / jnp.sum(ex, axis=0, keepdims=True)
    row_ref[...] = lax.shift_right_logical(be, 1) * SUBLANES
    shift_ref[...] = ((1 - (be & 1)) * 16).astype(F32)


def _peer_topk(xn2d, wq_hi, wq_lo, keys, tm):
    n = xn2d.shape[0]
    out = lambda i, h: (h, i)
    return pl.pallas_call(
        functools.partial(_topk_kernel, tm=tm),
        grid=(n // tm, PEER_HEADS),
        in_specs=[pl.BlockSpec((tm, D_MODEL), lambda i, h: (i, 0)),
                  pl.BlockSpec(wq_hi.shape, lambda i, h: (0, 0, 0)),
                  pl.BlockSpec(wq_lo.shape, lambda i, h: (0, 0, 0)),
                  pl.BlockSpec(keys.shape, lambda i, h: (0, 0, 0))],
        out_specs=[pl.BlockSpec((PEER_TOPK, tm), out)] * 3,
        out_shape=[jax.ShapeDtypeStruct((N_PAIRS, n), I32), jax.ShapeDtypeStruct((N_PAIRS, n), F32),
                   jax.ShapeDtypeStruct((N_PAIRS, n), F32)],
        compiler_params=_cparams(("arbitrary", "arbitrary")),
        name="peer_topk",
    )(xn2d, wq_hi, wq_lo, keys)


def _pack_kernel(t_ref, o_ref):
    even = t_ref[:, 0].astype(BF16).astype(F32)
    odd = t_ref[:, 1].astype(BF16).astype(F32)
    hi = pltpu.bitcast(odd, I32) & HI_MASK
    lo = lax.shift_right_logical(pltpu.bitcast(even, I32), 16)
    o_ref[...] = hi | lo


def _pack_table(table, tb=256):
    e = table.shape[0]
    t4 = table.reshape(e // 2, 2, SUBLANES, LANES)
    packed = pl.pallas_call(
        _pack_kernel,
        grid=(e // 2 // tb,),
        in_specs=[pl.BlockSpec((tb, 2, SUBLANES, LANES), lambda i: (i, 0, 0, 0))],
        out_specs=pl.BlockSpec((tb, SUBLANES, LANES), lambda i: (i, 0, 0)),
        out_shape=jax.ShapeDtypeStruct((e // 2, SUBLANES, LANES), I32),
        compiler_params=_cparams(("arbitrary",)),
        name="pack_table",
    )(t4)
    return packed.reshape(e // 2 * SUBLANES, LANES)


def _tile(tbl, offset):
    return tbl[pl.ds(pl.multiple_of(offset, SUBLANES), SUBLANES), :]


def _expert_row(word, shift):
    return pltpu.bitcast((word << shift) & HI_MASK, F32)


def _row8(ref, k):
    return jnp.broadcast_to(ref[k:k + 1, :], (SUBLANES, LANES))


def _fold(p, steps):
    sub = lax.broadcasted_iota(I32, p[0].shape, 0)
    for step in steps:
        first = (sub % (2 * step)) < step
        n = len(p) // 2
        p = [jnp.where(first, p[j] + pltpu.roll(p[j], SUBLANES - step, 0),
                       p[j + n] + pltpu.roll(p[j + n], step, 0)) for j in range(n)]
    return p


def _column(block_ref, tile, tm, t):
    blk = block_ref[:, tile * tm:(tile + 1) * tm]
    lane = lax.broadcasted_iota(I32, blk.shape, 1)
    col = jnp.sum(jnp.where(lane == t, blk, 0.0), axis=1, keepdims=True)
    return jnp.broadcast_to(col, (N_PAIRS, LANES))


def _rows_copy(row_hbm, buf, sem, tile, tm):
    return pltpu.make_async_copy(row_hbm.at[:, pl.ds(tile * tm, tm)], buf, sem)


def _for_each_tile(row_hbm, bufs, sems, tm, body):
    i = pl.program_id(0)
    tps = len(bufs)
    first = i * tps
    total = pl.num_programs(0) * tps

    def copy(j, tile):
        return _rows_copy(row_hbm, bufs[j], sems.at[j], tile, tm)

    @pl.when(i == 0)
    def _():
        copy(0, 0).start()

    for j in range(tps):
        nxt = (j + 1) % tps
        if tps == 1:
            copy(0, first).wait()
            body(0, bufs[0])

            @pl.when(first + 1 < total)
            def _():
                copy(0, first + 1).start()
        else:
            @pl.when(first + j + 1 < total)
            def _():
                copy(nxt, first + j + 1).start()

            copy(j, first + j).wait()
            body(j, bufs[j])


def _load_table(tbl_hbm, tbl_vmem, sem):
    @pl.when(pl.program_id(0) == 0)
    def _():
        c = pltpu.make_async_copy(tbl_hbm, tbl_vmem, sem)
        c.start()
        c.wait()


ACT_RING = 8
EXPAND_RING = 8


def _ring_loop(tm, ring, expand, gather):
    n = len(ring)
    ahead = n // 2
    for r in range(ahead):
        expand(r, *ring[r])

    def ring_pass(j, carry):
        t0 = n * j
        for r in range(n):
            gather(t0 + r, *ring[r])
            expand(jnp.minimum(t0 + r + ahead, tm - 1), *ring[(r + ahead) % n])
        return carry

    lax.fori_loop(0, tm // n, ring_pass, 0)


def _tag_parity(w, is_odd):
    bits = (pltpu.bitcast(w, I32) & -2) | jnp.where(is_odd, 1, 0)
    return pltpu.bitcast(bits, F32)


def _tiles_per_step(n, tm):
    return 2 if (n // tm) % 2 == 0 else 1


def _peer_act_kernel(row_hbm, x_ref, shift_ref, gate_ref, tbl_hbm, w_ref, tbl, *scratch, tm, tps):
    bufs = scratch[:tps]
    sh_ring = scratch[tps:tps + ACT_RING]
    acc, sems = scratch[tps + ACT_RING:]
    _load_table(tbl_hbm, tbl, sems.at[tps])
    lane = lax.broadcasted_iota(I32, (SUBLANES, tm), 1)
    n_groups = N_PAIRS // SUBLANES

    def tile_body(tile, rows):
        def expand(t, sh):
            sh[...] = _column(shift_ref, tile, tm, t).astype(I32)

        def gather(t, sh):
            xt = x_ref[tile * tm + t]
            here = lane == t
            for g in range(n_groups):
                prods = []
                for j in range(SUBLANES):
                    k = g * SUBLANES + j
                    prods.append(_expert_row(_tile(tbl, rows.at[k][t]), _row8(sh, k)) * xt)
                folded = _fold(prods, (4, 2, 1))[0]
                acc[g] = jnp.where(here, jnp.sum(folded, axis=1, keepdims=True), acc[g])

        acc[...] = jnp.zeros(acc.shape, F32)
        _ring_loop(tm, [(sh,) for sh in sh_ring], expand, gather)
        act = jnp.concatenate([acc[g] for g in range(n_groups)], axis=0)
        cols = slice(tile * tm, (tile + 1) * tm)
        w_ref[:, cols] = _tag_parity(gate_ref[:, cols] * _gelu(act), shift_ref[:, cols] == 0.0)

    _for_each_tile(row_hbm, bufs, sems, tm, tile_body)


def _peer_act(rows, xn3, shift, gate, tbl, tm):
    n = xn3.shape[0]
    tps = _tiles_per_step(n, tm)
    step = tps * tm
    col = pl.BlockSpec((N_PAIRS, step), lambda i: (0, i))
    return pl.pallas_call(
        functools.partial(_peer_act_kernel, tm=tm, tps=tps),
        grid=(n // step,),
        in_specs=[pl.BlockSpec(memory_space=pl.ANY),
                  pl.BlockSpec((step, SUBLANES, LANES), lambda i: (i, 0, 0)),
                  col, col, pl.BlockSpec(memory_space=pl.ANY)],
        out_specs=col,
        out_shape=jax.ShapeDtypeStruct((N_PAIRS, n), F32),
        scratch_shapes=([pltpu.VMEM(tbl.shape, I32)] + [pltpu.SMEM((N_PAIRS, tm), I32)] * tps
                        + [pltpu.VMEM((N_PAIRS, LANES), I32)] * ACT_RING
                        + [pltpu.VMEM((N_PAIRS // SUBLANES, SUBLANES, tm), F32),
                           pltpu.SemaphoreType.DMA((tps + 1,))]),
        compiler_params=_cparams(("arbitrary",)),
        name="peer_act",
    )(rows, xn3, shift, gate, tbl)


def _peer_out_kernel(row_hbm, w_ref, x1_ref, tbl_hbm, o_ref, tbl, *scratch, tm, tps):
    bufs = scratch[:tps]
    sh_ring = scratch[tps:tps + EXPAND_RING]
    wb_ring = scratch[tps + EXPAND_RING:tps + 2 * EXPAND_RING]
    sems = scratch[tps + 2 * EXPAND_RING]
    _load_table(tbl_hbm, tbl, sems.at[tps])
    n_acc = 4

    def tile_body(tile, rows):
        def expand(t, sh, wb):
            col = _column(w_ref, tile, tm, t)
            wb[...] = col
            sh[...] = ((pltpu.bitcast(col, I32) & 1) ^ 1) << 4

        def gather(t, sh, wb):
            accs = [x1_ref[tile * tm + t]] + [jnp.zeros((SUBLANES, LANES), F32)] * (n_acc - 1)
            for k in range(N_PAIRS):
                v_row = _expert_row(_tile(tbl, rows.at[k][t]), _row8(sh, k))
                accs[k % n_acc] = accs[k % n_acc] + _row8(wb, k) * v_row
            o_ref[tile * tm + t] = (accs[0] + accs[1]) + (accs[2] + accs[3])

        _ring_loop(tm, list(zip(sh_ring, wb_ring)), expand, gather)

    _for_each_tile(row_hbm, bufs, sems, tm, tile_body)


def _peer_out(rows, w, x1_3, tbl, tm):
    n = x1_3.shape[0]
    tps = _tiles_per_step(n, tm)
    step = tps * tm
    blk = pl.BlockSpec((step, SUBLANES, LANES), lambda i: (i, 0, 0))
    col = pl.BlockSpec((N_PAIRS, step), lambda i: (0, i))
    return pl.pallas_call(
        functools.partial(_peer_out_kernel, tm=tm, tps=tps),
        grid=(n // step,),
        in_specs=[pl.BlockSpec(memory_space=pl.ANY), col, blk, pl.BlockSpec(memory_space=pl.ANY)],
        out_specs=blk,
        out_shape=jax.ShapeDtypeStruct(x1_3.shape, F32),
        scratch_shapes=([pltpu.VMEM(tbl.shape, I32)] + [pltpu.SMEM((N_PAIRS, tm), I32)] * tps
                        + [pltpu.VMEM((N_PAIRS, LANES), I32)] * EXPAND_RING
                        + [pltpu.VMEM((N_PAIRS, LANES), F32)] * EXPAND_RING
                        + [pltpu.SemaphoreType.DMA((tps + 1,))]),
        compiler_params=_cparams(("arbitrary",)),
        name="peer_out",
    )(rows, w, x1_3, tbl)


def _final_kernel(x_ref, g_ref, o_ref):
    o_ref[...] = _rms(x_ref[...], g_ref[...])


def _final_norm(x2d, g, tm):
    n = x2d.shape[0]
    return pl.pallas_call(
        _final_kernel,
        grid=(n // tm,),
        in_specs=[pl.BlockSpec((tm, D_MODEL), lambda i: (i, 0)), pl.BlockSpec((1, D_MODEL), lambda i: (0, 0))],
        out_specs=pl.BlockSpec((tm, D_MODEL), lambda i: (i, 0)),
        out_shape=jax.ShapeDtypeStruct(x2d.shape, F32),
        compiler_params=_cparams(("arbitrary",)),
        name="final_norm",
    )(x2d, g)


def _block_diag_in(b):
    eye = jnp.eye(S5_CHUNK_GROUPS, dtype=F32)
    b4 = b.reshape(S5_CHUNKS, S5_CHUNK_GROUPS, SSM_STATE, SSM_GROUP)
    return jnp.einsum('cgnp,gh->cgphn', b4, eye).reshape(S5_CHUNKS, LANES, S5_CHUNK_LANES)


def _block_diag_out(c):
    eye = jnp.eye(S5_CHUNK_GROUPS, dtype=F32)
    c4 = c.reshape(S5_CHUNKS, S5_CHUNK_GROUPS, SSM_GROUP, SSM_STATE)
    return jnp.einsum('cgpn,gh->cgnhp', c4, eye).reshape(S5_CHUNKS, S5_CHUNK_LANES, LANES)


def _layer_weights(norm_mix, w_in, b_in, attn_sinks, lam_re, lam_im, log_step, b_re, b_im, c_re, c_im,
                   d_skip, w_glu, b_glu, norm_attn_out, norm_ssm_out, w_out, norm_ffn, w_query, sub_keys,
                   u_table, v_table):
    w = {}
    w['norm_mix'] = norm_mix.reshape(1, D_MODEL)
    w['w_in'] = _split(w_in)
    w['b_in'] = b_in.reshape(1, IN_WIDTH)
    w['sinks'] = attn_sinks
    ab_re, ab_im, z_re, z_im = _s5_params(lam_re, lam_im, log_step)
    w['s5'] = (list(_split(_block_diag_in(b_re))) + list(_split(_block_diag_in(b_im)))
               + [z_re, z_im, ab_re, ab_im,
                  _block_diag_out(c_re).astype(BF16), _block_diag_out(c_im).astype(BF16),
                  d_skip.reshape(1, SSM_WIDTH), w_glu.astype(BF16), b_glu.reshape(1, SSM_WIDTH),
                  norm_ssm_out.reshape(1, SSM_WIDTH)])
    w['norm_attn_out'] = norm_attn_out.reshape(1, ATTN_WIDTH)
    w['w_out_attn'] = w_out[:ATTN_WIDTH].astype(BF16)
    w['w_out_ssm'] = w_out[ATTN_WIDTH:].astype(BF16)
    w['norm_ffn'] = norm_ffn.reshape(1, D_MODEL)
    wq = w_query.reshape(D_MODEL, PEER_HEADS, PEER_QDIM).transpose(1, 0, 2)
    w['w_query'] = _split(wq)
    w['keys'] = sub_keys.reshape(PEER_HEADS * 2, N_KEYS, PEER_HALF)
    w['u_tbl'] = _pack_table(u_table)
    w['v_tbl'] = _pack_table(v_table)
    return w


def _peer(xn2d, x1_2d, w, tm_topk, tm_gather):
    n = xn2d.shape[0]
    rows, shift, gate = _peer_topk(xn2d, w['w_query'][0], w['w_query'][1], w['keys'], tm_topk)
    wts = _peer_act(rows, xn2d.reshape(n, SUBLANES, LANES), shift, gate, w['u_tbl'], tm_gather)
    out = _peer_out(rows, wts, x1_2d.reshape(n, SUBLANES, LANES), w['v_tbl'], tm_gather)
    return out.reshape(n, D_MODEL)


def _token_tile(n, cap):
    t = cap
    while n % t:
        t //= 2
    return t


def _mix_and_ffn(x2d, attn_n, ssm_n, w, norm_final):
    n = x2d.shape[0]
    x1, xn2 = _out_proj(attn_n, ssm_n, x2d, w['w_out_attn'], w['w_out_ssm'], w['norm_ffn'], _token_tile(n, 512))
    x2 = _peer(xn2, x1, w, _token_tile(n, 256), _token_tile(n, 128))
    return x2


def _prompt_layer(x, w):
    b, t, _ = x.shape
    n = b * t
    x2d = x.reshape(n, D_MODEL)
    tm = _token_tile(t, 512)
    tables = _rope_tables(jnp.arange(t, dtype=I32))
    q, k, v, u = _in_proj(x2d, tables, t // tm, w['norm_mix'], w['w_in'][0], w['w_in'][1], w['b_in'], tm)
    k3 = k.reshape(b, t, KV_WIDTH)
    v3 = v.reshape(b, t, KV_WIDTH)
    (attn_n,) = _attention(w['sinks'], q.reshape(b, t, ATTN_WIDTH), k3, k3, v3, v3, w['norm_attn_out'], False)
    h0 = jnp.zeros((b, 1, SSM_LANES), F32)
    ssm_n, h_re, h_im = _s5(u.reshape(b, t, SSM_WIDTH), h0, h0, w['s5'], _token_tile(t, 256), 1)
    x2 = _mix_and_ffn(x2d, attn_n.reshape(n, ATTN_WIDTH), ssm_n.reshape(n, SSM_WIDTH), w, None)
    k_win = k3[:, -WINDOW:].reshape(b, WINDOW, N_KV_HEADS, HEAD_DIM)
    v_win = v3[:, -WINDOW:].reshape(b, WINDOW, N_KV_HEADS, HEAD_DIM)
    st = lambda h: h.reshape(b, N_SSM_GROUPS, SSM_STATE)
    return x2.reshape(b, t, D_MODEL), k_win, v_win, st(h_re), st(h_im)


def _sample_layer(x, k_buf, v_buf, h0_re, h0_im, w, past_len):
    b, t, _ = x.shape
    n = b * t
    x2d = x.reshape(n, D_MODEL)
    tm = _token_tile(n, 512)
    pos = past_len + jnp.arange(t, dtype=I32)
    tables = tuple(jnp.tile(a, (tm // t, 1)) for a in _rope_tables(pos))
    q, k, v, u = _in_proj(x2d, tables, 1, w['norm_mix'], w['w_in'][0], w['w_in'][1], w['b_in'], tm)
    pad = lambda a: jnp.pad(a.reshape(b, t, KV_WIDTH), ((0, 0), (0, WINDOW - t), (0, 0)))
    attn_n, k_win, v_win = _attention(w['sinks'], q.reshape(b, t, ATTN_WIDTH),
                                      k_buf.reshape(b, WINDOW, KV_WIDTH), pad(k),
                                      v_buf.reshape(b, WINDOW, KV_WIDTH), pad(v), w['norm_attn_out'], True)
    ns = min(b, S5_DECODE_SEQS)
    nbk = b // ns
    u_tm = u.reshape(nbk, ns, t, SSM_WIDTH).transpose(0, 2, 1, 3).reshape(nbk, t * ns, SSM_WIDTH)
    ssm_tm, h_re, h_im = _s5(u_tm, h0_re.reshape(nbk, ns, SSM_LANES), h0_im.reshape(nbk, ns, SSM_LANES),
                             w['s5'], t, ns)
    ssm_n = ssm_tm.reshape(nbk, t, ns, SSM_WIDTH).transpose(0, 2, 1, 3).reshape(n, SSM_WIDTH)
    x2 = _mix_and_ffn(x2d, attn_n.reshape(n, ATTN_WIDTH), ssm_n, w, None)
    win = lambda a: a.reshape(b, WINDOW, N_KV_HEADS, HEAD_DIM)
    st = lambda h: h.reshape(b, N_SSM_GROUPS, SSM_STATE)
    return x2.reshape(b, t, D_MODEL), win(k_win), win(v_win), st(h_re), st(h_im)


PAST_LEN = 16384
S5_DECODE_SEQS = 64
DECODE_SEQS_PER_STEP = 8


def kernel(x_prompt, x_sample, cache_k, cache_v, state_ssm_re, state_ssm_im, norm_mix, w_in, b_in, attn_sinks, ssm_lam_re, ssm_lam_im, ssm_log_step, ssm_b_re, ssm_b_im, ssm_c_re, ssm_c_im, ssm_d, ssm_w_glu, ssm_b_glu, norm_attn_out, norm_ssm_out, w_out, norm_ffn, peer_w_query, peer_sub_keys, peer_u, peer_v, norm_final):
    depth = norm_mix.shape[0]
    xp, xs = x_prompt, x_sample
    outs = [[] for _ in range(8)]
    for l in range(depth):
        w = _layer_weights(*[a[l] for a in (norm_mix, w_in, b_in, attn_sinks, ssm_lam_re, ssm_lam_im,
                                            ssm_log_step, ssm_b_re, ssm_b_im, ssm_c_re, ssm_c_im, ssm_d,
                                            ssm_w_glu, ssm_b_glu, norm_attn_out, norm_ssm_out, w_out,
                                            norm_ffn, peer_w_query, peer_sub_keys, peer_u, peer_v)])
        xs, k2, v2, r2, i2 = _sample_layer(xs, cache_k[l], cache_v[l], state_ssm_re[l], state_ssm_im[l],
                                           w, PAST_LEN)
        xp, k1, v1, r1, i1 = _prompt_layer(xp, w)
        for lst, a in zip(outs, (k1, v1, r1, i1, k2, v2, r2, i2)):
            lst.append(a)
    g = norm_final.reshape(1, D_MODEL)
    yp = _final_norm(xp.reshape(-1, D_MODEL), g, 512).reshape(xp.shape)
    ys = _final_norm(xs.reshape(-1, D_MODEL), g, 512).reshape(xs.shape)
    return (yp, ys) + tuple(jnp.stack(o) for o in outs)
```

```python
import functools
import math

import jax
import jax.numpy as jnp
from jax import lax
from jax.experimental import pallas as pl
from jax.experimental.pallas import tpu as pltpu

F32 = jnp.float32
BF16 = jnp.bfloat16
I32 = jnp.int32

D_MODEL = 1024
N_HEADS = 8
N_KV_HEADS = 2
HEAD_DIM = 64
ATTN_WIDTH = N_HEADS * HEAD_DIM
KV_WIDTH = N_KV_HEADS * HEAD_DIM
WINDOW = 128
ROT_DIM = HEAD_DIM // 4
ROPE_THETA = 500000.0
NEG_INF = -1e30
SSM_WIDTH = D_MODEL - ATTN_WIDTH
SSM_GROUP = 16
N_SSM_GROUPS = SSM_WIDTH // SSM_GROUP
SSM_STATE = 64
SSM_LANES = N_SSM_GROUPS * SSM_STATE
IN_WIDTH = ATTN_WIDTH + 2 * KV_WIDTH + SSM_WIDTH
PEER_HEADS = 8
N_KEYS = 128
PEER_TOPK = 16
PEER_QDIM = 256
PEER_HALF = PEER_QDIM // 2
N_PAIRS = PEER_HEADS * PEER_TOPK
RMS_EPS = 1e-5

LANES = 128
SUBLANES = 8
VMEM_LIMIT = 56 * 1024 * 1024

HI_MASK = -65536

S5_CHUNK_GROUPS = LANES // SSM_GROUP
S5_CHUNKS = N_SSM_GROUPS // S5_CHUNK_GROUPS
S5_CHUNK_LANES = S5_CHUNK_GROUPS * SSM_STATE


def _cparams(sem):
    return pltpu.CompilerParams(dimension_semantics=sem, vmem_limit_bytes=VMEM_LIMIT)


def _split(w):
    hi = w.astype(BF16)
    lo = (w - hi.astype(F32)).astype(BF16)
    return hi, lo


def _dot1(a, b):
    return jnp.dot(a.astype(BF16), b, preferred_element_type=F32)


def _dot3(a, b_hi, b_lo):
    a_hi, a_lo = _split(a)
    d = functools.partial(jnp.dot, preferred_element_type=F32)
    return d(a_hi, b_hi) + (d(a_lo, b_hi) + d(a_hi, b_lo))


def _dot3_nt(a_hi, a_lo, b):
    b_hi, b_lo = _split(b)
    d = functools.partial(lax.dot_general, dimension_numbers=(((1,), (1,)), ((), ())),
                          preferred_element_type=F32)
    return d(a_hi, b_hi) + (d(a_lo, b_hi) + d(a_hi, b_lo))


def _gelu(x):
    return 0.5 * x * (1.0 + lax.erf(x * (2.0 ** -0.5)))


def _rms(x, g):
    return x * lax.rsqrt(jnp.mean(x * x, axis=-1, keepdims=True) + RMS_EPS) * g


def _inproj_kernel(x_ref, g_ref, whi_ref, wlo_ref, b_ref, c_ref, sa_ref, sb_ref,
                   q_ref, k_ref, v_ref, u_ref):
    xn = _rms(x_ref[...], g_ref[...])
    proj = _dot3(xn, whi_ref[...], wlo_ref[...]) + b_ref[...]
    c = c_ref[...]
    sa = sa_ref[...]
    sb = sb_ref[...]

    def rope(t):
        return t * c + pltpu.roll(t, LANES - ROT_DIM // 2, 1) * sa + pltpu.roll(t, ROT_DIM // 2, 1) * sb

    for j in range(ATTN_WIDTH // LANES):
        q_ref[:, j * LANES:(j + 1) * LANES] = rope(proj[:, j * LANES:(j + 1) * LANES])
    k_ref[...] = rope(proj[:, ATTN_WIDTH:ATTN_WIDTH + KV_WIDTH])
    v_ref[...] = proj[:, ATTN_WIDTH + KV_WIDTH:ATTN_WIDTH + 2 * KV_WIDTH]
    u_ref[...] = proj[:, ATTN_WIDTH + 2 * KV_WIDTH:]


def _rope_tables(pos):
    half = ROT_DIM // 2
    inv_freq = ROPE_THETA ** (-jnp.arange(half, dtype=F32) * 2.0 / ROT_DIM)
    ang = pos.astype(F32)[:, None] * inv_freq[None, :]
    cos = jnp.cos(ang)
    sin = jnp.sin(ang)
    t = pos.shape[0]
    one = jnp.ones((t, HEAD_DIM - ROT_DIM), F32)
    zero = jnp.zeros((t, HEAD_DIM - ROT_DIM), F32)
    zh = jnp.zeros((t, half), F32)
    c = jnp.concatenate([cos, cos, one], axis=1)
    sa = jnp.concatenate([-sin, zh, zero], axis=1)
    sb = jnp.concatenate([zh, sin, zero], axis=1)
    return tuple(jnp.tile(a, (1, LANES // HEAD_DIM)) for a in (c, sa, sb))


def _in_proj(x2d, pos_tables, n_pos_tiles, g, w_hi, w_lo, b, tm):
    n = x2d.shape[0]
    c, sa, sb = pos_tables
    row = lambda i: (i, 0)
    fixed = lambda i: (0, 0)
    pos_map = lambda i: (i % n_pos_tiles, 0)
    return pl.pallas_call(
        _inproj_kernel,
        grid=(n // tm,),
        in_specs=[pl.BlockSpec((tm, D_MODEL), row), pl.BlockSpec((1, D_MODEL), fixed),
                  pl.BlockSpec((D_MODEL, IN_WIDTH), fixed), pl.BlockSpec((D_MODEL, IN_WIDTH), fixed),
                  pl.BlockSpec((1, IN_WIDTH), fixed),
                  pl.BlockSpec((tm, LANES), pos_map), pl.BlockSpec((tm, LANES), pos_map),
                  pl.BlockSpec((tm, LANES), pos_map)],
        out_specs=[pl.BlockSpec((tm, ATTN_WIDTH), row), pl.BlockSpec((tm, KV_WIDTH), row),
                   pl.BlockSpec((tm, KV_WIDTH), row), pl.BlockSpec((tm, SSM_WIDTH), row)],
        out_shape=[jax.ShapeDtypeStruct((n, ATTN_WIDTH), F32), jax.ShapeDtypeStruct((n, KV_WIDTH), F32),
                   jax.ShapeDtypeStruct((n, KV_WIDTH), F32), jax.ShapeDtypeStruct((n, SSM_WIDTH), F32)],
        compiler_params=_cparams(("arbitrary",)),
        name="in_proj",
    )(x2d, g, w_hi, w_lo, b, c, sa, sb)


def _attn_kernel(sink_ref, q_ref, kp_ref, kc_ref, vp_ref, vc_ref, g_ref, o_ref, *win_refs, tq, decode, nseq):
    for b in range(nseq):
        _attn_one(sink_ref, q_ref.at[b], kp_ref.at[b], kc_ref.at[b], vp_ref.at[b], vc_ref.at[b], g_ref,
                  o_ref.at[b], *[w.at[b] for w in win_refs], tq=tq, decode=decode)


def _attn_one(sink_ref, q_ref, kp_ref, kc_ref, vp_ref, vc_ref, g_ref, o_ref, *win_refs, tq, decode):
    q = q_ref[...] * (HEAD_DIM ** -0.5)
    kp = kp_ref[...]
    kc = kc_ref[...]
    vp = vp_ref[...]
    vc = vc_ref[...]
    half = LANES // 2
    lane = lax.broadcasted_iota(I32, (1, LANES), 1)
    lo = lane < half
    qi = lax.broadcasted_iota(I32, (tq, WINDOW), 0)
    kj = lax.broadcasted_iota(I32, (tq, WINDOW), 1)
    first_off = 0 if decode else jnp.where(pl.program_id(1) > 0, 0, WINDOW)
    m_prev = kj > qi + first_off
    m_cur = kj <= qi

    def variants(t):
        r = pltpu.roll(t, half, 1)
        return [[t, r], [r, t]]

    kpv = [[a.astype(BF16) for a in row] for row in variants(kp)]
    kcv = [[a.astype(BF16) for a in row] for row in variants(kc)]
    lane_sel = [lo, jnp.logical_not(lo)]
    vpv = [[jnp.where(lane_sel[s], a, 0.0).astype(BF16) for s, a in enumerate(row)] for row in variants(vp)]
    vcv = [[jnp.where(lane_sel[s], a, 0.0).astype(BF16) for s, a in enumerate(row)] for row in variants(vc)]
    nt = functools.partial(lax.dot_general, dimension_numbers=(((1,), (1,)), ((), ())),
                           preferred_element_type=F32)
    outs = []
    for j in range(ATTN_WIDTH // LANES):
        g = (2 * j) // (N_HEADS // N_KV_HEADS)
        qt = q[:, j * LANES:(j + 1) * LANES]
        acc = jnp.zeros((tq, LANES), F32)
        for s in range(2):
            h = 2 * j + s
            qm = jnp.where(lane_sel[s], qt, 0.0).astype(BF16)
            sp = jnp.where(m_prev, nt(qm, kpv[g][s]), NEG_INF)
            sc = jnp.where(m_cur, nt(qm, kcv[g][s]), NEG_INF)
            sink = sink_ref[h]
            m = jnp.maximum(jnp.maximum(jnp.max(sp, axis=-1, keepdims=True),
                                        jnp.max(sc, axis=-1, keepdims=True)), sink)
            pp = jnp.exp(sp - m)
            pc = jnp.exp(sc - m)
            den = (jnp.sum(pp, axis=-1, keepdims=True) + jnp.sum(pc, axis=-1, keepdims=True)
                   + jnp.exp(sink - m))
            o = (jnp.dot(pp.astype(BF16), vpv[g][s], preferred_element_type=F32)
                 + jnp.dot(pc.astype(BF16), vcv[g][s], preferred_element_type=F32))
            acc = acc + o / den
        outs.append(acc)
    attn = jnp.concatenate(outs, axis=1)
    o_ref[...] = _rms(attn, g_ref[...])
    if decode:
        kw_ref, vw_ref = win_refs
        kw_ref[:WINDOW - tq] = kp[tq:]
        kw_ref[WINDOW - tq:] = kc[:tq]
        vw_ref[:WINDOW - tq] = vp[tq:]
        vw_ref[WINDOW - tq:] = vc[:tq]


def _attention(sinks, q3, k_prev, k_cur, v_prev, v_cur, g, decode):
    b, t, _ = q3.shape
    tq = t if decode else WINDOW
    nb = t // tq
    nseq = math.gcd(b, DECODE_SEQS_PER_STEP) if decode else 1
    cur = lambda i, n: (i, n, 0)
    prev = (lambda i, n: (i, 0, 0)) if decode else (lambda i, n: (i, jnp.maximum(n - 1, 0), 0))
    kvb = (nseq, WINDOW, KV_WIDTH)
    out_specs = [pl.BlockSpec((nseq, tq, ATTN_WIDTH), cur)]
    out_shape = [jax.ShapeDtypeStruct((b, t, ATTN_WIDTH), F32)]
    if decode:
        out_specs += [pl.BlockSpec(kvb, cur), pl.BlockSpec(kvb, cur)]
        out_shape += [jax.ShapeDtypeStruct((b, WINDOW, KV_WIDTH), F32)] * 2
    return pl.pallas_call(
        functools.partial(_attn_kernel, tq=tq, decode=decode, nseq=nseq),
        grid=(b // nseq, nb),
        in_specs=[pl.BlockSpec(memory_space=pltpu.SMEM),
                  pl.BlockSpec((nseq, tq, ATTN_WIDTH), cur),
                  pl.BlockSpec(kvb, prev), pl.BlockSpec(kvb, cur),
                  pl.BlockSpec(kvb, prev), pl.BlockSpec(kvb, cur),
                  pl.BlockSpec((1, ATTN_WIDTH), lambda i, n: (0, 0))],
        out_specs=out_specs,
        out_shape=out_shape,
        compiler_params=_cparams(("arbitrary", "arbitrary")),
        name="attn_decode" if decode else "attn_prompt",
    )(sinks, q3, k_prev, k_cur, v_prev, v_cur, g)


def _s5_param_kernel(lr_ref, li_ref, ls_ref, abr_ref, abi_ref, zr_ref, zi_ref):
    lr = lr_ref[...]
    li = li_ref[...]
    dt = jnp.exp(ls_ref[...])
    mag = jnp.exp(lr * dt)
    ab_re = mag * jnp.cos(li * dt)
    ab_im = mag * jnp.sin(li * dt)
    den = lr * lr + li * li
    abr_ref[...] = ab_re
    abi_ref[...] = ab_im
    zr_ref[...] = ((ab_re - 1.0) * lr + ab_im * li) / den
    zi_ref[...] = (ab_im * lr - (ab_re - 1.0) * li) / den


def _s5_params(lam_re, lam_im, log_step):
    ls = jnp.broadcast_to(log_step[:, None], lam_re.shape)
    outs = pl.pallas_call(
        _s5_param_kernel,
        out_shape=[jax.ShapeDtypeStruct(lam_re.shape, F32)] * 4,
        name="s5_params",
    )(lam_re, lam_im, ls)
    return [o.reshape(1, SSM_LANES) for o in outs]


def _s5_kernel(u_ref, h0r_ref, h0i_ref, brh_ref, brl_ref, bih_ref, bil_ref, zr_ref, zi_ref,
               ar_ref, ai_ref, cr_ref, ci_ref, d_ref, wg_ref, bg_ref, g_ref,
               y_ref, hr_ref, hi_ref, sr_ref, si_ref, *, tt, ns):
    @pl.when(pl.program_id(1) == 0)
    def _():
        hr_ref[0] = h0r_ref[0]
        hi_ref[0] = h0i_ref[0]

    u = u_ref[0]
    for c in range(S5_CHUNKS):
        uc = u[:, c * LANES:(c + 1) * LANES]
        lanes = slice(c * S5_CHUNK_LANES, (c + 1) * S5_CHUNK_LANES)
        pr = _dot3(uc, brh_ref[c], brl_ref[c])
        pi = _dot3(uc, bih_ref[c], bil_ref[c])
        zr = zr_ref[:, lanes]
        zi = zi_ref[:, lanes]
        sr_ref[:, lanes] = zr * pr - zi * pi
        si_ref[:, lanes] = zr * pi + zi * pr
    ar = jnp.broadcast_to(ar_ref[...], (ns, SSM_LANES))
    ai = jnp.broadcast_to(ai_ref[...], (ns, SSM_LANES))

    if ns <= SUBLANES:
        def body(t, carry):
            hr, hi = carry
            rows = pl.ds(t * ns, ns)
            nr = ar * hr - ai * hi + sr_ref[rows, :]
            ni = ar * hi + ai * hr + si_ref[rows, :]
            sr_ref[rows, :] = nr
            si_ref[rows, :] = ni
            return nr, ni

        hr, hi = lax.fori_loop(0, tt, body, (hr_ref[0], hi_ref[0]), unroll=8)
        hr_ref[0] = hr
        hi_ref[0] = hi
    else:
        def body(t, carry):
            rows = pl.ds(pl.multiple_of(t * ns, ns), ns)
            hr = hr_ref[0]
            hi = hi_ref[0]
            nr = ar * hr - ai * hi + sr_ref[rows, :]
            ni = ar * hi + ai * hr + si_ref[rows, :]
            sr_ref[rows, :] = nr
            si_ref[rows, :] = ni
            hr_ref[0] = nr
            hi_ref[0] = ni
            return carry

        lax.fori_loop(0, tt, body, 0)

    ch = []
    for c in range(S5_CHUNKS):
        lanes = slice(c * S5_CHUNK_LANES, (c + 1) * S5_CHUNK_LANES)
        ch.append(_dot1(sr_ref[:, lanes], cr_ref[c]) - _dot1(si_ref[:, lanes], ci_ref[c]))
    y = jnp.concatenate(ch, axis=1) + d_ref[...] * u
    y = _gelu(y)
    gate = _dot1(y, wg_ref[...]) + bg_ref[...]
    y = y * (1.0 / (1.0 + jnp.exp(-gate)))
    y_ref[0] = _rms(y, g_ref[...])


def _s5(u3, h0r, h0i, consts, tt, ns):
    nb, rows, _ = u3.shape
    r = tt * ns
    fixed = lambda b, t: (0, 0)
    tile = lambda b, t: (b, t, 0)
    seq = lambda b, t: (b, 0, 0)
    const_specs = [pl.BlockSpec(c.shape, lambda b, t, nd=c.ndim: (0,) * nd) for c in consts]
    return pl.pallas_call(
        functools.partial(_s5_kernel, tt=tt, ns=ns),
        grid=(nb, rows // r),
        in_specs=[pl.BlockSpec((1, r, SSM_WIDTH), tile),
                  pl.BlockSpec((1, ns, SSM_LANES), seq), pl.BlockSpec((1, ns, SSM_LANES), seq)] + const_specs,
        out_specs=[pl.BlockSpec((1, r, SSM_WIDTH), tile),
                   pl.BlockSpec((1, ns, SSM_LANES), seq), pl.BlockSpec((1, ns, SSM_LANES), seq)],
        out_shape=[jax.ShapeDtypeStruct((nb, rows, SSM_WIDTH), F32),
                   jax.ShapeDtypeStruct((nb, ns, SSM_LANES), F32),
                   jax.ShapeDtypeStruct((nb, ns, SSM_LANES), F32)],
        scratch_shapes=[pltpu.VMEM((r, SSM_LANES), F32), pltpu.VMEM((r, SSM_LANES), F32)],
        compiler_params=_cparams(("arbitrary", "arbitrary")),
        name="s5_ns%d" % ns,
    )(u3, h0r, h0i, *consts)


def _outproj_kernel(a_ref, s_ref, x_ref, wa_ref, ws_ref, g_ref, x1_ref, xn_ref):
    x1 = x_ref[...] + _dot1(a_ref[...], wa_ref[...]) + _dot1(s_ref[...], ws_ref[...])
    x1_ref[...] = x1
    xn_ref[...] = _rms(x1, g_ref[...])


def _out_proj(attn_n, ssm_n, x2d, w_attn, w_ssm, g, tm):
    n = x2d.shape[0]
    row = lambda i: (i, 0)
    fixed = lambda i: (0, 0)
    return pl.pallas_call(
        _outproj_kernel,
        grid=(n // tm,),
        in_specs=[pl.BlockSpec((tm, ATTN_WIDTH), row), pl.BlockSpec((tm, SSM_WIDTH), row),
                  pl.BlockSpec((tm, D_MODEL), row),
                  pl.BlockSpec((ATTN_WIDTH, D_MODEL), fixed), pl.BlockSpec((SSM_WIDTH, D_MODEL), fixed),
                  pl.BlockSpec((1, D_MODEL), fixed)],
        out_specs=[pl.BlockSpec((tm, D_MODEL), row), pl.BlockSpec((tm, D_MODEL), row)],
        out_shape=[jax.ShapeDtypeStruct((n, D_MODEL), F32)] * 2,
        compiler_params=_cparams(("arbitrary",)),
        name="out_proj",
    )(attn_n, ssm_n, x2d, w_attn, w_ssm, g)


TOPK_HEADS_PER_STEP = 8


def _topk_rows(s_ref, ids, k):
    vals, picks = [], []
    sentinel = jnp.iinfo(jnp.int32).max
    for _ in range(k):
        s = s_ref[...]
        m = jnp.max(s, axis=0, keepdims=True)
        pick = jnp.min(jnp.where(s == m, ids, sentinel), axis=0, keepdims=True)
        vals.append(m)
        picks.append(pick)
        s_ref[...] = jnp.where(ids == pick, -jnp.inf, s)
    return vals, picks


def _cand_layout():
    pieces = [(0, 1, 0, PEER_TOPK)]
    pieces += [(a, a + 1, 0, SUBLANES) for a in range(1, SUBLANES)]
    pieces += [(SUBLANES, PEER_TOPK, 0, 1)]
    return pieces


N_CAND_ROWS = sum(max(a1 - a0, b1 - b0) for a0, a1, b0, b1 in _cand_layout())


def _topk_kernel(xn_ref, wqh_ref, wql_ref, keys_ref, row_ref, shift_ref, gate_ref, s_scr, c_scr, *, tm):
    xn = xn_ref[...]
    for hh in range(TOPK_HEADS_PER_STEP):
        h = pl.program_id(1) * TOPK_HEADS_PER_STEP + hh
        q = _dot3(xn, wqh_ref[h], wql_ref[h])
        for c in range(2):
            q_hi, q_lo = _split(q[:, c * PEER_HALF:(c + 1) * PEER_HALF])
            k_hi, k_lo = _split(keys_ref[2 * h + c])
            d = functools.partial(lax.dot_general, dimension_numbers=(((1,), (1,)), ((), ())),
                                  preferred_element_type=F32)
            s_scr[2 * hh + c] = d(k_hi, q_hi) + (d(k_lo, q_hi) + d(k_hi, q_lo))
    for hh in range(TOPK_HEADS_PER_STEP):
        out = slice(hh * PEER_TOPK, (hh + 1) * PEER_TOPK)
        rows, shifts, gates = _select_experts([s_scr.at[2 * hh + c] for c in range(2)], c_scr.at[hh], tm)
        row_ref[out, :] = rows
        shift_ref[out, :] = shifts
        gate_ref[out, :] = gates


def _select_experts(score_refs, cand_ref, tm):
    key_id = lax.broadcasted_iota(I32, (N_KEYS, tm), 0)
    sub_v, sub_i = [], []
    for s_ref in score_refs:
        vals, picks = _topk_rows(s_ref, key_id, PEER_TOPK)
        sub_v.append(vals)
        sub_i.append(picks)
    cs, ce, cf = [], [], []

    def rows_of(lst, lo, hi):
        return lst[lo] if hi - lo == 1 else jnp.concatenate(lst[lo:hi], axis=0)

    for a_lo, a_hi, b_lo, b_hi in _cand_layout():
        na, nbb = a_hi - a_lo, b_hi - b_lo
        rows = max(na, nbb)
        cs.append(rows_of(sub_v[0], a_lo, a_hi) + rows_of(sub_v[1], b_lo, b_hi))
        ce.append(rows_of(sub_i[0], a_lo, a_hi) * N_KEYS + rows_of(sub_i[1], b_lo, b_hi))
        r = lax.broadcasted_iota(I32, (rows, tm), 0)
        cf.append((a_lo + r) * PEER_TOPK + b_lo if na > 1 else a_lo * PEER_TOPK + b_lo + r)
    cand_ref[...] = jnp.concatenate(cs, axis=0)
    cand_e = jnp.concatenate(ce, axis=0)
    cand_f = jnp.concatenate(cf, axis=0)
    best_v, best_e = [], []
    sentinel = jnp.iinfo(jnp.int32).max
    for _ in range(PEER_TOPK):
        cand_s = cand_ref[...]
        m = jnp.max(cand_s, axis=0, keepdims=True)
        f = jnp.min(jnp.where(cand_s == m, cand_f, sentinel), axis=0, keepdims=True)
        hit = cand_f == f
        best_v.append(m)
        best_e.append(jnp.max(jnp.where(hit, cand_e, -1), axis=0, keepdims=True))
        cand_ref[...] = jnp.where(hit, -jnp.inf, cand_s)
    bv = jnp.concatenate(best_v, axis=0)
    be = jnp.concatenate(best_e, axis=0)
    ex = jnp.exp(bv - bv[0:1])
    gates = ex / jnp.sum(ex, axis=0, keepdims=True)
    rows = lax.shift_right_logical(be, 1) * SUBLANES
    shifts = ((1 - (be & 1)) * 16).astype(F32)
    return rows, shifts, gates


def _peer_topk(xn2d, wq_hi, wq_lo, keys, tm):
    n = xn2d.shape[0]
    out = lambda i, h: (h, i)
    return pl.pallas_call(
        functools.partial(_topk_kernel, tm=tm),
        grid=(n // tm, PEER_HEADS // TOPK_HEADS_PER_STEP),
        in_specs=[pl.BlockSpec((tm, D_MODEL), lambda i, h: (i, 0)),
                  pl.BlockSpec(wq_hi.shape, lambda i, h: (0, 0, 0)),
                  pl.BlockSpec(wq_lo.shape, lambda i, h: (0, 0, 0)),
                  pl.BlockSpec(keys.shape, lambda i, h: (0, 0, 0))],
        out_specs=[pl.BlockSpec((TOPK_HEADS_PER_STEP * PEER_TOPK, tm), out)] * 3,
        out_shape=[jax.ShapeDtypeStruct((N_PAIRS, n), I32), jax.ShapeDtypeStruct((N_PAIRS, n), F32),
                   jax.ShapeDtypeStruct((N_PAIRS, n), F32)],
        scratch_shapes=[pltpu.VMEM((2 * TOPK_HEADS_PER_STEP, N_KEYS, tm), F32),
                        pltpu.VMEM((TOPK_HEADS_PER_STEP, N_CAND_ROWS, tm), F32)],
        compiler_params=_cparams(("arbitrary", "arbitrary")),
        name="peer_topk",
    )(xn2d, wq_hi, wq_lo, keys)


def _pack_kernel(t_ref, o_ref):
    even = t_ref[:, 0].astype(BF16).astype(F32)
    odd = t_ref[:, 1].astype(BF16).astype(F32)
    hi = pltpu.bitcast(odd, I32) & HI_MASK
    lo = lax.shift_right_logical(pltpu.bitcast(even, I32), 16)
    o_ref[...] = hi | lo


def _pack_table(table, tb=256):
    e = table.shape[0]
    t4 = table.reshape(e // 2, 2, SUBLANES, LANES)
    packed = pl.pallas_call(
        _pack_kernel,
        grid=(e // 2 // tb,),
        in_specs=[pl.BlockSpec((tb, 2, SUBLANES, LANES), lambda i: (i, 0, 0, 0))],
        out_specs=pl.BlockSpec((tb, SUBLANES, LANES), lambda i: (i, 0, 0)),
        out_shape=jax.ShapeDtypeStruct((e // 2, SUBLANES, LANES), I32),
        compiler_params=_cparams(("arbitrary",)),
        name="pack_table",
    )(t4)
    return packed.reshape(e // 2 * SUBLANES, LANES)


def _tile(tbl, offset):
    return tbl[pl.ds(pl.multiple_of(offset, SUBLANES), SUBLANES), :]


def _expert_row(word, shift):
    return pltpu.bitcast((word << shift) & HI_MASK, F32)


def _row8(ref, k):
    return jnp.broadcast_to(ref[k:k + 1, :], (SUBLANES, LANES))


def _fold(p, steps):
    sub = lax.broadcasted_iota(I32, p[0].shape, 0)
    for step in steps:
        first = (sub % (2 * step)) < step
        n = len(p) // 2
        p = [jnp.where(first, p[j] + pltpu.roll(p[j], SUBLANES - step, 0),
                       p[j + n] + pltpu.roll(p[j + n], step, 0)) for j in range(n)]
    return p


def _column(block_ref, tile, tm, t):
    blk = block_ref[:, tile * tm:(tile + 1) * tm]
    lane = lax.broadcasted_iota(I32, blk.shape, 1)
    col = jnp.sum(jnp.where(lane == t, blk, 0.0), axis=1, keepdims=True)
    return jnp.broadcast_to(col, (N_PAIRS, LANES))


def _rows_copy(row_hbm, buf, sem, tile, tm):
    return pltpu.make_async_copy(row_hbm.at[:, pl.ds(tile * tm, tm)], buf, sem)


def _for_each_tile(row_hbm, bufs, sems, tm, body):
    i = pl.program_id(0)
    tps = len(bufs)
    first = i * tps
    total = pl.num_programs(0) * tps

    def copy(j, tile):
        return _rows_copy(row_hbm, bufs[j], sems.at[j], tile, tm)

    @pl.when(i == 0)
    def _():
        copy(0, 0).start()

    for j in range(tps):
        nxt = (j + 1) % tps
        if tps == 1:
            copy(0, first).wait()
            body(0, bufs[0])

            @pl.when(first + 1 < total)
            def _():
                copy(0, first + 1).start()
        else:
            @pl.when(first + j + 1 < total)
            def _():
                copy(nxt, first + j + 1).start()

            copy(j, first + j).wait()
            body(j, bufs[j])


def _load_table(tbl_hbm, tbl_vmem, sem):
    @pl.when(pl.program_id(0) == 0)
    def _():
        c = pltpu.make_async_copy(tbl_hbm, tbl_vmem, sem)
        c.start()
        c.wait()


ACT_RING = 8
EXPAND_RING = 8


def _ring_loop(tm, ring, expand, gather):
    n = len(ring)
    ahead = n // 2
    for r in range(ahead):
        expand(r, *ring[r])

    def ring_pass(j, carry):
        t0 = n * j
        for r in range(n):
            gather(t0 + r, *ring[r])
            expand(jnp.minimum(t0 + r + ahead, tm - 1), *ring[(r + ahead) % n])
        return carry

    lax.fori_loop(0, tm // n, ring_pass, 0)


def _tag_parity(w, is_odd):
    bits = (pltpu.bitcast(w, I32) & -2) | jnp.where(is_odd, 1, 0)
    return pltpu.bitcast(bits, F32)


def _tiles_per_step(n, tm):
    return 2 if (n // tm) % 2 == 0 else 1


def _peer_act_kernel(row_hbm, x_ref, shift_ref, gate_ref, tbl_hbm, w_ref, tbl, *scratch, tm, tps):
    bufs = scratch[:tps]
    sh_ring = scratch[tps:tps + ACT_RING]
    acc, sems = scratch[tps + ACT_RING:]
    _load_table(tbl_hbm, tbl, sems.at[tps])
    lane = lax.broadcasted_iota(I32, (SUBLANES, tm), 1)
    n_groups = N_PAIRS // SUBLANES

    def tile_body(tile, rows):
        def expand(t, sh):
            sh[...] = _column(shift_ref, tile, tm, t).astype(I32)

        def gather(t, sh):
            xt = x_ref[tile * tm + t]
            here = lane == t
            for g in range(n_groups):
                prods = []
                for j in range(SUBLANES):
                    k = g * SUBLANES + j
                    prods.append(_expert_row(_tile(tbl, rows.at[k][t]), _row8(sh, k)) * xt)
                folded = _fold(prods, (4, 2, 1))[0]
                acc[g] = jnp.where(here, jnp.sum(folded, axis=1, keepdims=True), acc[g])

        acc[...] = jnp.zeros(acc.shape, F32)
        _ring_loop(tm, [(sh,) for sh in sh_ring], expand, gather)
        act = jnp.concatenate([acc[g] for g in range(n_groups)], axis=0)
        cols = slice(tile * tm, (tile + 1) * tm)
        w_ref[:, cols] = _tag_parity(gate_ref[:, cols] * _gelu(act), shift_ref[:, cols] == 0.0)

    _for_each_tile(row_hbm, bufs, sems, tm, tile_body)


def _peer_act(rows, xn3, shift, gate, tbl, tm):
    n = xn3.shape[0]
    tps = _tiles_per_step(n, tm)
    step = tps * tm
    col = pl.BlockSpec((N_PAIRS, step), lambda i: (0, i))
    return pl.pallas_call(
        functools.partial(_peer_act_kernel, tm=tm, tps=tps),
        grid=(n // step,),
        in_specs=[pl.BlockSpec(memory_space=pl.ANY),
                  pl.BlockSpec((step, SUBLANES, LANES), lambda i: (i, 0, 0)),
                  col, col, pl.BlockSpec(memory_space=pl.ANY)],
        out_specs=col,
        out_shape=jax.ShapeDtypeStruct((N_PAIRS, n), F32),
        scratch_shapes=([pltpu.VMEM(tbl.shape, I32)] + [pltpu.SMEM((N_PAIRS, tm), I32)] * tps
                        + [pltpu.VMEM((N_PAIRS, LANES), I32)] * ACT_RING
                        + [pltpu.VMEM((N_PAIRS // SUBLANES, SUBLANES, tm), F32),
                           pltpu.SemaphoreType.DMA((tps + 1,))]),
        compiler_params=_cparams(("arbitrary",)),
        name="peer_act",
    )(rows, xn3, shift, gate, tbl)


def _peer_out_kernel(row_hbm, w_ref, x1_ref, tbl_hbm, o_ref, tbl, *scratch, tm, tps):
    bufs = scratch[:tps]
    sh_ring = scratch[tps:tps + EXPAND_RING]
    wb_ring = scratch[tps + EXPAND_RING:tps + 2 * EXPAND_RING]
    sems = scratch[tps + 2 * EXPAND_RING]
    _load_table(tbl_hbm, tbl, sems.at[tps])
    n_acc = 4

    def tile_body(tile, rows):
        def expand(t, sh, wb):
            col = _column(w_ref, tile, tm, t)
            wb[...] = col
            sh[...] = ((pltpu.bitcast(col, I32) & 1) ^ 1) << 4

        def gather(t, sh, wb):
            accs = [x1_ref[tile * tm + t]] + [jnp.zeros((SUBLANES, LANES), F32)] * (n_acc - 1)
            for k in range(N_PAIRS):
                v_row = _expert_row(_tile(tbl, rows.at[k][t]), _row8(sh, k))
                accs[k % n_acc] = accs[k % n_acc] + _row8(wb, k) * v_row
            o_ref[tile * tm + t] = (accs[0] + accs[1]) + (accs[2] + accs[3])

        _ring_loop(tm, list(zip(sh_ring, wb_ring)), expand, gather)

    _for_each_tile(row_hbm, bufs, sems, tm, tile_body)


def _peer_out(rows, w, x1_3, tbl, tm):
    n = x1_3.shape[0]
    tps = _tiles_per_step(n, tm)
    step = tps * tm
    blk = pl.BlockSpec((step, SUBLANES, LANES), lambda i: (i, 0, 0))
    col = pl.BlockSpec((N_PAIRS, step), lambda i: (0, i))
    return pl.pallas_call(
        functools.partial(_peer_out_kernel, tm=tm, tps=tps),
        grid=(n // step,),
        in_specs=[pl.BlockSpec(memory_space=pl.ANY), col, blk, pl.BlockSpec(memory_space=pl.ANY)],
        out_specs=blk,
        out_shape=jax.ShapeDtypeStruct(x1_3.shape, F32),
        scratch_shapes=([pltpu.VMEM(tbl.shape, I32)] + [pltpu.SMEM((N_PAIRS, tm), I32)] * tps
                        + [pltpu.VMEM((N_PAIRS, LANES), I32)] * EXPAND_RING
                        + [pltpu.VMEM((N_PAIRS, LANES), F32)] * EXPAND_RING
                        + [pltpu.SemaphoreType.DMA((tps + 1,))]),
        compiler_params=_cparams(("arbitrary",)),
        name="peer_out",
    )(rows, w, x1_3, tbl)


def _final_kernel(x_ref, g_ref, o_ref):
    o_ref[...] = _rms(x_ref[...], g_ref[...])


def _final_norm(x2d, g, tm):
    n = x2d.shape[0]
    return pl.pallas_call(
        _final_kernel,
        grid=(n // tm,),
        in_specs=[pl.BlockSpec((tm, D_MODEL), lambda i: (i, 0)), pl.BlockSpec((1, D_MODEL), lambda i: (0, 0))],
        out_specs=pl.BlockSpec((tm, D_MODEL), lambda i: (i, 0)),
        out_shape=jax.ShapeDtypeStruct(x2d.shape, F32),
        compiler_params=_cparams(("arbitrary",)),
        name="final_norm",
    )(x2d, g)


def _block_diag_in(b):
    eye = jnp.eye(S5_CHUNK_GROUPS, dtype=F32)
    b4 = b.reshape(S5_CHUNKS, S5_CHUNK_GROUPS, SSM_STATE, SSM_GROUP)
    return jnp.einsum('cgnp,gh->cgphn', b4, eye).reshape(S5_CHUNKS, LANES, S5_CHUNK_LANES)


def _block_diag_out(c):
    eye = jnp.eye(S5_CHUNK_GROUPS, dtype=F32)
    c4 = c.reshape(S5_CHUNKS, S5_CHUNK_GROUPS, SSM_GROUP, SSM_STATE)
    return jnp.einsum('cgpn,gh->cgnhp', c4, eye).reshape(S5_CHUNKS, S5_CHUNK_LANES, LANES)


def _layer_weights(norm_mix, w_in, b_in, attn_sinks, lam_re, lam_im, log_step, b_re, b_im, c_re, c_im,
                   d_skip, w_glu, b_glu, norm_attn_out, norm_ssm_out, w_out, norm_ffn, w_query, sub_keys,
                   u_table, v_table):
    w = {}
    w['norm_mix'] = norm_mix.reshape(1, D_MODEL)
    w['w_in'] = _split(w_in)
    w['b_in'] = b_in.reshape(1, IN_WIDTH)
    w['sinks'] = attn_sinks
    ab_re, ab_im, z_re, z_im = _s5_params(lam_re, lam_im, log_step)
    w['s5'] = (list(_split(_block_diag_in(b_re))) + list(_split(_block_diag_in(b_im)))
               + [z_re, z_im, ab_re, ab_im,
                  _block_diag_out(c_re).astype(BF16), _block_diag_out(c_im).astype(BF16),
                  d_skip.reshape(1, SSM_WIDTH), w_glu.astype(BF16), b_glu.reshape(1, SSM_WIDTH),
                  norm_ssm_out.reshape(1, SSM_WIDTH)])
    w['norm_attn_out'] = norm_attn_out.reshape(1, ATTN_WIDTH)
    w['w_out_attn'] = w_out[:ATTN_WIDTH].astype(BF16)
    w['w_out_ssm'] = w_out[ATTN_WIDTH:].astype(BF16)
    w['norm_ffn'] = norm_ffn.reshape(1, D_MODEL)
    wq = w_query.reshape(D_MODEL, PEER_HEADS, PEER_QDIM).transpose(1, 0, 2)
    w['w_query'] = _split(wq)
    w['keys'] = sub_keys.reshape(PEER_HEADS * 2, N_KEYS, PEER_HALF)
    w['u_tbl'] = _pack_table(u_table)
    w['v_tbl'] = _pack_table(v_table)
    return w


def _peer(xn2d, x1_2d, w, tm_topk, tm_gather):
    n = xn2d.shape[0]
    rows, shift, gate = _peer_topk(xn2d, w['w_query'][0], w['w_query'][1], w['keys'], tm_topk)
    wts = _peer_act(rows, xn2d.reshape(n, SUBLANES, LANES), shift, gate, w['u_tbl'], tm_gather)
    out = _peer_out(rows, wts, x1_2d.reshape(n, SUBLANES, LANES), w['v_tbl'], tm_gather)
    return out.reshape(n, D_MODEL)


def _token_tile(n, cap):
    t = cap
    while n % t:
        t //= 2
    return t


def _mix_and_ffn(x2d, attn_n, ssm_n, w, norm_final):
    n = x2d.shape[0]
    x1, xn2 = _out_proj(attn_n, ssm_n, x2d, w['w_out_attn'], w['w_out_ssm'], w['norm_ffn'], _token_tile(n, 512))
    x2 = _peer(xn2, x1, w, _token_tile(n, 256), _token_tile(n, 128))
    return x2


def _prompt_layer(x, w):
    b, t, _ = x.shape
    n = b * t
    x2d = x.reshape(n, D_MODEL)
    tm = _token_tile(t, 512)
    tables = _rope_tables(jnp.arange(t, dtype=I32))
    q, k, v, u = _in_proj(x2d, tables, t // tm, w['norm_mix'], w['w_in'][0], w['w_in'][1], w['b_in'], tm)
    k3 = k.reshape(b, t, KV_WIDTH)
    v3 = v.reshape(b, t, KV_WIDTH)
    (attn_n,) = _attention(w['sinks'], q.reshape(b, t, ATTN_WIDTH), k3, k3, v3, v3, w['norm_attn_out'], False)
    h0 = jnp.zeros((b, 1, SSM_LANES), F32)
    ssm_n, h_re, h_im = _s5(u.reshape(b, t, SSM_WIDTH), h0, h0, w['s5'], _token_tile(t, 256), 1)
    x2 = _mix_and_ffn(x2d, attn_n.reshape(n, ATTN_WIDTH), ssm_n.reshape(n, SSM_WIDTH), w, None)
    k_win = k3[:, -WINDOW:].reshape(b, WINDOW, N_KV_HEADS, HEAD_DIM)
    v_win = v3[:, -WINDOW:].reshape(b, WINDOW, N_KV_HEADS, HEAD_DIM)
    st = lambda h: h.reshape(b, N_SSM_GROUPS, SSM_STATE)
    return x2.reshape(b, t, D_MODEL), k_win, v_win, st(h_re), st(h_im)


def _sample_layer(x, k_buf, v_buf, h0_re, h0_im, w, past_len):
    b, t, _ = x.shape
    n = b * t
    x2d = x.reshape(n, D_MODEL)
    tm = _token_tile(n, 512)
    pos = past_len + jnp.arange(t, dtype=I32)
    tables = tuple(jnp.tile(a, (tm // t, 1)) for a in _rope_tables(pos))
    q, k, v, u = _in_proj(x2d, tables, 1, w['norm_mix'], w['w_in'][0], w['w_in'][1], w['b_in'], tm)
    pad = lambda a: jnp.pad(a.reshape(b, t, KV_WIDTH), ((0, 0), (0, WINDOW - t), (0, 0)))
    attn_n, k_win, v_win = _attention(w['sinks'], q.reshape(b, t, ATTN_WIDTH),
                                      k_buf.reshape(b, WINDOW, KV_WIDTH), pad(k),
                                      v_buf.reshape(b, WINDOW, KV_WIDTH), pad(v), w['norm_attn_out'], True)
    ns = min(b, S5_DECODE_SEQS)
    nbk = b // ns
    u_tm = u.reshape(nbk, ns, t, SSM_WIDTH).transpose(0, 2, 1, 3).reshape(nbk, t * ns, SSM_WIDTH)
    ssm_tm, h_re, h_im = _s5(u_tm, h0_re.reshape(nbk, ns, SSM_LANES), h0_im.reshape(nbk, ns, SSM_LANES),
                             w['s5'], t, ns)
    ssm_n = ssm_tm.reshape(nbk, t, ns, SSM_WIDTH).transpose(0, 2, 1, 3).reshape(n, SSM_WIDTH)
    x2 = _mix_and_ffn(x2d, attn_n.reshape(n, ATTN_WIDTH), ssm_n, w, None)
    win = lambda a: a.reshape(b, WINDOW, N_KV_HEADS, HEAD_DIM)
    st = lambda h: h.reshape(b, N_SSM_GROUPS, SSM_STATE)
    return x2.reshape(b, t, D_MODEL), win(k_win), win(v_win), st(h_re), st(h_im)


PAST_LEN = 16384
S5_DECODE_SEQS = 64
DECODE_SEQS_PER_STEP = 8


def kernel(x_prompt, x_sample, cache_k, cache_v, state_ssm_re, state_ssm_im, norm_mix, w_in, b_in, attn_sinks, ssm_lam_re, ssm_lam_im, ssm_log_step, ssm_b_re, ssm_b_im, ssm_c_re, ssm_c_im, ssm_d, ssm_w_glu, ssm_b_glu, norm_attn_out, norm_ssm_out, w_out, norm_ffn, peer_w_query, peer_sub_keys, peer_u, peer_v, norm_final):
    depth = norm_mix.shape[0]
    xp, xs = x_prompt, x_sample
    outs = [[] for _ in range(8)]
    for l in range(depth):
        w = _layer_weights(*[a[l] for a in (norm_mix, w_in, b_in, attn_sinks, ssm_lam_re, ssm_lam_im,
                                            ssm_log_step, ssm_b_re, ssm_b_im, ssm_c_re, ssm_c_im, ssm_d,
                                            ssm_w_glu, ssm_b_glu, norm_attn_out, norm_ssm_out, w_out,
                                            norm_ffn, peer_w_query, peer_sub_keys, peer_u, peer_v)])
        xs, k2, v2, r2, i2 = _sample_layer(xs, cache_k[l], cache_v[l], state_ssm_re[l], state_ssm_im[l],
                                           w, PAST_LEN)
        xp, k1, v1, r1, i1 = _prompt_layer(xp, w)
        for lst, a in zip(outs, (k1, v1, r1, i1, k2, v2, r2, i2)):
            lst.append(a)
    g = norm_final.reshape(1, D_MODEL)
    yp = _final_norm(xp.reshape(-1, D_MODEL), g, 512).reshape(xp.shape)
    ys = _final_norm(xs.reshape(-1, D_MODEL), g, 512).reshape(xs.shape)
    return (yp, ys) + tuple(jnp.stack(o) for o in outs)
```

```python
import functools
import math

import jax
import jax.numpy as jnp
from jax import lax
from jax.experimental import pallas as pl
from jax.experimental.pallas import tpu as pltpu

F32 = jnp.float32
BF16 = jnp.bfloat16
I32 = jnp.int32

D_MODEL = 1024
N_HEADS = 8
N_KV_HEADS = 2
HEAD_DIM = 64
ATTN_WIDTH = N_HEADS * HEAD_DIM
KV_WIDTH = N_KV_HEADS * HEAD_DIM
WINDOW = 128
ROT_DIM = HEAD_DIM // 4
ROPE_THETA = 500000.0
NEG_INF = -1e30
SSM_WIDTH = D_MODEL - ATTN_WIDTH
SSM_GROUP = 16
N_SSM_GROUPS = SSM_WIDTH // SSM_GROUP
SSM_STATE = 64
SSM_LANES = N_SSM_GROUPS * SSM_STATE
IN_WIDTH = ATTN_WIDTH + 2 * KV_WIDTH + SSM_WIDTH
PEER_HEADS = 8
N_KEYS = 128
PEER_TOPK = 16
PEER_QDIM = 256
PEER_HALF = PEER_QDIM // 2
N_PAIRS = PEER_HEADS * PEER_TOPK
RMS_EPS = 1e-5

LANES = 128
SUBLANES = 8
VMEM_LIMIT = 56 * 1024 * 1024

HI_MASK = -65536

S5_CHUNK_GROUPS = LANES // SSM_GROUP
S5_CHUNKS = N_SSM_GROUPS // S5_CHUNK_GROUPS
S5_CHUNK_LANES = S5_CHUNK_GROUPS * SSM_STATE


def _cparams(sem):
    return pltpu.CompilerParams(dimension_semantics=sem, vmem_limit_bytes=VMEM_LIMIT)


def _split(w):
    hi = w.astype(BF16)
    lo = (w - hi.astype(F32)).astype(BF16)
    return hi, lo


def _dot1(a, b):
    return jnp.dot(a.astype(BF16), b, preferred_element_type=F32)


def _dot3(a, b_hi, b_lo):
    a_hi, a_lo = _split(a)
    d = functools.partial(jnp.dot, preferred_element_type=F32)
    return d(a_hi, b_hi) + (d(a_lo, b_hi) + d(a_hi, b_lo))


def _dot3_nt(a_hi, a_lo, b):
    b_hi, b_lo = _split(b)
    d = functools.partial(lax.dot_general, dimension_numbers=(((1,), (1,)), ((), ())),
                          preferred_element_type=F32)
    return d(a_hi, b_hi) + (d(a_lo, b_hi) + d(a_hi, b_lo))


def _gelu(x):
    return 0.5 * x * (1.0 + lax.erf(x * (2.0 ** -0.5)))


def _rms(x, g):
    return x * lax.rsqrt(jnp.mean(x * x, axis=-1, keepdims=True) + RMS_EPS) * g


def _inproj_kernel(x_ref, g_ref, whi_ref, wlo_ref, b_ref, c_ref, sa_ref, sb_ref,
                   q_ref, k_ref, v_ref, u_ref):
    xn = _rms(x_ref[...], g_ref[...])
    proj = _dot3(xn, whi_ref[...], wlo_ref[...]) + b_ref[...]
    c = c_ref[...]
    sa = sa_ref[...]
    sb = sb_ref[...]

    def rope(t):
        return t * c + pltpu.roll(t, LANES - ROT_DIM // 2, 1) * sa + pltpu.roll(t, ROT_DIM // 2, 1) * sb

    for j in range(ATTN_WIDTH // LANES):
        q_ref[:, j * LANES:(j + 1) * LANES] = rope(proj[:, j * LANES:(j + 1) * LANES])
    k_ref[...] = rope(proj[:, ATTN_WIDTH:ATTN_WIDTH + KV_WIDTH])
    v_ref[...] = proj[:, ATTN_WIDTH + KV_WIDTH:ATTN_WIDTH + 2 * KV_WIDTH]
    u_ref[...] = proj[:, ATTN_WIDTH + 2 * KV_WIDTH:]


def _rope_tables(pos):
    half = ROT_DIM // 2
    inv_freq = ROPE_THETA ** (-jnp.arange(half, dtype=F32) * 2.0 / ROT_DIM)
    ang = pos.astype(F32)[:, None] * inv_freq[None, :]
    cos = jnp.cos(ang)
    sin = jnp.sin(ang)
    t = pos.shape[0]
    one = jnp.ones((t, HEAD_DIM - ROT_DIM), F32)
    zero = jnp.zeros((t, HEAD_DIM - ROT_DIM), F32)
    zh = jnp.zeros((t, half), F32)
    c = jnp.concatenate([cos, cos, one], axis=1)
    sa = jnp.concatenate([-sin, zh, zero], axis=1)
    sb = jnp.concatenate([zh, sin, zero], axis=1)
    return tuple(jnp.tile(a, (1, LANES // HEAD_DIM)) for a in (c, sa, sb))


def _in_proj(x2d, pos_tables, n_pos_tiles, g, w_hi, w_lo, b, tm):
    n = x2d.shape[0]
    c, sa, sb = pos_tables
    row = lambda i: (i, 0)
    fixed = lambda i: (0, 0)
    pos_map = lambda i: (i % n_pos_tiles, 0)
    return pl.pallas_call(
        _inproj_kernel,
        grid=(n // tm,),
        in_specs=[pl.BlockSpec((tm, D_MODEL), row), pl.BlockSpec((1, D_MODEL), fixed),
                  pl.BlockSpec((D_MODEL, IN_WIDTH), fixed), pl.BlockSpec((D_MODEL, IN_WIDTH), fixed),
                  pl.BlockSpec((1, IN_WIDTH), fixed),
                  pl.BlockSpec((tm, LANES), pos_map), pl.BlockSpec((tm, LANES), pos_map),
                  pl.BlockSpec((tm, LANES), pos_map)],
        out_specs=[pl.BlockSpec((tm, ATTN_WIDTH), row), pl.BlockSpec((tm, KV_WIDTH), row),
                   pl.BlockSpec((tm, KV_WIDTH), row), pl.BlockSpec((tm, SSM_WIDTH), row)],
        out_shape=[jax.ShapeDtypeStruct((n, ATTN_WIDTH), F32), jax.ShapeDtypeStruct((n, KV_WIDTH), F32),
                   jax.ShapeDtypeStruct((n, KV_WIDTH), F32), jax.ShapeDtypeStruct((n, SSM_WIDTH), F32)],
        compiler_params=_cparams(("arbitrary",)),
        name="in_proj",
    )(x2d, g, w_hi, w_lo, b, c, sa, sb)


def _attn_kernel(sink_ref, q_ref, kp_ref, kc_ref, vp_ref, vc_ref, g_ref, o_ref, *win_refs, tq, decode, nseq):
    for b in range(nseq):
        _attn_one(sink_ref, q_ref.at[b], kp_ref.at[b], kc_ref.at[b], vp_ref.at[b], vc_ref.at[b], g_ref,
                  o_ref.at[b], *[w.at[b] for w in win_refs], tq=tq, decode=decode)


def _attn_one(sink_ref, q_ref, kp_ref, kc_ref, vp_ref, vc_ref, g_ref, o_ref, *win_refs, tq, decode):
    q = q_ref[...] * (HEAD_DIM ** -0.5)
    kp = kp_ref[...]
    kc = kc_ref[...]
    vp = vp_ref[...]
    vc = vc_ref[...]
    half = LANES // 2
    lane = lax.broadcasted_iota(I32, (1, LANES), 1)
    lo = lane < half
    qi = lax.broadcasted_iota(I32, (tq, WINDOW), 0)
    kj = lax.broadcasted_iota(I32, (tq, WINDOW), 1)
    first_off = 0 if decode else jnp.where(pl.program_id(1) > 0, 0, WINDOW)
    m_prev = kj > qi + first_off
    m_cur = kj <= qi

    def variants(t):
        r = pltpu.roll(t, half, 1)
        return [[t, r], [r, t]]

    kpv = [[a.astype(BF16) for a in row] for row in variants(kp)]
    kcv = [[a.astype(BF16) for a in row] for row in variants(kc)]
    lane_sel = [lo, jnp.logical_not(lo)]
    vpv = [[jnp.where(lane_sel[s], a, 0.0).astype(BF16) for s, a in enumerate(row)] for row in variants(vp)]
    vcv = [[jnp.where(lane_sel[s], a, 0.0).astype(BF16) for s, a in enumerate(row)] for row in variants(vc)]
    nt = functools.partial(lax.dot_general, dimension_numbers=(((1,), (1,)), ((), ())),
                           preferred_element_type=F32)
    outs = []
    for j in range(ATTN_WIDTH // LANES):
        g = (2 * j) // (N_HEADS // N_KV_HEADS)
        qt = q[:, j * LANES:(j + 1) * LANES]
        acc = jnp.zeros((tq, LANES), F32)
        for s in range(2):
            h = 2 * j + s
            qm = jnp.where(lane_sel[s], qt, 0.0).astype(BF16)
            sp = jnp.where(m_prev, nt(qm, kpv[g][s]), NEG_INF)
            sc = jnp.where(m_cur, nt(qm, kcv[g][s]), NEG_INF)
            sink = sink_ref[h]
            m = jnp.maximum(jnp.maximum(jnp.max(sp, axis=-1, keepdims=True),
                                        jnp.max(sc, axis=-1, keepdims=True)), sink)
            pp = jnp.exp(sp - m)
            pc = jnp.exp(sc - m)
            den = (jnp.sum(pp, axis=-1, keepdims=True) + jnp.sum(pc, axis=-1, keepdims=True)
                   + jnp.exp(sink - m))
            o = (jnp.dot(pp.astype(BF16), vpv[g][s], preferred_element_type=F32)
                 + jnp.dot(pc.astype(BF16), vcv[g][s], preferred_element_type=F32))
            acc = acc + o / den
        outs.append(acc)
    attn = jnp.concatenate(outs, axis=1)
    o_ref[...] = _rms(attn, g_ref[...])
    if decode:
        kw_ref, vw_ref = win_refs
        kw_ref[:WINDOW - tq] = kp[tq:]
        kw_ref[WINDOW - tq:] = kc[:tq]
        vw_ref[:WINDOW - tq] = vp[tq:]
        vw_ref[WINDOW - tq:] = vc[:tq]


def _attention(sinks, q3, k_prev, k_cur, v_prev, v_cur, g, decode):
    b, t, _ = q3.shape
    tq = t if decode else WINDOW
    nb = t // tq
    nseq = math.gcd(b, DECODE_SEQS_PER_STEP) if decode else 1
    cur = lambda i, n: (i, n, 0)
    prev = (lambda i, n: (i, 0, 0)) if decode else (lambda i, n: (i, jnp.maximum(n - 1, 0), 0))
    kvb = (nseq, WINDOW, KV_WIDTH)
    out_specs = [pl.BlockSpec((nseq, tq, ATTN_WIDTH), cur)]
    out_shape = [jax.ShapeDtypeStruct((b, t, ATTN_WIDTH), F32)]
    if decode:
        out_specs += [pl.BlockSpec(kvb, cur), pl.BlockSpec(kvb, cur)]
        out_shape += [jax.ShapeDtypeStruct((b, WINDOW, KV_WIDTH), F32)] * 2
    return pl.pallas_call(
        functools.partial(_attn_kernel, tq=tq, decode=decode, nseq=nseq),
        grid=(b // nseq, nb),
        in_specs=[pl.BlockSpec(memory_space=pltpu.SMEM),
                  pl.BlockSpec((nseq, tq, ATTN_WIDTH), cur),
                  pl.BlockSpec(kvb, prev), pl.BlockSpec(kvb, cur),
                  pl.BlockSpec(kvb, prev), pl.BlockSpec(kvb, cur),
                  pl.BlockSpec((1, ATTN_WIDTH), lambda i, n: (0, 0))],
        out_specs=out_specs,
        out_shape=out_shape,
        compiler_params=_cparams(("arbitrary", "arbitrary")),
        name="attn_decode" if decode else "attn_prompt",
    )(sinks, q3, k_prev, k_cur, v_prev, v_cur, g)


def _s5_param_kernel(lr_ref, li_ref, ls_ref, abr_ref, abi_ref, zr_ref, zi_ref):
    lr = lr_ref[...]
    li = li_ref[...]
    dt = jnp.exp(ls_ref[...])
    mag = jnp.exp(lr * dt)
    ab_re = mag * jnp.cos(li * dt)
    ab_im = mag * jnp.sin(li * dt)
    den = lr * lr + li * li
    abr_ref[...] = ab_re
    abi_ref[...] = ab_im
    zr_ref[...] = ((ab_re - 1.0) * lr + ab_im * li) / den
    zi_ref[...] = (ab_im * lr - (ab_re - 1.0) * li) / den


def _s5_params(lam_re, lam_im, log_step):
    ls = jnp.broadcast_to(log_step[:, None], lam_re.shape)
    outs = pl.pallas_call(
        _s5_param_kernel,
        out_shape=[jax.ShapeDtypeStruct(lam_re.shape, F32)] * 4,
        name="s5_params",
    )(lam_re, lam_im, ls)
    return [o.reshape(1, SSM_LANES) for o in outs]


def _s5_kernel(u_ref, h0r_ref, h0i_ref, brh_ref, brl_ref, bih_ref, bil_ref, zr_ref, zi_ref,
               ar_ref, ai_ref, cr_ref, ci_ref, d_ref, wg_ref, bg_ref, g_ref,
               y_ref, hr_ref, hi_ref, sr_ref, si_ref, *, tt, ns):
    @pl.when(pl.program_id(1) == 0)
    def _():
        hr_ref[0] = h0r_ref[0]
        hi_ref[0] = h0i_ref[0]

    u = u_ref[0]
    for c in range(S5_CHUNKS):
        uc = u[:, c * LANES:(c + 1) * LANES]
        lanes = slice(c * S5_CHUNK_LANES, (c + 1) * S5_CHUNK_LANES)
        pr = _dot3(uc, brh_ref[c], brl_ref[c])
        pi = _dot3(uc, bih_ref[c], bil_ref[c])
        zr = zr_ref[:, lanes]
        zi = zi_ref[:, lanes]
        sr_ref[:, lanes] = zr * pr - zi * pi
        si_ref[:, lanes] = zr * pi + zi * pr
    ar = jnp.broadcast_to(ar_ref[...], (ns, SSM_LANES))
    ai = jnp.broadcast_to(ai_ref[...], (ns, SSM_LANES))

    if ns <= SUBLANES:
        def body(t, carry):
            hr, hi = carry
            rows = pl.ds(t * ns, ns)
            nr = ar * hr - ai * hi + sr_ref[rows, :]
            ni = ar * hi + ai * hr + si_ref[rows, :]
            sr_ref[rows, :] = nr
            si_ref[rows, :] = ni
            return nr, ni

        hr, hi = lax.fori_loop(0, tt, body, (hr_ref[0], hi_ref[0]), unroll=8)
        hr_ref[0] = hr
        hi_ref[0] = hi
    else:
        def body(t, carry):
            rows = pl.ds(pl.multiple_of(t * ns, ns), ns)
            hr = hr_ref[0]
            hi = hi_ref[0]
            nr = ar * hr - ai * hi + sr_ref[rows, :]
            ni = ar * hi + ai * hr + si_ref[rows, :]
            sr_ref[rows, :] = nr
            si_ref[rows, :] = ni
            hr_ref[0] = nr
            hi_ref[0] = ni
            return carry

        lax.fori_loop(0, tt, body, 0)

    ch = []
    for c in range(S5_CHUNKS):
        lanes = slice(c * S5_CHUNK_LANES, (c + 1) * S5_CHUNK_LANES)
        ch.append(_dot1(sr_ref[:, lanes], cr_ref[c]) - _dot1(si_ref[:, lanes], ci_ref[c]))
    y = jnp.concatenate(ch, axis=1) + d_ref[...] * u
    y = _gelu(y)
    gate = _dot1(y, wg_ref[...]) + bg_ref[...]
    y = y * (1.0 / (1.0 + jnp.exp(-gate)))
    y_ref[0] = _rms(y, g_ref[...])


def _s5(u3, h0r, h0i, consts, tt, ns):
    nb, rows, _ = u3.shape
    r = tt * ns
    fixed = lambda b, t: (0, 0)
    tile = lambda b, t: (b, t, 0)
    seq = lambda b, t: (b, 0, 0)
    const_specs = [pl.BlockSpec(c.shape, lambda b, t, nd=c.ndim: (0,) * nd) for c in consts]
    return pl.pallas_call(
        functools.partial(_s5_kernel, tt=tt, ns=ns),
        grid=(nb, rows // r),
        in_specs=[pl.BlockSpec((1, r, SSM_WIDTH), tile),
                  pl.BlockSpec((1, ns, SSM_LANES), seq), pl.BlockSpec((1, ns, SSM_LANES), seq)] + const_specs,
        out_specs=[pl.BlockSpec((1, r, SSM_WIDTH), tile),
                   pl.BlockSpec((1, ns, SSM_LANES), seq), pl.BlockSpec((1, ns, SSM_LANES), seq)],
        out_shape=[jax.ShapeDtypeStruct((nb, rows, SSM_WIDTH), F32),
                   jax.ShapeDtypeStruct((nb, ns, SSM_LANES), F32),
                   jax.ShapeDtypeStruct((nb, ns, SSM_LANES), F32)],
        scratch_shapes=[pltpu.VMEM((r, SSM_LANES), F32), pltpu.VMEM((r, SSM_LANES), F32)],
        compiler_params=_cparams(("arbitrary", "arbitrary")),
        name="s5_ns%d" % ns,
    )(u3, h0r, h0i, *consts)


def _outproj_kernel(a_ref, s_ref, x_ref, wa_ref, ws_ref, g_ref, x1_ref, xn_ref):
    x1 = x_ref[...] + _dot1(a_ref[...], wa_ref[...]) + _dot1(s_ref[...], ws_ref[...])
    x1_ref[...] = x1
    xn_ref[...] = _rms(x1, g_ref[...])


def _out_proj(attn_n, ssm_n, x2d, w_attn, w_ssm, g, tm):
    n = x2d.shape[0]
    row = lambda i: (i, 0)
    fixed = lambda i: (0, 0)
    return pl.pallas_call(
        _outproj_kernel,
        grid=(n // tm,),
        in_specs=[pl.BlockSpec((tm, ATTN_WIDTH), row), pl.BlockSpec((tm, SSM_WIDTH), row),
                  pl.BlockSpec((tm, D_MODEL), row),
                  pl.BlockSpec((ATTN_WIDTH, D_MODEL), fixed), pl.BlockSpec((SSM_WIDTH, D_MODEL), fixed),
                  pl.BlockSpec((1, D_MODEL), fixed)],
        out_specs=[pl.BlockSpec((tm, D_MODEL), row), pl.BlockSpec((tm, D_MODEL), row)],
        out_shape=[jax.ShapeDtypeStruct((n, D_MODEL), F32)] * 2,
        compiler_params=_cparams(("arbitrary",)),
        name="out_proj",
    )(attn_n, ssm_n, x2d, w_attn, w_ssm, g)


TOPK_HEADS_PER_STEP = 8


def _topk_rows(s_ref, ids, k):
    vals, picks = [], []
    sentinel = jnp.iinfo(jnp.int32).max
    for _ in range(k):
        s = s_ref[...]
        m = jnp.max(s, axis=0, keepdims=True)
        pick = jnp.min(jnp.where(s == m, ids, sentinel), axis=0, keepdims=True)
        vals.append(m)
        picks.append(pick)
        s_ref[...] = jnp.where(ids == pick, -jnp.inf, s)
    return vals, picks


def _cand_layout():
    pieces = [(0, 1, 0, PEER_TOPK)]
    pieces += [(a, a + 1, 0, SUBLANES) for a in range(1, SUBLANES)]
    pieces += [(SUBLANES, PEER_TOPK, 0, 1)]
    return pieces


N_CAND_ROWS = sum(max(a1 - a0, b1 - b0) for a0, a1, b0, b1 in _cand_layout())


def _topk_kernel(xn_ref, wqh_ref, wql_ref, keys_ref, row_ref, shift_ref, gate_ref, s_scr, c_scr, *, tm):
    xn = xn_ref[...]
    for hh in range(TOPK_HEADS_PER_STEP):
        h = pl.program_id(1) * TOPK_HEADS_PER_STEP + hh
        q = _dot3(xn, wqh_ref[h], wql_ref[h])
        for c in range(2):
            q_hi, q_lo = _split(q[:, c * PEER_HALF:(c + 1) * PEER_HALF])
            k_hi, k_lo = _split(keys_ref[2 * h + c])
            d = functools.partial(lax.dot_general, dimension_numbers=(((1,), (1,)), ((), ())),
                                  preferred_element_type=F32)
            s_scr[2 * hh + c] = d(k_hi, q_hi) + (d(k_lo, q_hi) + d(k_hi, q_lo))
    for hh in range(TOPK_HEADS_PER_STEP):
        out = slice(hh * PEER_TOPK, (hh + 1) * PEER_TOPK)
        rows, shifts, gates = _select_experts([s_scr.at[2 * hh + c] for c in range(2)], c_scr.at[hh], tm)
        row_ref[out, :] = rows
        shift_ref[out, :] = shifts
        gate_ref[out, :] = gates


def _select_experts(score_refs, cand_ref, tm):
    key_id = lax.broadcasted_iota(I32, (N_KEYS, tm), 0)
    sub_v, sub_i = [], []
    for s_ref in score_refs:
        vals, picks = _topk_rows(s_ref, key_id, PEER_TOPK)
        sub_v.append(vals)
        sub_i.append(picks)
    cs, ce, cf = [], [], []

    def rows_of(lst, lo, hi):
        return lst[lo] if hi - lo == 1 else jnp.concatenate(lst[lo:hi], axis=0)

    for a_lo, a_hi, b_lo, b_hi in _cand_layout():
        na, nbb = a_hi - a_lo, b_hi - b_lo
        rows = max(na, nbb)
        cs.append(rows_of(sub_v[0], a_lo, a_hi) + rows_of(sub_v[1], b_lo, b_hi))
        ce.append(rows_of(sub_i[0], a_lo, a_hi) * N_KEYS + rows_of(sub_i[1], b_lo, b_hi))
        r = lax.broadcasted_iota(I32, (rows, tm), 0)
        cf.append((a_lo + r) * PEER_TOPK + b_lo if na > 1 else a_lo * PEER_TOPK + b_lo + r)
    cand_ref[...] = jnp.concatenate(cs, axis=0)
    cand_e = jnp.concatenate(ce, axis=0)
    cand_f = jnp.concatenate(cf, axis=0)
    best_v, best_e = [], []
    sentinel = jnp.iinfo(jnp.int32).max
    for _ in range(PEER_TOPK):
        cand_s = cand_ref[...]
        m = jnp.max(cand_s, axis=0, keepdims=True)
        f = jnp.min(jnp.where(cand_s == m, cand_f, sentinel), axis=0, keepdims=True)
        hit = cand_f == f
        best_v.append(m)
        best_e.append(jnp.max(jnp.where(hit, cand_e, -1), axis=0, keepdims=True))
        cand_ref[...] = jnp.where(hit, -jnp.inf, cand_s)
    bv = jnp.concatenate(best_v, axis=0)
    be = jnp.concatenate(best_e, axis=0)
    ex = jnp.exp(bv - bv[0:1])
    gates = ex / jnp.sum(ex, axis=0, keepdims=True)
    rows = lax.shift_right_logical(be, 1) * SUBLANES
    shifts = ((1 - (be & 1)) * 16).astype(F32)
    return rows, shifts, gates


def _peer_topk(xn2d, wq_hi, wq_lo, keys, tm):
    n = xn2d.shape[0]
    out = lambda i, h: (h, i)
    return pl.pallas_call(
        functools.partial(_topk_kernel, tm=tm),
        grid=(n // tm, PEER_HEADS // TOPK_HEADS_PER_STEP),
        in_specs=[pl.BlockSpec((tm, D_MODEL), lambda i, h: (i, 0)),
                  pl.BlockSpec(wq_hi.shape, lambda i, h: (0, 0, 0)),
                  pl.BlockSpec(wq_lo.shape, lambda i, h: (0, 0, 0)),
                  pl.BlockSpec(keys.shape, lambda i, h: (0, 0, 0))],
        out_specs=[pl.BlockSpec((TOPK_HEADS_PER_STEP * PEER_TOPK, tm), out)] * 3,
        out_shape=[jax.ShapeDtypeStruct((N_PAIRS, n), I32), jax.ShapeDtypeStruct((N_PAIRS, n), F32),
                   jax.ShapeDtypeStruct((N_PAIRS, n), F32)],
        scratch_shapes=[pltpu.VMEM((2 * TOPK_HEADS_PER_STEP, N_KEYS, tm), F32),
                        pltpu.VMEM((TOPK_HEADS_PER_STEP, N_CAND_ROWS, tm), F32)],
        compiler_params=_cparams(("arbitrary", "arbitrary")),
        name="peer_topk",
    )(xn2d, wq_hi, wq_lo, keys)


def _pack_kernel(t_ref, o_ref):
    even = t_ref[:, 0].astype(BF16).astype(F32)
    odd = t_ref[:, 1].astype(BF16).astype(F32)
    hi = pltpu.bitcast(odd, I32) & HI_MASK
    lo = lax.shift_right_logical(pltpu.bitcast(even, I32), 16)
    o_ref[...] = hi | lo


def _pack_table(table, tb=256):
    e = table.shape[0]
    t4 = table.reshape(e // 2, 2, SUBLANES, LANES)
    packed = pl.pallas_call(
        _pack_kernel,
        grid=(e // 2 // tb,),
        in_specs=[pl.BlockSpec((tb, 2, SUBLANES, LANES), lambda i: (i, 0, 0, 0))],
        out_specs=pl.BlockSpec((tb, SUBLANES, LANES), lambda i: (i, 0, 0)),
        out_shape=jax.ShapeDtypeStruct((e // 2, SUBLANES, LANES), I32),
        compiler_params=_cparams(("arbitrary",)),
        name="pack_table",
    )(t4)
    return packed.reshape(e // 2 * SUBLANES, LANES)


def _tile(tbl, offset):
    return tbl[pl.ds(pl.multiple_of(offset, SUBLANES), SUBLANES), :]


def _expert_row(word, shift):
    return pltpu.bitcast((word << shift) & HI_MASK, F32)


def _row8(ref, k):
    return jnp.broadcast_to(ref[k:k + 1, :], (SUBLANES, LANES))


def _fold(p, steps):
    sub = lax.broadcasted_iota(I32, p[0].shape, 0)
    for step in steps:
        first = (sub % (2 * step)) < step
        n = len(p) // 2
        if 2 * step == SUBLANES:
            p = [jnp.where(first, p[j], p[j + n]) + pltpu.roll(jnp.where(first, p[j + n], p[j]), step, 0)
                 for j in range(n)]
        else:
            p = [jnp.where(first, p[j] + pltpu.roll(p[j], SUBLANES - step, 0),
                           p[j + n] + pltpu.roll(p[j + n], step, 0)) for j in range(n)]
    return p


def _column(block_ref, tile, tm, t):
    blk = block_ref[:, tile * tm:(tile + 1) * tm]
    lane = lax.broadcasted_iota(I32, blk.shape, 1)
    col = jnp.sum(jnp.where(lane == t, blk, 0.0), axis=1, keepdims=True)
    return jnp.broadcast_to(col, (N_PAIRS, LANES))


def _rows_copy(row_hbm, buf, sem, tile, tm):
    return pltpu.make_async_copy(row_hbm.at[:, pl.ds(tile * tm, tm)], buf, sem)


def _for_each_tile(row_hbm, bufs, sems, tm, body):
    i = pl.program_id(0)
    tps = len(bufs)
    first = i * tps
    total = pl.num_programs(0) * tps

    def copy(j, tile):
        return _rows_copy(row_hbm, bufs[j], sems.at[j], tile, tm)

    @pl.when(i == 0)
    def _():
        copy(0, 0).start()

    for j in range(tps):
        nxt = (j + 1) % tps
        if tps == 1:
            copy(0, first).wait()
            body(0, bufs[0])

            @pl.when(first + 1 < total)
            def _():
                copy(0, first + 1).start()
        else:
            @pl.when(first + j + 1 < total)
            def _():
                copy(nxt, first + j + 1).start()

            copy(j, first + j).wait()
            body(j, bufs[j])


def _load_table(tbl_hbm, tbl_vmem, sem):
    @pl.when(pl.program_id(0) == 0)
    def _():
        c = pltpu.make_async_copy(tbl_hbm, tbl_vmem, sem)
        c.start()
        c.wait()


ACT_RING = 8
EXPAND_RING = 8


def _ring_loop(tm, ring, expand, gather):
    n = len(ring)
    ahead = n // 2
    for r in range(ahead):
        expand(r, *ring[r])

    def ring_pass(j, carry):
        t0 = n * j
        for r in range(n):
            gather(t0 + r, *ring[r])
            expand(jnp.minimum(t0 + r + ahead, tm - 1), *ring[(r + ahead) % n])
        return carry

    lax.fori_loop(0, tm // n, ring_pass, 0)


def _tag_parity(w, is_odd):
    bits = (pltpu.bitcast(w, I32) & -2) | jnp.where(is_odd, 1, 0)
    return pltpu.bitcast(bits, F32)


def _tiles_per_step(n, tm):
    return 2 if (n // tm) % 2 == 0 else 1


def _peer_act_kernel(row_hbm, x_ref, shift_ref, gate_ref, tbl_hbm, w_ref, tbl, *scratch, tm, tps):
    bufs = scratch[:tps]
    sh_ring = scratch[tps:tps + ACT_RING]
    acc, sems = scratch[tps + ACT_RING:]
    _load_table(tbl_hbm, tbl, sems.at[tps])
    lane = lax.broadcasted_iota(I32, (SUBLANES, tm), 1)
    n_groups = N_PAIRS // SUBLANES

    def tile_body(tile, rows):
        def expand(t, sh):
            sh[...] = _column(shift_ref, tile, tm, t).astype(I32)

        def gather(t, sh):
            xt = x_ref[tile * tm + t]
            here = lane == t
            for g in range(n_groups):
                prods = []
                for j in range(SUBLANES):
                    k = g * SUBLANES + j
                    prods.append(_expert_row(_tile(tbl, rows.at[k][t]), _row8(sh, k)) * xt)
                for q, part in enumerate(_fold(prods, (SUBLANES // 2,))):
                    slot = g * (SUBLANES // 2) + q
                    acc[slot] = jnp.where(here, jnp.sum(part, axis=1, keepdims=True), acc[slot])

        acc[...] = jnp.zeros(acc.shape, F32)
        _ring_loop(tm, [(sh,) for sh in sh_ring], expand, gather)
        half = SUBLANES // 2
        act = jnp.concatenate([_fold([acc[g * half + q] for q in range(half)], (2, 1))[0]
                               for g in range(n_groups)], axis=0)
        cols = slice(tile * tm, (tile + 1) * tm)
        w_ref[:, cols] = _tag_parity(gate_ref[:, cols] * _gelu(act), shift_ref[:, cols] == 0.0)

    _for_each_tile(row_hbm, bufs, sems, tm, tile_body)


def _peer_act(rows, xn3, shift, gate, tbl, tm):
    n = xn3.shape[0]
    tps = _tiles_per_step(n, tm)
    step = tps * tm
    col = pl.BlockSpec((N_PAIRS, step), lambda i: (0, i))
    return pl.pallas_call(
        functools.partial(_peer_act_kernel, tm=tm, tps=tps),
        grid=(n // step,),
        in_specs=[pl.BlockSpec(memory_space=pl.ANY),
                  pl.BlockSpec((step, SUBLANES, LANES), lambda i: (i, 0, 0)),
                  col, col, pl.BlockSpec(memory_space=pl.ANY)],
        out_specs=col,
        out_shape=jax.ShapeDtypeStruct((N_PAIRS, n), F32),
        scratch_shapes=([pltpu.VMEM(tbl.shape, I32)] + [pltpu.SMEM((N_PAIRS, tm), I32)] * tps
                        + [pltpu.VMEM((N_PAIRS, LANES), I32)] * ACT_RING
                        + [pltpu.VMEM((N_PAIRS // 2, SUBLANES, tm), F32),
                           pltpu.SemaphoreType.DMA((tps + 1,))]),
        compiler_params=_cparams(("arbitrary",)),
        name="peer_act",
    )(rows, xn3, shift, gate, tbl)


def _peer_out_kernel(row_hbm, w_ref, x1_ref, tbl_hbm, o_ref, tbl, *scratch, tm, tps):
    bufs = scratch[:tps]
    sh_ring = scratch[tps:tps + EXPAND_RING]
    wb_ring = scratch[tps + EXPAND_RING:tps + 2 * EXPAND_RING]
    sems = scratch[tps + 2 * EXPAND_RING]
    _load_table(tbl_hbm, tbl, sems.at[tps])
    n_acc = 4

    def tile_body(tile, rows):
        def expand(t, sh, wb):
            col = _column(w_ref, tile, tm, t)
            wb[...] = col
            sh[...] = ((pltpu.bitcast(col, I32) & 1) ^ 1) << 4

        def gather(t, sh, wb):
            accs = [x1_ref[tile * tm + t]] + [jnp.zeros((SUBLANES, LANES), F32)] * (n_acc - 1)
            for k in range(N_PAIRS):
                v_row = _expert_row(_tile(tbl, rows.at[k][t]), _row8(sh, k))
                accs[k % n_acc] = accs[k % n_acc] + _row8(wb, k) * v_row
            o_ref[tile * tm + t] = (accs[0] + accs[1]) + (accs[2] + accs[3])

        _ring_loop(tm, list(zip(sh_ring, wb_ring)), expand, gather)

    _for_each_tile(row_hbm, bufs, sems, tm, tile_body)


def _peer_out(rows, w, x1_3, tbl, tm):
    n = x1_3.shape[0]
    tps = _tiles_per_step(n, tm)
    step = tps * tm
    blk = pl.BlockSpec((step, SUBLANES, LANES), lambda i: (i, 0, 0))
    col = pl.BlockSpec((N_PAIRS, step), lambda i: (0, i))
    return pl.pallas_call(
        functools.partial(_peer_out_kernel, tm=tm, tps=tps),
        grid=(n // step,),
        in_specs=[pl.BlockSpec(memory_space=pl.ANY), col, blk, pl.BlockSpec(memory_space=pl.ANY)],
        out_specs=blk,
        out_shape=jax.ShapeDtypeStruct(x1_3.shape, F32),
        scratch_shapes=([pltpu.VMEM(tbl.shape, I32)] + [pltpu.SMEM((N_PAIRS, tm), I32)] * tps
                        + [pltpu.VMEM((N_PAIRS, LANES), I32)] * EXPAND_RING
                        + [pltpu.VMEM((N_PAIRS, LANES), F32)] * EXPAND_RING
                        + [pltpu.SemaphoreType.DMA((tps + 1,))]),
        compiler_params=_cparams(("arbitrary",)),
        name="peer_out",
    )(rows, w, x1_3, tbl)


def _final_kernel(x_ref, g_ref, o_ref):
    o_ref[...] = _rms(x_ref[...], g_ref[...])


def _final_norm(x2d, g, tm):
    n = x2d.shape[0]
    return pl.pallas_call(
        _final_kernel,
        grid=(n // tm,),
        in_specs=[pl.BlockSpec((tm, D_MODEL), lambda i: (i, 0)), pl.BlockSpec((1, D_MODEL), lambda i: (0, 0))],
        out_specs=pl.BlockSpec((tm, D_MODEL), lambda i: (i, 0)),
        out_shape=jax.ShapeDtypeStruct(x2d.shape, F32),
        compiler_params=_cparams(("arbitrary",)),
        name="final_norm",
    )(x2d, g)


def _block_diag_in(b):
    eye = jnp.eye(S5_CHUNK_GROUPS, dtype=F32)
    b4 = b.reshape(S5_CHUNKS, S5_CHUNK_GROUPS, SSM_STATE, SSM_GROUP)
    return jnp.einsum('cgnp,gh->cgphn', b4, eye).reshape(S5_CHUNKS, LANES, S5_CHUNK_LANES)


def _block_diag_out(c):
    eye = jnp.eye(S5_CHUNK_GROUPS, dtype=F32)
    c4 = c.reshape(S5_CHUNKS, S5_CHUNK_GROUPS, SSM_GROUP, SSM_STATE)
    return jnp.einsum('cgpn,gh->cgnhp', c4, eye).reshape(S5_CHUNKS, S5_CHUNK_LANES, LANES)


def _layer_weights(norm_mix, w_in, b_in, attn_sinks, lam_re, lam_im, log_step, b_re, b_im, c_re, c_im,
                   d_skip, w_glu, b_glu, norm_attn_out, norm_ssm_out, w_out, norm_ffn, w_query, sub_keys,
                   u_table, v_table):
    w = {}
    w['norm_mix'] = norm_mix.reshape(1, D_MODEL)
    w['w_in'] = _split(w_in)
    w['b_in'] = b_in.reshape(1, IN_WIDTH)
    w['sinks'] = attn_sinks
    ab_re, ab_im, z_re, z_im = _s5_params(lam_re, lam_im, log_step)
    w['s5'] = (list(_split(_block_diag_in(b_re))) + list(_split(_block_diag_in(b_im)))
               + [z_re, z_im, ab_re, ab_im,
                  _block_diag_out(c_re).astype(BF16), _block_diag_out(c_im).astype(BF16),
                  d_skip.reshape(1, SSM_WIDTH), w_glu.astype(BF16), b_glu.reshape(1, SSM_WIDTH),
                  norm_ssm_out.reshape(1, SSM_WIDTH)])
    w['norm_attn_out'] = norm_attn_out.reshape(1, ATTN_WIDTH)
    w['w_out_attn'] = w_out[:ATTN_WIDTH].astype(BF16)
    w['w_out_ssm'] = w_out[ATTN_WIDTH:].astype(BF16)
    w['norm_ffn'] = norm_ffn.reshape(1, D_MODEL)
    wq = w_query.reshape(D_MODEL, PEER_HEADS, PEER_QDIM).transpose(1, 0, 2)
    w['w_query'] = _split(wq)
    w['keys'] = sub_keys.reshape(PEER_HEADS * 2, N_KEYS, PEER_HALF)
    w['u_tbl'] = _pack_table(u_table)
    w['v_tbl'] = _pack_table(v_table)
    return w


def _peer(xn2d, x1_2d, w, tm_topk, tm_gather):
    n = xn2d.shape[0]
    rows, shift, gate = _peer_topk(xn2d, w['w_query'][0], w['w_query'][1], w['keys'], tm_topk)
    wts = _peer_act(rows, xn2d.reshape(n, SUBLANES, LANES), shift, gate, w['u_tbl'], tm_gather)
    out = _peer_out(rows, wts, x1_2d.reshape(n, SUBLANES, LANES), w['v_tbl'], tm_gather)
    return out.reshape(n, D_MODEL)


def _token_tile(n, cap):
    t = cap
    while n % t:
        t //= 2
    return t


def _mix_and_ffn(x2d, attn_n, ssm_n, w, norm_final):
    n = x2d.shape[0]
    x1, xn2 = _out_proj(attn_n, ssm_n, x2d, w['w_out_attn'], w['w_out_ssm'], w['norm_ffn'], _token_tile(n, 512))
    x2 = _peer(xn2, x1, w, _token_tile(n, 256), _token_tile(n, 128))
    return x2


def _prompt_layer(x, w):
    b, t, _ = x.shape
    n = b * t
    x2d = x.reshape(n, D_MODEL)
    tm = _token_tile(t, 512)
    tables = _rope_tables(jnp.arange(t, dtype=I32))
    q, k, v, u = _in_proj(x2d, tables, t // tm, w['norm_mix'], w['w_in'][0], w['w_in'][1], w['b_in'], tm)
    k3 = k.reshape(b, t, KV_WIDTH)
    v3 = v.reshape(b, t, KV_WIDTH)
    (attn_n,) = _attention(w['sinks'], q.reshape(b, t, ATTN_WIDTH), k3, k3, v3, v3, w['norm_attn_out'], False)
    h0 = jnp.zeros((b, 1, SSM_LANES), F32)
    ssm_n, h_re, h_im = _s5(u.reshape(b, t, SSM_WIDTH), h0, h0, w['s5'], _token_tile(t, 256), 1)
    x2 = _mix_and_ffn(x2d, attn_n.reshape(n, ATTN_WIDTH), ssm_n.reshape(n, SSM_WIDTH), w, None)
    k_win = k3[:, -WINDOW:].reshape(b, WINDOW, N_KV_HEADS, HEAD_DIM)
    v_win = v3[:, -WINDOW:].reshape(b, WINDOW, N_KV_HEADS, HEAD_DIM)
    st = lambda h: h.reshape(b, N_SSM_GROUPS, SSM_STATE)
    return x2.reshape(b, t, D_MODEL), k_win, v_win, st(h_re), st(h_im)


def _sample_layer(x, k_buf, v_buf, h0_re, h0_im, w, past_len):
    b, t, _ = x.shape
    n = b * t
    x2d = x.reshape(n, D_MODEL)
    tm = _token_tile(n, 512)
    pos = past_len + jnp.arange(t, dtype=I32)
    tables = tuple(jnp.tile(a, (tm // t, 1)) for a in _rope_tables(pos))
    q, k, v, u = _in_proj(x2d, tables, 1, w['norm_mix'], w['w_in'][0], w['w_in'][1], w['b_in'], tm)
    pad = lambda a: jnp.pad(a.reshape(b, t, KV_WIDTH), ((0, 0), (0, WINDOW - t), (0, 0)))
    attn_n, k_win, v_win = _attention(w['sinks'], q.reshape(b, t, ATTN_WIDTH),
                                      k_buf.reshape(b, WINDOW, KV_WIDTH), pad(k),
                                      v_buf.reshape(b, WINDOW, KV_WIDTH), pad(v), w['norm_attn_out'], True)
    ns = min(b, S5_DECODE_SEQS)
    nbk = b // ns
    u_tm = u.reshape(nbk, ns, t, SSM_WIDTH).transpose(0, 2, 1, 3).reshape(nbk, t * ns, SSM_WIDTH)
    ssm_tm, h_re, h_im = _s5(u_tm, h0_re.reshape(nbk, ns, SSM_LANES), h0_im.reshape(nbk, ns, SSM_LANES),
                             w['s5'], t, ns)
    ssm_n = ssm_tm.reshape(nbk, t, ns, SSM_WIDTH).transpose(0, 2, 1, 3).reshape(n, SSM_WIDTH)
    x2 = _mix_and_ffn(x2d, attn_n.reshape(n, ATTN_WIDTH), ssm_n, w, None)
    win = lambda a: a.reshape(b, WINDOW, N_KV_HEADS, HEAD_DIM)
    st = lambda h: h.reshape(b, N_SSM_GROUPS, SSM_STATE)
    return x2.reshape(b, t, D_MODEL), win(k_win), win(v_win), st(h_re), st(h_im)


PAST_LEN = 16384
S5_DECODE_SEQS = 64
DECODE_SEQS_PER_STEP = 8


def kernel(x_prompt, x_sample, cache_k, cache_v, state_ssm_re, state_ssm_im, norm_mix, w_in, b_in, attn_sinks, ssm_lam_re, ssm_lam_im, ssm_log_step, ssm_b_re, ssm_b_im, ssm_c_re, ssm_c_im, ssm_d, ssm_w_glu, ssm_b_glu, norm_attn_out, norm_ssm_out, w_out, norm_ffn, peer_w_query, peer_sub_keys, peer_u, peer_v, norm_final):
    depth = norm_mix.shape[0]
    xp, xs = x_prompt, x_sample
    outs = [[] for _ in range(8)]
    for l in range(depth):
        w = _layer_weights(*[a[l] for a in (norm_mix, w_in, b_in, attn_sinks, ssm_lam_re, ssm_lam_im,
                                            ssm_log_step, ssm_b_re, ssm_b_im, ssm_c_re, ssm_c_im, ssm_d,
                                            ssm_w_glu, ssm_b_glu, norm_attn_out, norm_ssm_out, w_out,
                                            norm_ffn, peer_w_query, peer_sub_keys, peer_u, peer_v)])
        xs, k2, v2, r2, i2 = _sample_layer(xs, cache_k[l], cache_v[l], state_ssm_re[l], state_ssm_im[l],
                                           w, PAST_LEN)
        xp, k1, v1, r1, i1 = _prompt_layer(xp, w)
        for lst, a in zip(outs, (k1, v1, r1, i1, k2, v2, r2, i2)):
            lst.append(a)
    g = norm_final.reshape(1, D_MODEL)
    yp = _final_norm(xp.reshape(-1, D_MODEL), g, 512).reshape(xp.shape)
    ys = _final_norm(xs.reshape(-1, D_MODEL), g, 512).reshape(xs.shape)
    return (yp, ys) + tuple(jnp.stack(o) for o in outs)
```

```python
import functools
import math

import jax
import jax.numpy as jnp
from jax import lax
from jax.experimental import pallas as pl
from jax.experimental.pallas import tpu as pltpu

F32 = jnp.float32
BF16 = jnp.bfloat16
I32 = jnp.int32

D_MODEL = 1024
N_HEADS = 8
N_KV_HEADS = 2
HEAD_DIM = 64
ATTN_WIDTH = N_HEADS * HEAD_DIM
KV_WIDTH = N_KV_HEADS * HEAD_DIM
WINDOW = 128
ROT_DIM = HEAD_DIM // 4
ROPE_THETA = 500000.0
NEG_INF = -1e30
SSM_WIDTH = D_MODEL - ATTN_WIDTH
SSM_GROUP = 16
N_SSM_GROUPS = SSM_WIDTH // SSM_GROUP
SSM_STATE = 64
SSM_LANES = N_SSM_GROUPS * SSM_STATE
IN_WIDTH = ATTN_WIDTH + 2 * KV_WIDTH + SSM_WIDTH
PEER_HEADS = 8
N_KEYS = 128
PEER_TOPK = 16
PEER_QDIM = 256
PEER_HALF = PEER_QDIM // 2
N_PAIRS = PEER_HEADS * PEER_TOPK
RMS_EPS = 1e-5

LANES = 128
SUBLANES = 8
VMEM_LIMIT = 56 * 1024 * 1024

BF16_BITS = 16
HI_MASK = -65536

S5_CHUNK_GROUPS = LANES // SSM_GROUP
S5_CHUNKS = N_SSM_GROUPS // S5_CHUNK_GROUPS
S5_CHUNK_LANES = S5_CHUNK_GROUPS * SSM_STATE


def _cparams(sem):
    return pltpu.CompilerParams(dimension_semantics=sem, vmem_limit_bytes=VMEM_LIMIT)


def _split(w):
    hi = w.astype(BF16)
    lo = (w - hi.astype(F32)).astype(BF16)
    return hi, lo


def _dot1(a, b):
    return jnp.dot(a.astype(BF16), b, preferred_element_type=F32)


def _dot3(a, b_hi, b_lo):
    a_hi, a_lo = _split(a)
    d = functools.partial(jnp.dot, preferred_element_type=F32)
    return d(a_hi, b_hi) + (d(a_lo, b_hi) + d(a_hi, b_lo))


def _gelu(x):
    return 0.5 * x * (1.0 + lax.erf(x * (2.0 ** -0.5)))


def _rms(x, g):
    return x * lax.rsqrt(jnp.mean(x * x, axis=-1, keepdims=True) + RMS_EPS) * g


def _inproj_kernel(x_ref, g_ref, whi_ref, wlo_ref, b_ref, c_ref, sa_ref, sb_ref,
                   q_ref, k_ref, v_ref, u_ref):
    xn = _rms(x_ref[...], g_ref[...])
    proj = _dot3(xn, whi_ref[...], wlo_ref[...]) + b_ref[...]
    c = c_ref[...]
    sa = sa_ref[...]
    sb = sb_ref[...]

    def rope(t):
        return t * c + pltpu.roll(t, LANES - ROT_DIM // 2, 1) * sa + pltpu.roll(t, ROT_DIM // 2, 1) * sb

    for j in range(ATTN_WIDTH // LANES):
        q_ref[:, j * LANES:(j + 1) * LANES] = rope(proj[:, j * LANES:(j + 1) * LANES])
    k_ref[...] = rope(proj[:, ATTN_WIDTH:ATTN_WIDTH + KV_WIDTH])
    v_ref[...] = proj[:, ATTN_WIDTH + KV_WIDTH:ATTN_WIDTH + 2 * KV_WIDTH]
    u_ref[...] = proj[:, ATTN_WIDTH + 2 * KV_WIDTH:]


def _rope_tables(pos):
    half = ROT_DIM // 2
    inv_freq = ROPE_THETA ** (-jnp.arange(half, dtype=F32) * 2.0 / ROT_DIM)
    ang = pos.astype(F32)[:, None] * inv_freq[None, :]
    cos = jnp.cos(ang)
    sin = jnp.sin(ang)
    t = pos.shape[0]
    one = jnp.ones((t, HEAD_DIM - ROT_DIM), F32)
    zero = jnp.zeros((t, HEAD_DIM - ROT_DIM), F32)
    zh = jnp.zeros((t, half), F32)
    c = jnp.concatenate([cos, cos, one], axis=1)
    sa = jnp.concatenate([-sin, zh, zero], axis=1)
    sb = jnp.concatenate([zh, sin, zero], axis=1)
    return tuple(jnp.tile(a, (1, LANES // HEAD_DIM)) for a in (c, sa, sb))


def _in_proj(x2d, pos_tables, n_pos_tiles, g, w_hi, w_lo, b, tm):
    n = x2d.shape[0]
    c, sa, sb = pos_tables
    row = lambda i: (i, 0)
    fixed = lambda i: (0, 0)
    pos_map = lambda i: (i % n_pos_tiles, 0)
    return pl.pallas_call(
        _inproj_kernel,
        grid=(n // tm,),
        in_specs=[pl.BlockSpec((tm, D_MODEL), row), pl.BlockSpec((1, D_MODEL), fixed),
                  pl.BlockSpec((D_MODEL, IN_WIDTH), fixed), pl.BlockSpec((D_MODEL, IN_WIDTH), fixed),
                  pl.BlockSpec((1, IN_WIDTH), fixed),
                  pl.BlockSpec((tm, LANES), pos_map), pl.BlockSpec((tm, LANES), pos_map),
                  pl.BlockSpec((tm, LANES), pos_map)],
        out_specs=[pl.BlockSpec((tm, ATTN_WIDTH), row), pl.BlockSpec((tm, KV_WIDTH), row),
                   pl.BlockSpec((tm, KV_WIDTH), row), pl.BlockSpec((tm, SSM_WIDTH), row)],
        out_shape=[jax.ShapeDtypeStruct((n, ATTN_WIDTH), F32), jax.ShapeDtypeStruct((n, KV_WIDTH), F32),
                   jax.ShapeDtypeStruct((n, KV_WIDTH), F32), jax.ShapeDtypeStruct((n, SSM_WIDTH), F32)],
        compiler_params=_cparams(("arbitrary",)),
        name="in_proj",
    )(x2d, g, w_hi, w_lo, b, c, sa, sb)


def _attn_kernel(sink_ref, q_ref, kp_ref, kc_ref, vp_ref, vc_ref, g_ref, o_ref, *win_refs, tq, decode, nseq):
    for b in range(nseq):
        _attn_one(sink_ref, q_ref.at[b], kp_ref.at[b], kc_ref.at[b], vp_ref.at[b], vc_ref.at[b], g_ref,
                  o_ref.at[b], *[w.at[b] for w in win_refs], tq=tq, decode=decode)


def _attn_one(sink_ref, q_ref, kp_ref, kc_ref, vp_ref, vc_ref, g_ref, o_ref, *win_refs, tq, decode):
    q = q_ref[...] * (HEAD_DIM ** -0.5)
    kp = kp_ref[...]
    kc = kc_ref[...]
    vp = vp_ref[...]
    vc = vc_ref[...]
    half = LANES // 2
    lane = lax.broadcasted_iota(I32, (1, LANES), 1)
    lo = lane < half
    qi = lax.broadcasted_iota(I32, (tq, WINDOW), 0)
    kj = lax.broadcasted_iota(I32, (tq, WINDOW), 1)
    first_off = 0 if decode else jnp.where(pl.program_id(1) > 0, 0, WINDOW)
    m_prev = kj > qi + first_off
    m_cur = kj <= qi

    def variants(t):
        r = pltpu.roll(t, half, 1)
        return [[t, r], [r, t]]

    kpv = [[a.astype(BF16) for a in row] for row in variants(kp)]
    kcv = [[a.astype(BF16) for a in row] for row in variants(kc)]
    lane_sel = [lo, jnp.logical_not(lo)]
    vpv = [[jnp.where(lane_sel[s], a, 0.0).astype(BF16) for s, a in enumerate(row)] for row in variants(vp)]
    vcv = [[jnp.where(lane_sel[s], a, 0.0).astype(BF16) for s, a in enumerate(row)] for row in variants(vc)]
    nt = functools.partial(lax.dot_general, dimension_numbers=(((1,), (1,)), ((), ())),
                           preferred_element_type=F32)
    outs = []
    for j in range(ATTN_WIDTH // LANES):
        g = (2 * j) // (N_HEADS // N_KV_HEADS)
        qt = q[:, j * LANES:(j + 1) * LANES]
        acc = jnp.zeros((tq, LANES), F32)
        for s in range(2):
            h = 2 * j + s
            qm = jnp.where(lane_sel[s], qt, 0.0).astype(BF16)
            sp = jnp.where(m_prev, nt(qm, kpv[g][s]), NEG_INF)
            sc = jnp.where(m_cur, nt(qm, kcv[g][s]), NEG_INF)
            sink = sink_ref[h]
            m = jnp.maximum(jnp.maximum(jnp.max(sp, axis=-1, keepdims=True),
                                        jnp.max(sc, axis=-1, keepdims=True)), sink)
            pp = jnp.exp(sp - m)
            pc = jnp.exp(sc - m)
            den = (jnp.sum(pp, axis=-1, keepdims=True) + jnp.sum(pc, axis=-1, keepdims=True)
                   + jnp.exp(sink - m))
            o = (jnp.dot(pp.astype(BF16), vpv[g][s], preferred_element_type=F32)
                 + jnp.dot(pc.astype(BF16), vcv[g][s], preferred_element_type=F32))
            acc = acc + o / den
        outs.append(acc)
    attn = jnp.concatenate(outs, axis=1)
    o_ref[...] = _rms(attn, g_ref[...])
    if decode:
        kw_ref, vw_ref = win_refs
        kw_ref[:WINDOW - tq] = kp[tq:]
        kw_ref[WINDOW - tq:] = kc[:tq]
        vw_ref[:WINDOW - tq] = vp[tq:]
        vw_ref[WINDOW - tq:] = vc[:tq]


def _attention(sinks, q3, k_prev, k_cur, v_prev, v_cur, g, decode):
    b, t, _ = q3.shape
    tq = t if decode else WINDOW
    nb = t // tq
    nseq = math.gcd(b, DECODE_SEQS_PER_STEP) if decode else 1
    cur = lambda i, n: (i, n, 0)
    prev = (lambda i, n: (i, 0, 0)) if decode else (lambda i, n: (i, jnp.maximum(n - 1, 0), 0))
    kvb = (nseq, WINDOW, KV_WIDTH)
    out_specs = [pl.BlockSpec((nseq, tq, ATTN_WIDTH), cur)]
    out_shape = [jax.ShapeDtypeStruct((b, t, ATTN_WIDTH), F32)]
    if decode:
        out_specs += [pl.BlockSpec(kvb, cur), pl.BlockSpec(kvb, cur)]
        out_shape += [jax.ShapeDtypeStruct((b, WINDOW, KV_WIDTH), F32)] * 2
    return pl.pallas_call(
        functools.partial(_attn_kernel, tq=tq, decode=decode, nseq=nseq),
        grid=(b // nseq, nb),
        in_specs=[pl.BlockSpec(memory_space=pltpu.SMEM),
                  pl.BlockSpec((nseq, tq, ATTN_WIDTH), cur),
                  pl.BlockSpec(kvb, prev), pl.BlockSpec(kvb, cur),
                  pl.BlockSpec(kvb, prev), pl.BlockSpec(kvb, cur),
                  pl.BlockSpec((1, ATTN_WIDTH), lambda i, n: (0, 0))],
        out_specs=out_specs,
        out_shape=out_shape,
        compiler_params=_cparams(("arbitrary", "arbitrary")),
        name="attn_decode" if decode else "attn_prompt",
    )(sinks, q3, k_prev, k_cur, v_prev, v_cur, g)


def _s5_param_kernel(lr_ref, li_ref, ls_ref, abr_ref, abi_ref, zr_ref, zi_ref):
    lr = lr_ref[...]
    li = li_ref[...]
    dt = jnp.exp(ls_ref[...])
    mag = jnp.exp(lr * dt)
    ab_re = mag * jnp.cos(li * dt)
    ab_im = mag * jnp.sin(li * dt)
    den = lr * lr + li * li
    abr_ref[...] = ab_re
    abi_ref[...] = ab_im
    zr_ref[...] = ((ab_re - 1.0) * lr + ab_im * li) / den
    zi_ref[...] = (ab_im * lr - (ab_re - 1.0) * li) / den


def _s5_params(lam_re, lam_im, log_step):
    ls = jnp.broadcast_to(log_step[:, None], lam_re.shape)
    outs = pl.pallas_call(
        _s5_param_kernel,
        out_shape=[jax.ShapeDtypeStruct(lam_re.shape, F32)] * 4,
        name="s5_params",
    )(lam_re, lam_im, ls)
    return [o.reshape(1, SSM_LANES) for o in outs]


def _s5_kernel(u_ref, h0r_ref, h0i_ref, brh_ref, brl_ref, bih_ref, bil_ref, zr_ref, zi_ref,
               ar_ref, ai_ref, cr_ref, ci_ref, d_ref, wg_ref, bg_ref, g_ref,
               y_ref, hr_ref, hi_ref, sr_ref, si_ref, *, tt, ns):
    @pl.when(pl.program_id(1) == 0)
    def _():
        hr_ref[0] = h0r_ref[0]
        hi_ref[0] = h0i_ref[0]

    u = u_ref[0]
    for c in range(S5_CHUNKS):
        uc = u[:, c * LANES:(c + 1) * LANES]
        lanes = slice(c * S5_CHUNK_LANES, (c + 1) * S5_CHUNK_LANES)
        pr = _dot3(uc, brh_ref[c], brl_ref[c])
        pi = _dot3(uc, bih_ref[c], bil_ref[c])
        zr = zr_ref[:, lanes]
        zi = zi_ref[:, lanes]
        sr_ref[:, lanes] = zr * pr - zi * pi
        si_ref[:, lanes] = zr * pi + zi * pr
    ar = jnp.broadcast_to(ar_ref[...], (ns, SSM_LANES))
    ai = jnp.broadcast_to(ai_ref[...], (ns, SSM_LANES))

    if ns <= SUBLANES:
        def body(t, carry):
            hr, hi = carry
            rows = pl.ds(t * ns, ns)
            nr = ar * hr - ai * hi + sr_ref[rows, :]
            ni = ar * hi + ai * hr + si_ref[rows, :]
            sr_ref[rows, :] = nr
            si_ref[rows, :] = ni
            return nr, ni

        hr, hi = lax.fori_loop(0, tt, body, (hr_ref[0], hi_ref[0]), unroll=8)
        hr_ref[0] = hr
        hi_ref[0] = hi
    else:
        def body(t, carry):
            rows = pl.ds(pl.multiple_of(t * ns, ns), ns)
            hr = hr_ref[0]
            hi = hi_ref[0]
            nr = ar * hr - ai * hi + sr_ref[rows, :]
            ni = ar * hi + ai * hr + si_ref[rows, :]
            sr_ref[rows, :] = nr
            si_ref[rows, :] = ni
            hr_ref[0] = nr
            hi_ref[0] = ni
            return carry

        lax.fori_loop(0, tt, body, 0)

    ch = []
    for c in range(S5_CHUNKS):
        lanes = slice(c * S5_CHUNK_LANES, (c + 1) * S5_CHUNK_LANES)
        ch.append(_dot1(sr_ref[:, lanes], cr_ref[c]) - _dot1(si_ref[:, lanes], ci_ref[c]))
    y = jnp.concatenate(ch, axis=1) + d_ref[...] * u
    y = _gelu(y)
    gate = _dot1(y, wg_ref[...]) + bg_ref[...]
    y = y * (1.0 / (1.0 + jnp.exp(-gate)))
    y_ref[0] = _rms(y, g_ref[...])


def _s5(u3, h0r, h0i, consts, tt, ns):
    nb, rows, _ = u3.shape
    r = tt * ns
    fixed = lambda b, t: (0, 0)
    tile = lambda b, t: (b, t, 0)
    seq = lambda b, t: (b, 0, 0)
    const_specs = [pl.BlockSpec(c.shape, lambda b, t, nd=c.ndim: (0,) * nd) for c in consts]
    return pl.pallas_call(
        functools.partial(_s5_kernel, tt=tt, ns=ns),
        grid=(nb, rows // r),
        in_specs=[pl.BlockSpec((1, r, SSM_WIDTH), tile),
                  pl.BlockSpec((1, ns, SSM_LANES), seq), pl.BlockSpec((1, ns, SSM_LANES), seq)] + const_specs,
        out_specs=[pl.BlockSpec((1, r, SSM_WIDTH), tile),
                   pl.BlockSpec((1, ns, SSM_LANES), seq), pl.BlockSpec((1, ns, SSM_LANES), seq)],
        out_shape=[jax.ShapeDtypeStruct((nb, rows, SSM_WIDTH), F32),
                   jax.ShapeDtypeStruct((nb, ns, SSM_LANES), F32),
                   jax.ShapeDtypeStruct((nb, ns, SSM_LANES), F32)],
        scratch_shapes=[pltpu.VMEM((r, SSM_LANES), F32), pltpu.VMEM((r, SSM_LANES), F32)],
        compiler_params=_cparams(("arbitrary", "arbitrary")),
        name="s5_ns%d" % ns,
    )(u3, h0r, h0i, *consts)


def _outproj_kernel(a_ref, s_ref, x_ref, wa_ref, ws_ref, g_ref, x1_ref, xn_ref):
    x1 = x_ref[...] + _dot1(a_ref[...], wa_ref[...]) + _dot1(s_ref[...], ws_ref[...])
    x1_ref[...] = x1
    xn_ref[...] = _rms(x1, g_ref[...])


def _out_proj(attn_n, ssm_n, x2d, w_attn, w_ssm, g, tm):
    n = x2d.shape[0]
    row = lambda i: (i, 0)
    fixed = lambda i: (0, 0)
    return pl.pallas_call(
        _outproj_kernel,
        grid=(n // tm,),
        in_specs=[pl.BlockSpec((tm, ATTN_WIDTH), row), pl.BlockSpec((tm, SSM_WIDTH), row),
                  pl.BlockSpec((tm, D_MODEL), row),
                  pl.BlockSpec((ATTN_WIDTH, D_MODEL), fixed), pl.BlockSpec((SSM_WIDTH, D_MODEL), fixed),
                  pl.BlockSpec((1, D_MODEL), fixed)],
        out_specs=[pl.BlockSpec((tm, D_MODEL), row), pl.BlockSpec((tm, D_MODEL), row)],
        out_shape=[jax.ShapeDtypeStruct((n, D_MODEL), F32)] * 2,
        compiler_params=_cparams(("arbitrary",)),
        name="out_proj",
    )(attn_n, ssm_n, x2d, w_attn, w_ssm, g)


TOPK_HEADS_PER_STEP = 8


def _topk_rows(s_ref, ids, k):
    vals, picks = [], []
    sentinel = jnp.iinfo(jnp.int32).max
    for _ in range(k):
        s = s_ref[...]
        m = jnp.max(s, axis=0, keepdims=True)
        pick = jnp.min(jnp.where(s == m, ids, sentinel), axis=0, keepdims=True)
        vals.append(m)
        picks.append(pick)
        s_ref[...] = jnp.where(ids == pick, -jnp.inf, s)
    return vals, picks


def _cand_layout():
    pieces = [(0, 1, 0, PEER_TOPK)]
    pieces += [(a, a + 1, 0, SUBLANES) for a in range(1, SUBLANES)]
    pieces += [(SUBLANES, PEER_TOPK, 0, 1)]
    return pieces


N_CAND_ROWS = sum(max(a1 - a0, b1 - b0) for a0, a1, b0, b1 in _cand_layout())


def _topk_kernel(xn_ref, wqh_ref, wql_ref, keys_ref, row_ref, shift_ref, gate_ref, s_scr, c_scr, *, tm):
    xn = xn_ref[...]
    for hh in range(TOPK_HEADS_PER_STEP):
        h = pl.program_id(1) * TOPK_HEADS_PER_STEP + hh
        q = _dot3(xn, wqh_ref[h], wql_ref[h])
        for c in range(2):
            q_hi, q_lo = _split(q[:, c * PEER_HALF:(c + 1) * PEER_HALF])
            k_hi, k_lo = _split(keys_ref[2 * h + c])
            d = functools.partial(lax.dot_general, dimension_numbers=(((1,), (1,)), ((), ())),
                                  preferred_element_type=F32)
            s_scr[2 * hh + c] = d(k_hi, q_hi) + (d(k_lo, q_hi) + d(k_hi, q_lo))
    for hh in range(TOPK_HEADS_PER_STEP):
        out = slice(hh * PEER_TOPK, (hh + 1) * PEER_TOPK)
        rows, shifts, gates = _select_experts([s_scr.at[2 * hh + c] for c in range(2)], c_scr.at[hh], tm)
        row_ref[out, :] = rows
        shift_ref[out, :] = shifts
        gate_ref[out, :] = gates


def _select_experts(score_refs, cand_ref, tm):
    key_id = lax.broadcasted_iota(I32, (N_KEYS, tm), 0)
    sub_v, sub_i = [], []
    for s_ref in score_refs:
        vals, picks = _topk_rows(s_ref, key_id, PEER_TOPK)
        sub_v.append(vals)
        sub_i.append(picks)
    cs, ce, cf = [], [], []

    def rows_of(lst, lo, hi):
        return lst[lo] if hi - lo == 1 else jnp.concatenate(lst[lo:hi], axis=0)

    for a_lo, a_hi, b_lo, b_hi in _cand_layout():
        na, nbb = a_hi - a_lo, b_hi - b_lo
        rows = max(na, nbb)
        cs.append(rows_of(sub_v[0], a_lo, a_hi) + rows_of(sub_v[1], b_lo, b_hi))
        ce.append(rows_of(sub_i[0], a_lo, a_hi) * N_KEYS + rows_of(sub_i[1], b_lo, b_hi))
        r = lax.broadcasted_iota(I32, (rows, tm), 0)
        cf.append((a_lo + r) * PEER_TOPK + b_lo if na > 1 else a_lo * PEER_TOPK + b_lo + r)
    cand_ref[...] = jnp.concatenate(cs, axis=0)
    cand_e = jnp.concatenate(ce, axis=0)
    cand_f = jnp.concatenate(cf, axis=0)
    best_v, best_e = [], []
    sentinel = jnp.iinfo(jnp.int32).max
    for _ in range(PEER_TOPK):
        cand_s = cand_ref[...]
        m = jnp.max(cand_s, axis=0, keepdims=True)
        f = jnp.min(jnp.where(cand_s == m, cand_f, sentinel), axis=0, keepdims=True)
        hit = cand_f == f
        best_v.append(m)
        best_e.append(jnp.max(jnp.where(hit, cand_e, -1), axis=0, keepdims=True))
        cand_ref[...] = jnp.where(hit, -jnp.inf, cand_s)
    bv = jnp.concatenate(best_v, axis=0)
    be = jnp.concatenate(best_e, axis=0)
    ex = jnp.exp(bv - bv[0:1])
    gates = ex / jnp.sum(ex, axis=0, keepdims=True)
    rows = lax.shift_right_logical(be, 1) * SUBLANES
    shifts = ((1 - (be & 1)) * BF16_BITS).astype(F32)
    return rows, shifts, gates


def _peer_topk(xn2d, wq_hi, wq_lo, keys, tm):
    n = xn2d.shape[0]
    out = lambda i, h: (h, i)
    return pl.pallas_call(
        functools.partial(_topk_kernel, tm=tm),
        grid=(n // tm, PEER_HEADS // TOPK_HEADS_PER_STEP),
        in_specs=[pl.BlockSpec((tm, D_MODEL), lambda i, h: (i, 0)),
                  pl.BlockSpec(wq_hi.shape, lambda i, h: (0, 0, 0)),
                  pl.BlockSpec(wq_lo.shape, lambda i, h: (0, 0, 0)),
                  pl.BlockSpec(keys.shape, lambda i, h: (0, 0, 0))],
        out_specs=[pl.BlockSpec((TOPK_HEADS_PER_STEP * PEER_TOPK, tm), out)] * 3,
        out_shape=[jax.ShapeDtypeStruct((N_PAIRS, n), I32), jax.ShapeDtypeStruct((N_PAIRS, n), F32),
                   jax.ShapeDtypeStruct((N_PAIRS, n), F32)],
        scratch_shapes=[pltpu.VMEM((2 * TOPK_HEADS_PER_STEP, N_KEYS, tm), F32),
                        pltpu.VMEM((TOPK_HEADS_PER_STEP, N_CAND_ROWS, tm), F32)],
        compiler_params=_cparams(("arbitrary", "arbitrary")),
        name="peer_topk",
    )(xn2d, wq_hi, wq_lo, keys)


def _pack_kernel(t_ref, o_ref):
    even = t_ref[:, 0].astype(BF16).astype(F32)
    odd = t_ref[:, 1].astype(BF16).astype(F32)
    hi = pltpu.bitcast(odd, I32) & HI_MASK
    lo = lax.shift_right_logical(pltpu.bitcast(even, I32), BF16_BITS)
    o_ref[...] = hi | lo


def _pack_table(table, tb=256):
    e = table.shape[0]
    t4 = table.reshape(e // 2, 2, SUBLANES, LANES)
    packed = pl.pallas_call(
        _pack_kernel,
        grid=(e // 2 // tb,),
        in_specs=[pl.BlockSpec((tb, 2, SUBLANES, LANES), lambda i: (i, 0, 0, 0))],
        out_specs=pl.BlockSpec((tb, SUBLANES, LANES), lambda i: (i, 0, 0)),
        out_shape=jax.ShapeDtypeStruct((e // 2, SUBLANES, LANES), I32),
        compiler_params=_cparams(("arbitrary",)),
        name="pack_table",
    )(t4)
    return packed.reshape(e // 2 * SUBLANES, LANES)


def _tile(tbl, offset):
    return tbl[pl.ds(pl.multiple_of(offset, SUBLANES), SUBLANES), :]


def _expert_row(word, shift):
    return pltpu.bitcast((word << shift) & HI_MASK, F32)


def _row8(ref, k):
    return jnp.broadcast_to(ref[k:k + 1, :], (SUBLANES, LANES))


def _fold(p, steps):
    sub = lax.broadcasted_iota(I32, p[0].shape, 0)
    for step in steps:
        first = (sub % (2 * step)) < step
        n = len(p) // 2
        if 2 * step == SUBLANES:
            p = [jnp.where(first, p[j], p[j + n]) + pltpu.roll(jnp.where(first, p[j + n], p[j]), step, 0)
                 for j in range(n)]
        else:
            p = [jnp.where(first, p[j] + pltpu.roll(p[j], SUBLANES - step, 0),
                           p[j + n] + pltpu.roll(p[j + n], step, 0)) for j in range(n)]
    return p


def _column(block_ref, tile, tm, t):
    blk = block_ref[:, tile * tm:(tile + 1) * tm]
    lane = lax.broadcasted_iota(I32, blk.shape, 1)
    col = jnp.sum(jnp.where(lane == t, blk, 0.0), axis=1, keepdims=True)
    return jnp.broadcast_to(col, (N_PAIRS, LANES))


def _rows_copy(row_hbm, buf, sem, tile, tm):
    return pltpu.make_async_copy(row_hbm.at[:, pl.ds(tile * tm, tm)], buf, sem)


def _for_each_tile(row_hbm, bufs, sems, tm, body):
    i = pl.program_id(0)
    tps = len(bufs)
    first = i * tps
    total = pl.num_programs(0) * tps

    def copy(j, tile):
        return _rows_copy(row_hbm, bufs[j], sems.at[j], tile, tm)

    @pl.when(i == 0)
    def _():
        copy(0, 0).start()

    for j in range(tps):
        nxt = (j + 1) % tps
        if tps == 1:
            copy(0, first).wait()
            body(0, bufs[0])

            @pl.when(first + 1 < total)
            def _():
                copy(0, first + 1).start()
        else:
            @pl.when(first + j + 1 < total)
            def _():
                copy(nxt, first + j + 1).start()

            copy(j, first + j).wait()
            body(j, bufs[j])


def _load_table(tbl_hbm, tbl_vmem, sem):
    @pl.when(pl.program_id(0) == 0)
    def _():
        c = pltpu.make_async_copy(tbl_hbm, tbl_vmem, sem)
        c.start()
        c.wait()


ACT_RING = 8
EXPAND_RING = 8


def _ring_loop(tm, ring, expand, gather):
    n = len(ring)
    ahead = n // 2
    for r in range(ahead):
        expand(r, *ring[r])

    def ring_pass(j, carry):
        t0 = n * j
        for r in range(n):
            gather(t0 + r, *ring[r])
            expand(jnp.minimum(t0 + r + ahead, tm - 1), *ring[(r + ahead) % n])
        return carry

    lax.fori_loop(0, tm // n, ring_pass, 0)


def _tag_parity(w, is_odd):
    bits = (pltpu.bitcast(w, I32) & -2) | jnp.where(is_odd, 1, 0)
    return pltpu.bitcast(bits, F32)


def _tiles_per_step(n, tm):
    return 2 if (n // tm) % 2 == 0 else 1


def _peer_act_kernel(row_hbm, x_ref, shift_ref, gate_ref, tbl_hbm, w_ref, tbl, *scratch, tm, tps):
    bufs = scratch[:tps]
    sh_ring = scratch[tps:tps + ACT_RING]
    acc, sems = scratch[tps + ACT_RING:]
    _load_table(tbl_hbm, tbl, sems.at[tps])
    lane = lax.broadcasted_iota(I32, (SUBLANES, tm), 1)
    n_groups = N_PAIRS // SUBLANES

    def tile_body(tile, rows):
        def expand(t, sh):
            sh[...] = _column(shift_ref, tile, tm, t).astype(I32)

        def gather(t, sh):
            xt = x_ref[tile * tm + t]
            here = lane == t
            for g in range(n_groups):
                prods = []
                for j in range(SUBLANES):
                    k = g * SUBLANES + j
                    prods.append(_expert_row(_tile(tbl, rows.at[k][t]), _row8(sh, k)) * xt)
                for q, part in enumerate(_fold(prods, (SUBLANES // 2,))):
                    slot = g * (SUBLANES // 2) + q
                    acc[slot] = jnp.where(here, jnp.sum(part, axis=1, keepdims=True), acc[slot])

        acc[...] = jnp.zeros(acc.shape, F32)
        _ring_loop(tm, [(sh,) for sh in sh_ring], expand, gather)
        half = SUBLANES // 2
        act = jnp.concatenate([_fold([acc[g * half + q] for q in range(half)], (2, 1))[0]
                               for g in range(n_groups)], axis=0)
        cols = slice(tile * tm, (tile + 1) * tm)
        w_ref[:, cols] = _tag_parity(gate_ref[:, cols] * _gelu(act), shift_ref[:, cols] == 0.0)

    _for_each_tile(row_hbm, bufs, sems, tm, tile_body)


def _peer_act(rows, xn3, shift, gate, tbl, tm):
    n = xn3.shape[0]
    tps = _tiles_per_step(n, tm)
    step = tps * tm
    col = pl.BlockSpec((N_PAIRS, step), lambda i: (0, i))
    return pl.pallas_call(
        functools.partial(_peer_act_kernel, tm=tm, tps=tps),
        grid=(n // step,),
        in_specs=[pl.BlockSpec(memory_space=pl.ANY),
                  pl.BlockSpec((step, SUBLANES, LANES), lambda i: (i, 0, 0)),
                  col, col, pl.BlockSpec(memory_space=pl.ANY)],
        out_specs=col,
        out_shape=jax.ShapeDtypeStruct((N_PAIRS, n), F32),
        scratch_shapes=([pltpu.VMEM(tbl.shape, I32)] + [pltpu.SMEM((N_PAIRS, tm), I32)] * tps
                        + [pltpu.VMEM((N_PAIRS, LANES), I32)] * ACT_RING
                        + [pltpu.VMEM((N_PAIRS // 2, SUBLANES, tm), F32),
                           pltpu.SemaphoreType.DMA((tps + 1,))]),
        compiler_params=_cparams(("arbitrary",)),
        name="peer_act",
    )(rows, xn3, shift, gate, tbl)


def _peer_out_kernel(row_hbm, w_ref, x1_ref, tbl_hbm, o_ref, tbl, *scratch, tm, tps):
    bufs = scratch[:tps]
    sh_ring = scratch[tps:tps + EXPAND_RING]
    wb_ring = scratch[tps + EXPAND_RING:tps + 2 * EXPAND_RING]
    sems = scratch[tps + 2 * EXPAND_RING]
    _load_table(tbl_hbm, tbl, sems.at[tps])
    n_acc = 4

    def tile_body(tile, rows):
        def expand(t, sh, wb):
            col = _column(w_ref, tile, tm, t)
            wb[...] = col
            sh[...] = ((pltpu.bitcast(col, I32) & 1) ^ 1) * BF16_BITS

        def gather(t, sh, wb):
            accs = [x1_ref[tile * tm + t]] + [jnp.zeros((SUBLANES, LANES), F32)] * (n_acc - 1)
            for k in range(N_PAIRS):
                v_row = _expert_row(_tile(tbl, rows.at[k][t]), _row8(sh, k))
                accs[k % n_acc] = accs[k % n_acc] + _row8(wb, k) * v_row
            o_ref[tile * tm + t] = (accs[0] + accs[1]) + (accs[2] + accs[3])

        _ring_loop(tm, list(zip(sh_ring, wb_ring)), expand, gather)

    _for_each_tile(row_hbm, bufs, sems, tm, tile_body)


def _peer_out(rows, w, x1_3, tbl, tm):
    n = x1_3.shape[0]
    tps = _tiles_per_step(n, tm)
    step = tps * tm
    blk = pl.BlockSpec((step, SUBLANES, LANES), lambda i: (i, 0, 0))
    col = pl.BlockSpec((N_PAIRS, step), lambda i: (0, i))
    return pl.pallas_call(
        functools.partial(_peer_out_kernel, tm=tm, tps=tps),
        grid=(n // step,),
        in_specs=[pl.BlockSpec(memory_space=pl.ANY), col, blk, pl.BlockSpec(memory_space=pl.ANY)],
        out_specs=blk,
        out_shape=jax.ShapeDtypeStruct(x1_3.shape, F32),
        scratch_shapes=([pltpu.VMEM(tbl.shape, I32)] + [pltpu.SMEM((N_PAIRS, tm), I32)] * tps
                        + [pltpu.VMEM((N_PAIRS, LANES), I32)] * EXPAND_RING
                        + [pltpu.VMEM((N_PAIRS, LANES), F32)] * EXPAND_RING
                        + [pltpu.SemaphoreType.DMA((tps + 1,))]),
        compiler_params=_cparams(("arbitrary",)),
        name="peer_out",
    )(rows, w, x1_3, tbl)


def _final_kernel(x_ref, g_ref, o_ref):
    o_ref[...] = _rms(x_ref[...], g_ref[...])


def _final_norm(x2d, g, tm):
    n = x2d.shape[0]
    return pl.pallas_call(
        _final_kernel,
        grid=(n // tm,),
        in_specs=[pl.BlockSpec((tm, D_MODEL), lambda i: (i, 0)), pl.BlockSpec((1, D_MODEL), lambda i: (0, 0))],
        out_specs=pl.BlockSpec((tm, D_MODEL), lambda i: (i, 0)),
        out_shape=jax.ShapeDtypeStruct(x2d.shape, F32),
        compiler_params=_cparams(("arbitrary",)),
        name="final_norm",
    )(x2d, g)


def _block_diag_in(b):
    eye = jnp.eye(S5_CHUNK_GROUPS, dtype=F32)
    b4 = b.reshape(S5_CHUNKS, S5_CHUNK_GROUPS, SSM_STATE, SSM_GROUP)
    return jnp.einsum('cgnp,gh->cgphn', b4, eye).reshape(S5_CHUNKS, LANES, S5_CHUNK_LANES)


def _block_diag_out(c):
    eye = jnp.eye(S5_CHUNK_GROUPS, dtype=F32)
    c4 = c.reshape(S5_CHUNKS, S5_CHUNK_GROUPS, SSM_GROUP, SSM_STATE)
    return jnp.einsum('cgpn,gh->cgnhp', c4, eye).reshape(S5_CHUNKS, S5_CHUNK_LANES, LANES)


def _layer_weights(norm_mix, w_in, b_in, attn_sinks, lam_re, lam_im, log_step, b_re, b_im, c_re, c_im,
                   d_skip, w_glu, b_glu, norm_attn_out, norm_ssm_out, w_out, norm_ffn, w_query, sub_keys,
                   u_table, v_table):
    w = {}
    w['norm_mix'] = norm_mix.reshape(1, D_MODEL)
    w['w_in'] = _split(w_in)
    w['b_in'] = b_in.reshape(1, IN_WIDTH)
    w['sinks'] = attn_sinks
    ab_re, ab_im, z_re, z_im = _s5_params(lam_re, lam_im, log_step)
    w['s5'] = (list(_split(_block_diag_in(b_re))) + list(_split(_block_diag_in(b_im)))
               + [z_re, z_im, ab_re, ab_im,
                  _block_diag_out(c_re).astype(BF16), _block_diag_out(c_im).astype(BF16),
                  d_skip.reshape(1, SSM_WIDTH), w_glu.astype(BF16), b_glu.reshape(1, SSM_WIDTH),
                  norm_ssm_out.reshape(1, SSM_WIDTH)])
    w['norm_attn_out'] = norm_attn_out.reshape(1, ATTN_WIDTH)
    w['w_out_attn'] = w_out[:ATTN_WIDTH].astype(BF16)
    w['w_out_ssm'] = w_out[ATTN_WIDTH:].astype(BF16)
    w['norm_ffn'] = norm_ffn.reshape(1, D_MODEL)
    wq = w_query.reshape(D_MODEL, PEER_HEADS, PEER_QDIM).transpose(1, 0, 2)
    w['w_query'] = _split(wq)
    w['keys'] = sub_keys.reshape(PEER_HEADS * 2, N_KEYS, PEER_HALF)
    w['u_tbl'] = _pack_table(u_table)
    w['v_tbl'] = _pack_table(v_table)
    return w


def _peer(xn2d, x1_2d, w, tm_topk, tm_gather):
    n = xn2d.shape[0]
    rows, shift, gate = _peer_topk(xn2d, w['w_query'][0], w['w_query'][1], w['keys'], tm_topk)
    wts = _peer_act(rows, xn2d.reshape(n, SUBLANES, LANES), shift, gate, w['u_tbl'], tm_gather)
    out = _peer_out(rows, wts, x1_2d.reshape(n, SUBLANES, LANES), w['v_tbl'], tm_gather)
    return out.reshape(n, D_MODEL)


def _token_tile(n, cap):
    t = cap
    while n % t:
        t //= 2
    return t


def _mix_and_ffn(x2d, attn_n, ssm_n, w, norm_final):
    n = x2d.shape[0]
    x1, xn2 = _out_proj(attn_n, ssm_n, x2d, w['w_out_attn'], w['w_out_ssm'], w['norm_ffn'], _token_tile(n, 512))
    x2 = _peer(xn2, x1, w, _token_tile(n, 256), _token_tile(n, 128))
    return x2


def _prompt_layer(x, w):
    b, t, _ = x.shape
    n = b * t
    x2d = x.reshape(n, D_MODEL)
    tm = _token_tile(t, 512)
    tables = _rope_tables(jnp.arange(t, dtype=I32))
    q, k, v, u = _in_proj(x2d, tables, t // tm, w['norm_mix'], w['w_in'][0], w['w_in'][1], w['b_in'], tm)
    k3 = k.reshape(b, t, KV_WIDTH)
    v3 = v.reshape(b, t, KV_WIDTH)
    (attn_n,) = _attention(w['sinks'], q.reshape(b, t, ATTN_WIDTH), k3, k3, v3, v3, w['norm_attn_out'], False)
    h0 = jnp.zeros((b, 1, SSM_LANES), F32)
    ssm_n, h_re, h_im = _s5(u.reshape(b, t, SSM_WIDTH), h0, h0, w['s5'], _token_tile(t, 256), 1)
    x2 = _mix_and_ffn(x2d, attn_n.reshape(n, ATTN_WIDTH), ssm_n.reshape(n, SSM_WIDTH), w, None)
    k_win = k3[:, -WINDOW:].reshape(b, WINDOW, N_KV_HEADS, HEAD_DIM)
    v_win = v3[:, -WINDOW:].reshape(b, WINDOW, N_KV_HEADS, HEAD_DIM)
    st = lambda h: h.reshape(b, N_SSM_GROUPS, SSM_STATE)
    return x2.reshape(b, t, D_MODEL), k_win, v_win, st(h_re), st(h_im)


def _sample_layer(x, k_buf, v_buf, h0_re, h0_im, w, past_len):
    b, t, _ = x.shape
    n = b * t
    x2d = x.reshape(n, D_MODEL)
    tm = _token_tile(n, 512)
    pos = past_len + jnp.arange(t, dtype=I32)
    tables = tuple(jnp.tile(a, (tm // t, 1)) for a in _rope_tables(pos))
    q, k, v, u = _in_proj(x2d, tables, 1, w['norm_mix'], w['w_in'][0], w['w_in'][1], w['b_in'], tm)
    pad = lambda a: jnp.pad(a.reshape(b, t, KV_WIDTH), ((0, 0), (0, WINDOW - t), (0, 0)))
    attn_n, k_win, v_win = _attention(w['sinks'], q.reshape(b, t, ATTN_WIDTH),
                                      k_buf.reshape(b, WINDOW, KV_WIDTH), pad(k),
                                      v_buf.reshape(b, WINDOW, KV_WIDTH), pad(v), w['norm_attn_out'], True)
    ns = min(b, S5_DECODE_SEQS)
    nbk = b // ns
    u_tm = u.reshape(nbk, ns, t, SSM_WIDTH).transpose(0, 2, 1, 3).reshape(nbk, t * ns, SSM_WIDTH)
    ssm_tm, h_re, h_im = _s5(u_tm, h0_re.reshape(nbk, ns, SSM_LANES), h0_im.reshape(nbk, ns, SSM_LANES),
                             w['s5'], t, ns)
    ssm_n = ssm_tm.reshape(nbk, t, ns, SSM_WIDTH).transpose(0, 2, 1, 3).reshape(n, SSM_WIDTH)
    x2 = _mix_and_ffn(x2d, attn_n.reshape(n, ATTN_WIDTH), ssm_n, w, None)
    win = lambda a: a.reshape(b, WINDOW, N_KV_HEADS, HEAD_DIM)
    st = lambda h: h.reshape(b, N_SSM_GROUPS, SSM_STATE)
    return x2.reshape(b, t, D_MODEL), win(k_win), win(v_win), st(h_re), st(h_im)


PAST_LEN = 16384
S5_DECODE_SEQS = 64
DECODE_SEQS_PER_STEP = 8


def kernel(x_prompt, x_sample, cache_k, cache_v, state_ssm_re, state_ssm_im, norm_mix, w_in, b_in, attn_sinks, ssm_lam_re, ssm_lam_im, ssm_log_step, ssm_b_re, ssm_b_im, ssm_c_re, ssm_c_im, ssm_d, ssm_w_glu, ssm_b_glu, norm_attn_out, norm_ssm_out, w_out, norm_ffn, peer_w_query, peer_sub_keys, peer_u, peer_v, norm_final):
    depth = norm_mix.shape[0]
    xp, xs = x_prompt, x_sample
    outs = [[] for _ in range(8)]
    for l in range(depth):
        w = _layer_weights(*[a[l] for a in (norm_mix, w_in, b_in, attn_sinks, ssm_lam_re, ssm_lam_im,
                                            ssm_log_step, ssm_b_re, ssm_b_im, ssm_c_re, ssm_c_im, ssm_d,
                                            ssm_w_glu, ssm_b_glu, norm_attn_out, norm_ssm_out, w_out,
                                            norm_ffn, peer_w_query, peer_sub_keys, peer_u, peer_v)])
        xs, k2, v2, r2, i2 = _sample_layer(xs, cache_k[l], cache_v[l], state_ssm_re[l], state_ssm_im[l],
                                           w, PAST_LEN)
        xp, k1, v1, r1, i1 = _prompt_layer(xp, w)
        for lst, a in zip(outs, (k1, v1, r1, i1, k2, v2, r2, i2)):
            lst.append(a)
    g = norm_final.reshape(1, D_MODEL)
    yp = _final_norm(xp.reshape(-1, D_MODEL), g, 512).reshape(xp.shape)
    ys = _final_norm(xs.reshape(-1, D_MODEL), g, 512).reshape(xs.shape)
    return (yp, ys) + tuple(jnp.stack(o) for o in outs)
```

```python
import functools
import math

import jax
import jax.numpy as jnp
from jax import lax
from jax.experimental import pallas as pl
from jax.experimental.pallas import tpu as pltpu

F32 = jnp.float32
BF16 = jnp.bfloat16
I32 = jnp.int32

D_MODEL = 1024
N_HEADS = 8
N_KV_HEADS = 2
HEAD_DIM = 64
ATTN_WIDTH = N_HEADS * HEAD_DIM
KV_WIDTH = N_KV_HEADS * HEAD_DIM
WINDOW = 128
ROT_DIM = HEAD_DIM // 4
ROPE_THETA = 500000.0
NEG_INF = -1e30
SSM_WIDTH = D_MODEL - ATTN_WIDTH
SSM_GROUP = 16
N_SSM_GROUPS = SSM_WIDTH // SSM_GROUP
SSM_STATE = 64
SSM_LANES = N_SSM_GROUPS * SSM_STATE
IN_WIDTH = ATTN_WIDTH + 2 * KV_WIDTH + SSM_WIDTH
PEER_HEADS = 8
N_KEYS = 128
PEER_TOPK = 16
PEER_QDIM = 256
PEER_HALF = PEER_QDIM // 2
N_PAIRS = PEER_HEADS * PEER_TOPK
RMS_EPS = 1e-5

LANES = 128
SUBLANES = 8
VMEM_LIMIT = 56 * 1024 * 1024

BF16_BITS = 16
HI_MASK = -65536

S5_CHUNK_GROUPS = LANES // SSM_GROUP
S5_CHUNKS = N_SSM_GROUPS // S5_CHUNK_GROUPS
S5_CHUNK_LANES = S5_CHUNK_GROUPS * SSM_STATE


def _cparams(sem):
    return pltpu.CompilerParams(dimension_semantics=sem, vmem_limit_bytes=VMEM_LIMIT)


def _split(w):
    hi = w.astype(BF16)
    lo = (w - hi.astype(F32)).astype(BF16)
    return hi, lo


def _dot1(a, b):
    return jnp.dot(a.astype(BF16), b, preferred_element_type=F32)


def _dot3(a, b_hi, b_lo):
    a_hi, a_lo = _split(a)
    d = functools.partial(jnp.dot, preferred_element_type=F32)
    return d(a_hi, b_hi) + (d(a_lo, b_hi) + d(a_hi, b_lo))


def _gelu(x):
    return 0.5 * x * (1.0 + lax.erf(x * (2.0 ** -0.5)))


def _rms(x, g):
    return x * lax.rsqrt(jnp.mean(x * x, axis=-1, keepdims=True) + RMS_EPS) * g


def _inproj_kernel(x_ref, g_ref, whi_ref, wlo_ref, b_ref, c_ref, sa_ref, sb_ref,
                   q_ref, k_ref, v_ref, u_ref):
    xn = _rms(x_ref[...], g_ref[...])
    proj = _dot3(xn, whi_ref[...], wlo_ref[...]) + b_ref[...]
    c = c_ref[...]
    sa = sa_ref[...]
    sb = sb_ref[...]

    def rope(t):
        return t * c + pltpu.roll(t, LANES - ROT_DIM // 2, 1) * sa + pltpu.roll(t, ROT_DIM // 2, 1) * sb

    for j in range(ATTN_WIDTH // LANES):
        q_ref[:, j * LANES:(j + 1) * LANES] = rope(proj[:, j * LANES:(j + 1) * LANES])
    k_ref[...] = rope(proj[:, ATTN_WIDTH:ATTN_WIDTH + KV_WIDTH])
    v_ref[...] = proj[:, ATTN_WIDTH + KV_WIDTH:ATTN_WIDTH + 2 * KV_WIDTH]
    u_ref[...] = proj[:, ATTN_WIDTH + 2 * KV_WIDTH:]


def _rope_tables(pos):
    half = ROT_DIM // 2
    inv_freq = ROPE_THETA ** (-jnp.arange(half, dtype=F32) * 2.0 / ROT_DIM)
    ang = pos.astype(F32)[:, None] * inv_freq[None, :]
    cos = jnp.cos(ang)
    sin = jnp.sin(ang)
    t = pos.shape[0]
    one = jnp.ones((t, HEAD_DIM - ROT_DIM), F32)
    zero = jnp.zeros((t, HEAD_DIM - ROT_DIM), F32)
    zh = jnp.zeros((t, half), F32)
    c = jnp.concatenate([cos, cos, one], axis=1)
    sa = jnp.concatenate([-sin, zh, zero], axis=1)
    sb = jnp.concatenate([zh, sin, zero], axis=1)
    return tuple(jnp.tile(a, (1, LANES // HEAD_DIM)) for a in (c, sa, sb))


def _in_proj(x2d, pos_tables, n_pos_tiles, g, w_hi, w_lo, b, tm):
    n = x2d.shape[0]
    c, sa, sb = pos_tables
    row = lambda i: (i, 0)
    fixed = lambda i: (0, 0)
    pos_map = lambda i: (i % n_pos_tiles, 0)
    return pl.pallas_call(
        _inproj_kernel,
        grid=(n // tm,),
        in_specs=[pl.BlockSpec((tm, D_MODEL), row), pl.BlockSpec((1, D_MODEL), fixed),
                  pl.BlockSpec((D_MODEL, IN_WIDTH), fixed), pl.BlockSpec((D_MODEL, IN_WIDTH), fixed),
                  pl.BlockSpec((1, IN_WIDTH), fixed),
                  pl.BlockSpec((tm, LANES), pos_map), pl.BlockSpec((tm, LANES), pos_map),
                  pl.BlockSpec((tm, LANES), pos_map)],
        out_specs=[pl.BlockSpec((tm, ATTN_WIDTH), row), pl.BlockSpec((tm, KV_WIDTH), row),
                   pl.BlockSpec((tm, KV_WIDTH), row), pl.BlockSpec((tm, SSM_WIDTH), row)],
        out_shape=[jax.ShapeDtypeStruct((n, ATTN_WIDTH), F32), jax.ShapeDtypeStruct((n, KV_WIDTH), F32),
                   jax.ShapeDtypeStruct((n, KV_WIDTH), F32), jax.ShapeDtypeStruct((n, SSM_WIDTH), F32)],
        compiler_params=_cparams(("arbitrary",)),
        name="in_proj",
    )(x2d, g, w_hi, w_lo, b, c, sa, sb)


def _attn_kernel(sink_ref, q_ref, kp_ref, kc_ref, vp_ref, vc_ref, g_ref, o_ref, *win_refs, tq, decode, nseq):
    for b in range(nseq):
        _attn_one(sink_ref, q_ref.at[b], kp_ref.at[b], kc_ref.at[b], vp_ref.at[b], vc_ref.at[b], g_ref,
                  o_ref.at[b], *[w.at[b] for w in win_refs], tq=tq, decode=decode)


def _attn_one(sink_ref, q_ref, kp_ref, kc_ref, vp_ref, vc_ref, g_ref, o_ref, *win_refs, tq, decode):
    q = q_ref[...] * (HEAD_DIM ** -0.5)
    kp = kp_ref[...]
    kc = kc_ref[...]
    vp = vp_ref[...]
    vc = vc_ref[...]
    half = LANES // 2
    lane = lax.broadcasted_iota(I32, (1, LANES), 1)
    lo = lane < half
    tiles_per_kv = (N_HEADS // N_KV_HEADS) // 2
    rows = tiles_per_kv * tq
    row = lax.broadcasted_iota(I32, (rows, 1), 0)
    qi = lax.broadcasted_iota(I32, (rows, WINDOW), 0) % tq
    kj = lax.broadcasted_iota(I32, (rows, WINDOW), 1)
    first_off = 0 if decode else jnp.where(pl.program_id(1) > 0, 0, WINDOW)
    m_prev = kj > qi + first_off
    m_cur = kj <= qi

    def variants(t):
        r = pltpu.roll(t, half, 1)
        return [[t, r], [r, t]]

    kpv = [[a.astype(BF16) for a in row] for row in variants(kp)]
    kcv = [[a.astype(BF16) for a in row] for row in variants(kc)]
    lane_sel = [lo, jnp.logical_not(lo)]
    vpv = [[jnp.where(lane_sel[s], a, 0.0).astype(BF16) for s, a in enumerate(row)] for row in variants(vp)]
    vcv = [[jnp.where(lane_sel[s], a, 0.0).astype(BF16) for s, a in enumerate(row)] for row in variants(vc)]
    nt = functools.partial(lax.dot_general, dimension_numbers=(((1,), (1,)), ((), ())),
                           preferred_element_type=F32)
    outs = [jnp.zeros((tq, LANES), F32)] * (ATTN_WIDTH // LANES)
    for g in range(N_KV_HEADS):
        tiles = [g * tiles_per_kv + i for i in range(tiles_per_kv)]
        qg = jnp.concatenate([q[:, j * LANES:(j + 1) * LANES] for j in tiles], axis=0)
        for s in range(2):
            qm = jnp.where(lane_sel[s], qg, 0.0).astype(BF16)
            sp = jnp.where(m_prev, nt(qm, kpv[g][s]), NEG_INF)
            sc = jnp.where(m_cur, nt(qm, kcv[g][s]), NEG_INF)
            sink = sink_ref[2 * tiles[0] + s]
            for i in range(1, tiles_per_kv):
                sink = jnp.where(row >= i * tq, sink_ref[2 * tiles[i] + s], sink)
            m = jnp.maximum(jnp.max(jnp.maximum(sp, sc), axis=-1, keepdims=True), sink)
            pp = jnp.exp(sp - m)
            pc = jnp.exp(sc - m)
            den = jnp.sum(pp + pc, axis=-1, keepdims=True) + jnp.exp(sink - m)
            o = (jnp.dot(pp.astype(BF16), vpv[g][s], preferred_element_type=F32)
                 + jnp.dot(pc.astype(BF16), vcv[g][s], preferred_element_type=F32)) / den
            for i, j in enumerate(tiles):
                outs[j] = outs[j] + o[i * tq:(i + 1) * tq]
    attn = jnp.concatenate(outs, axis=1)
    o_ref[...] = _rms(attn, g_ref[...])
    if decode:
        kw_ref, vw_ref = win_refs
        kw_ref[:WINDOW - tq] = kp[tq:]
        kw_ref[WINDOW - tq:] = kc[:tq]
        vw_ref[:WINDOW - tq] = vp[tq:]
        vw_ref[WINDOW - tq:] = vc[:tq]


def _attention(sinks, q3, k_prev, k_cur, v_prev, v_cur, g, decode):
    b, t, _ = q3.shape
    tq = t if decode else WINDOW
    nb = t // tq
    nseq = math.gcd(b, DECODE_SEQS_PER_STEP) if decode else 1
    cur = lambda i, n: (i, n, 0)
    prev = (lambda i, n: (i, 0, 0)) if decode else (lambda i, n: (i, jnp.maximum(n - 1, 0), 0))
    kvb = (nseq, WINDOW, KV_WIDTH)
    out_specs = [pl.BlockSpec((nseq, tq, ATTN_WIDTH), cur)]
    out_shape = [jax.ShapeDtypeStruct((b, t, ATTN_WIDTH), F32)]
    if decode:
        out_specs += [pl.BlockSpec(kvb, cur), pl.BlockSpec(kvb, cur)]
        out_shape += [jax.ShapeDtypeStruct((b, WINDOW, KV_WIDTH), F32)] * 2
    return pl.pallas_call(
        functools.partial(_attn_kernel, tq=tq, decode=decode, nseq=nseq),
        grid=(b // nseq, nb),
        in_specs=[pl.BlockSpec(memory_space=pltpu.SMEM),
                  pl.BlockSpec((nseq, tq, ATTN_WIDTH), cur),
                  pl.BlockSpec(kvb, prev), pl.BlockSpec(kvb, cur),
                  pl.BlockSpec(kvb, prev), pl.BlockSpec(kvb, cur),
                  pl.BlockSpec((1, ATTN_WIDTH), lambda i, n: (0, 0))],
        out_specs=out_specs,
        out_shape=out_shape,
        compiler_params=_cparams(("arbitrary", "arbitrary")),
        name="attn_decode" if decode else "attn_prompt",
    )(sinks, q3, k_prev, k_cur, v_prev, v_cur, g)


def _s5_param_kernel(lr_ref, li_ref, ls_ref, abr_ref, abi_ref, zr_ref, zi_ref):
    lr = lr_ref[...]
    li = li_ref[...]
    dt = jnp.exp(ls_ref[...])
    mag = jnp.exp(lr * dt)
    ab_re = mag * jnp.cos(li * dt)
    ab_im = mag * jnp.sin(li * dt)
    den = lr * lr + li * li
    abr_ref[...] = ab_re
    abi_ref[...] = ab_im
    zr_ref[...] = ((ab_re - 1.0) * lr + ab_im * li) / den
    zi_ref[...] = (ab_im * lr - (ab_re - 1.0) * li) / den


def _s5_params(lam_re, lam_im, log_step):
    ls = jnp.broadcast_to(log_step[:, None], lam_re.shape)
    outs = pl.pallas_call(
        _s5_param_kernel,
        out_shape=[jax.ShapeDtypeStruct(lam_re.shape, F32)] * 4,
        name="s5_params",
    )(lam_re, lam_im, ls)
    return [o.reshape(1, SSM_LANES) for o in outs]


def _s5_kernel(u_ref, h0r_ref, h0i_ref, brh_ref, brl_ref, bih_ref, bil_ref, zr_ref, zi_ref,
               ar_ref, ai_ref, cr_ref, ci_ref, d_ref, wg_ref, bg_ref, g_ref,
               y_ref, hr_ref, hi_ref, sr_ref, si_ref, *, tt, ns):
    @pl.when(pl.program_id(1) == 0)
    def _():
        hr_ref[0] = h0r_ref[0]
        hi_ref[0] = h0i_ref[0]

    u = u_ref[0]
    for c in range(S5_CHUNKS):
        uc = u[:, c * LANES:(c + 1) * LANES]
        lanes = slice(c * S5_CHUNK_LANES, (c + 1) * S5_CHUNK_LANES)
        pr = _dot3(uc, brh_ref[c], brl_ref[c])
        pi = _dot3(uc, bih_ref[c], bil_ref[c])
        zr = zr_ref[:, lanes]
        zi = zi_ref[:, lanes]
        sr_ref[:, lanes] = zr * pr - zi * pi
        si_ref[:, lanes] = zr * pi + zi * pr
    ar = jnp.broadcast_to(ar_ref[...], (ns, SSM_LANES))
    ai = jnp.broadcast_to(ai_ref[...], (ns, SSM_LANES))

    if ns <= SUBLANES:
        def body(t, carry):
            hr, hi = carry
            rows = pl.ds(t * ns, ns)
            nr = ar * hr - ai * hi + sr_ref[rows, :]
            ni = ar * hi + ai * hr + si_ref[rows, :]
            sr_ref[rows, :] = nr
            si_ref[rows, :] = ni
            return nr, ni

        hr, hi = lax.fori_loop(0, tt, body, (hr_ref[0], hi_ref[0]), unroll=8)
        hr_ref[0] = hr
        hi_ref[0] = hi
    else:
        def body(t, carry):
            rows = pl.ds(pl.multiple_of(t * ns, ns), ns)
            hr = hr_ref[0]
            hi = hi_ref[0]
            nr = ar * hr - ai * hi + sr_ref[rows, :]
            ni = ar * hi + ai * hr + si_ref[rows, :]
            sr_ref[rows, :] = nr
            si_ref[rows, :] = ni
            hr_ref[0] = nr
            hi_ref[0] = ni
            return carry

        lax.fori_loop(0, tt, body, 0)

    ch = []
    for c in range(S5_CHUNKS):
        lanes = slice(c * S5_CHUNK_LANES, (c + 1) * S5_CHUNK_LANES)
        ch.append(_dot1(sr_ref[:, lanes], cr_ref[c]) - _dot1(si_ref[:, lanes], ci_ref[c]))
    y = jnp.concatenate(ch, axis=1) + d_ref[...] * u
    y = _gelu(y)
    gate = _dot1(y, wg_ref[...]) + bg_ref[...]
    y = y * (1.0 / (1.0 + jnp.exp(-gate)))
    y_ref[0] = _rms(y, g_ref[...])


def _s5(u3, h0r, h0i, consts, tt, ns):
    nb, rows, _ = u3.shape
    r = tt * ns
    fixed = lambda b, t: (0, 0)
    tile = lambda b, t: (b, t, 0)
    seq = lambda b, t: (b, 0, 0)
    const_specs = [pl.BlockSpec(c.shape, lambda b, t, nd=c.ndim: (0,) * nd) for c in consts]
    return pl.pallas_call(
        functools.partial(_s5_kernel, tt=tt, ns=ns),
        grid=(nb, rows // r),
        in_specs=[pl.BlockSpec((1, r, SSM_WIDTH), tile),
                  pl.BlockSpec((1, ns, SSM_LANES), seq), pl.BlockSpec((1, ns, SSM_LANES), seq)] + const_specs,
        out_specs=[pl.BlockSpec((1, r, SSM_WIDTH), tile),
                   pl.BlockSpec((1, ns, SSM_LANES), seq), pl.BlockSpec((1, ns, SSM_LANES), seq)],
        out_shape=[jax.ShapeDtypeStruct((nb, rows, SSM_WIDTH), F32),
                   jax.ShapeDtypeStruct((nb, ns, SSM_LANES), F32),
                   jax.ShapeDtypeStruct((nb, ns, SSM_LANES), F32)],
        scratch_shapes=[pltpu.VMEM((r, SSM_LANES), F32), pltpu.VMEM((r, SSM_LANES), F32)],
        compiler_params=_cparams(("arbitrary", "arbitrary")),
        name="s5_ns%d" % ns,
    )(u3, h0r, h0i, *consts)


def _outproj_kernel(a_ref, s_ref, x_ref, wa_ref, ws_ref, g_ref, x1_ref, xn_ref):
    x1 = x_ref[...] + _dot1(a_ref[...], wa_ref[...]) + _dot1(s_ref[...], ws_ref[...])
    x1_ref[...] = x1
    xn_ref[...] = _rms(x1, g_ref[...])


def _out_proj(attn_n, ssm_n, x2d, w_attn, w_ssm, g, tm):
    n = x2d.shape[0]
    row = lambda i: (i, 0)
    fixed = lambda i: (0, 0)
    return pl.pallas_call(
        _outproj_kernel,
        grid=(n // tm,),
        in_specs=[pl.BlockSpec((tm, ATTN_WIDTH), row), pl.BlockSpec((tm, SSM_WIDTH), row),
                  pl.BlockSpec((tm, D_MODEL), row),
                  pl.BlockSpec((ATTN_WIDTH, D_MODEL), fixed), pl.BlockSpec((SSM_WIDTH, D_MODEL), fixed),
                  pl.BlockSpec((1, D_MODEL), fixed)],
        out_specs=[pl.BlockSpec((tm, D_MODEL), row), pl.BlockSpec((tm, D_MODEL), row)],
        out_shape=[jax.ShapeDtypeStruct((n, D_MODEL), F32)] * 2,
        compiler_params=_cparams(("arbitrary",)),
        name="out_proj",
    )(attn_n, ssm_n, x2d, w_attn, w_ssm, g)


TOPK_HEADS_PER_STEP = 8


def _topk_rows(s_ref, ids, k):
    vals, picks = [], []
    sentinel = jnp.iinfo(jnp.int32).max
    for _ in range(k):
        s = s_ref[...]
        m = jnp.max(s, axis=0, keepdims=True)
        pick = jnp.min(jnp.where(s == m, ids, sentinel), axis=0, keepdims=True)
        vals.append(m)
        picks.append(pick)
        s_ref[...] = jnp.where(ids == pick, -jnp.inf, s)
    return vals, picks


def _cand_layout():
    pieces = [(0, 1, 0, PEER_TOPK)]
    pieces += [(a, a + 1, 0, SUBLANES) for a in range(1, SUBLANES)]
    pieces += [(SUBLANES, PEER_TOPK, 0, 1)]
    return pieces


N_CAND_ROWS = sum(max(a1 - a0, b1 - b0) for a0, a1, b0, b1 in _cand_layout())


def _topk_kernel(xn_ref, wqh_ref, wql_ref, keys_ref, row_ref, shift_ref, gate_ref, s_scr, c_scr, *, tm):
    xn = xn_ref[...]
    for hh in range(TOPK_HEADS_PER_STEP):
        h = pl.program_id(1) * TOPK_HEADS_PER_STEP + hh
        q = _dot3(xn, wqh_ref[h], wql_ref[h])
        for c in range(2):
            q_hi, q_lo = _split(q[:, c * PEER_HALF:(c + 1) * PEER_HALF])
            k_hi, k_lo = _split(keys_ref[2 * h + c])
            d = functools.partial(lax.dot_general, dimension_numbers=(((1,), (1,)), ((), ())),
                                  preferred_element_type=F32)
            s_scr[2 * hh + c] = d(k_hi, q_hi) + (d(k_lo, q_hi) + d(k_hi, q_lo))
    for hh in range(TOPK_HEADS_PER_STEP):
        out = slice(hh * PEER_TOPK, (hh + 1) * PEER_TOPK)
        rows, shifts, gates = _select_experts([s_scr.at[2 * hh + c] for c in range(2)], c_scr.at[hh], tm)
        row_ref[out, :] = rows
        shift_ref[out, :] = shifts
        gate_ref[out, :] = gates


def _select_experts(score_refs, cand_ref, tm):
    key_id = lax.broadcasted_iota(I32, (N_KEYS, tm), 0)
    sub_v, sub_i = [], []
    for s_ref in score_refs:
        vals, picks = _topk_rows(s_ref, key_id, PEER_TOPK)
        sub_v.append(vals)
        sub_i.append(picks)
    cs, ce, cf = [], [], []

    def rows_of(lst, lo, hi):
        return lst[lo] if hi - lo == 1 else jnp.concatenate(lst[lo:hi], axis=0)

    for a_lo, a_hi, b_lo, b_hi in _cand_layout():
        na, nbb = a_hi - a_lo, b_hi - b_lo
        rows = max(na, nbb)
        cs.append(rows_of(sub_v[0], a_lo, a_hi) + rows_of(sub_v[1], b_lo, b_hi))
        ce.append(rows_of(sub_i[0], a_lo, a_hi) * N_KEYS + rows_of(sub_i[1], b_lo, b_hi))
        r = lax.broadcasted_iota(I32, (rows, tm), 0)
        cf.append((a_lo + r) * PEER_TOPK + b_lo if na > 1 else a_lo * PEER_TOPK + b_lo + r)
    cand_ref[...] = jnp.concatenate(cs, axis=0)
    cand_e = jnp.concatenate(ce, axis=0)
    cand_f = jnp.concatenate(cf, axis=0)
    best_v, best_e = [], []
    sentinel = jnp.iinfo(jnp.int32).max
    for _ in range(PEER_TOPK):
        cand_s = cand_ref[...]
        m = jnp.max(cand_s, axis=0, keepdims=True)
        f = jnp.min(jnp.where(cand_s == m, cand_f, sentinel), axis=0, keepdims=True)
        hit = cand_f == f
        best_v.append(m)
        best_e.append(jnp.max(jnp.where(hit, cand_e, -1), axis=0, keepdims=True))
        cand_ref[...] = jnp.where(hit, -jnp.inf, cand_s)
    bv = jnp.concatenate(best_v, axis=0)
    be = jnp.concatenate(best_e, axis=0)
    ex = jnp.exp(bv - bv[0:1])
    gates = ex / jnp.sum(ex, axis=0, keepdims=True)
    rows = lax.shift_right_logical(be, 1) * SUBLANES
    shifts = ((1 - (be & 1)) * BF16_BITS).astype(F32)
    return rows, shifts, gates


def _peer_topk(xn2d, wq_hi, wq_lo, keys, tm):
    n = xn2d.shape[0]
    out = lambda i, h: (h, i)
    return pl.pallas_call(
        functools.partial(_topk_kernel, tm=tm),
        grid=(n // tm, PEER_HEADS // TOPK_HEADS_PER_STEP),
        in_specs=[pl.BlockSpec((tm, D_MODEL), lambda i, h: (i, 0)),
                  pl.BlockSpec(wq_hi.shape, lambda i, h: (0, 0, 0)),
                  pl.BlockSpec(wq_lo.shape, lambda i, h: (0, 0, 0)),
                  pl.BlockSpec(keys.shape, lambda i, h: (0, 0, 0))],
        out_specs=[pl.BlockSpec((TOPK_HEADS_PER_STEP * PEER_TOPK, tm), out)] * 3,
        out_shape=[jax.ShapeDtypeStruct((N_PAIRS, n), I32), jax.ShapeDtypeStruct((N_PAIRS, n), F32),
                   jax.ShapeDtypeStruct((N_PAIRS, n), F32)],
        scratch_shapes=[pltpu.VMEM((2 * TOPK_HEADS_PER_STEP, N_KEYS, tm), F32),
                        pltpu.VMEM((TOPK_HEADS_PER_STEP, N_CAND_ROWS, tm), F32)],
        compiler_params=_cparams(("arbitrary", "arbitrary")),
        name="peer_topk",
    )(xn2d, wq_hi, wq_lo, keys)


def _pack_kernel(t_ref, o_ref):
    even = t_ref[:, 0].astype(BF16).astype(F32)
    odd = t_ref[:, 1].astype(BF16).astype(F32)
    hi = pltpu.bitcast(odd, I32) & HI_MASK
    lo = lax.shift_right_logical(pltpu.bitcast(even, I32), BF16_BITS)
    o_ref[...] = hi | lo


def _pack_table(table, tb=256):
    e = table.shape[0]
    t4 = table.reshape(e // 2, 2, SUBLANES, LANES)
    packed = pl.pallas_call(
        _pack_kernel,
        grid=(e // 2 // tb,),
        in_specs=[pl.BlockSpec((tb, 2, SUBLANES, LANES), lambda i: (i, 0, 0, 0))],
        out_specs=pl.BlockSpec((tb, SUBLANES, LANES), lambda i: (i, 0, 0)),
        out_shape=jax.ShapeDtypeStruct((e // 2, SUBLANES, LANES), I32),
        compiler_params=_cparams(("arbitrary",)),
        name="pack_table",
    )(t4)
    return packed.reshape(e // 2 * SUBLANES, LANES)


def _tile(tbl, offset):
    return tbl[pl.ds(pl.multiple_of(offset, SUBLANES), SUBLANES), :]


def _expert_row(word, shift):
    return pltpu.bitcast((word << shift) & HI_MASK, F32)


def _row8(ref, k):
    return jnp.broadcast_to(ref[k:k + 1, :], (SUBLANES, LANES))


def _fold(p, steps):
    sub = lax.broadcasted_iota(I32, p[0].shape, 0)
    for step in steps:
        first = (sub % (2 * step)) < step
        n = len(p) // 2
        if 2 * step == SUBLANES:
            p = [jnp.where(first, p[j], p[j + n]) + pltpu.roll(jnp.where(first, p[j + n], p[j]), step, 0)
                 for j in range(n)]
        else:
            p = [jnp.where(first, p[j] + pltpu.roll(p[j], SUBLANES - step, 0),
                           p[j + n] + pltpu.roll(p[j + n], step, 0)) for j in range(n)]
    return p


def _column(block_ref, tile, tm, t):
    blk = block_ref[:, tile * tm:(tile + 1) * tm]
    lane = lax.broadcasted_iota(I32, blk.shape, 1)
    col = jnp.sum(jnp.where(lane == t, blk, 0.0), axis=1, keepdims=True)
    return jnp.broadcast_to(col, (N_PAIRS, LANES))


def _rows_copy(row_hbm, buf, sem, tile, tm):
    return pltpu.make_async_copy(row_hbm.at[:, pl.ds(tile * tm, tm)], buf, sem)


def _for_each_tile(row_hbm, bufs, sems, tm, body):
    i = pl.program_id(0)
    tps = len(bufs)
    first = i * tps
    total = pl.num_programs(0) * tps

    def copy(j, tile):
        return _rows_copy(row_hbm, bufs[j], sems.at[j], tile, tm)

    @pl.when(i == 0)
    def _():
        copy(0, 0).start()

    for j in range(tps):
        nxt = (j + 1) % tps
        if tps == 1:
            copy(0, first).wait()
            body(0, bufs[0])

            @pl.when(first + 1 < total)
            def _():
                copy(0, first + 1).start()
        else:
            @pl.when(first + j + 1 < total)
            def _():
                copy(nxt, first + j + 1).start()

            copy(j, first + j).wait()
            body(j, bufs[j])


def _load_table(tbl_hbm, tbl_vmem, sem):
    @pl.when(pl.program_id(0) == 0)
    def _():
        c = pltpu.make_async_copy(tbl_hbm, tbl_vmem, sem)
        c.start()
        c.wait()


ACT_RING = 8
EXPAND_RING = 8


def _ring_loop(tm, ring, expand, gather):
    n = len(ring)
    ahead = n // 2
    for r in range(ahead):
        expand(r, *ring[r])

    def ring_pass(j, carry):
        t0 = n * j
        for r in range(n):
            gather(t0 + r, *ring[r])
            expand(jnp.minimum(t0 + r + ahead, tm - 1), *ring[(r + ahead) % n])
        return carry

    lax.fori_loop(0, tm // n, ring_pass, 0)


def _tag_parity(w, is_odd):
    bits = (pltpu.bitcast(w, I32) & -2) | jnp.where(is_odd, 1, 0)
    return pltpu.bitcast(bits, F32)


def _tiles_per_step(n, tm):
    return 2 if (n // tm) % 2 == 0 else 1


def _peer_act_kernel(row_hbm, x_ref, shift_ref, gate_ref, tbl_hbm, w_ref, tbl, *scratch, tm, tps):
    bufs = scratch[:tps]
    sh_ring = scratch[tps:tps + ACT_RING]
    acc, sems = scratch[tps + ACT_RING:]
    _load_table(tbl_hbm, tbl, sems.at[tps])
    lane = lax.broadcasted_iota(I32, (SUBLANES, tm), 1)
    n_groups = N_PAIRS // SUBLANES

    def tile_body(tile, rows):
        def expand(t, sh):
            sh[...] = _column(shift_ref, tile, tm, t).astype(I32)

        def gather(t, sh):
            xt = x_ref[tile * tm + t]
            here = lane == t
            for g in range(n_groups):
                prods = []
                for j in range(SUBLANES):
                    k = g * SUBLANES + j
                    prods.append(_expert_row(_tile(tbl, rows.at[k][t]), _row8(sh, k)) * xt)
                for q, part in enumerate(_fold(prods, (SUBLANES // 2,))):
                    slot = g * (SUBLANES // 2) + q
                    acc[slot] = jnp.where(here, jnp.sum(part, axis=1, keepdims=True), acc[slot])

        acc[...] = jnp.zeros(acc.shape, F32)
        _ring_loop(tm, [(sh,) for sh in sh_ring], expand, gather)
        half = SUBLANES // 2
        act = jnp.concatenate([_fold([acc[g * half + q] for q in range(half)], (2, 1))[0]
                               for g in range(n_groups)], axis=0)
        cols = slice(tile * tm, (tile + 1) * tm)
        w_ref[:, cols] = _tag_parity(gate_ref[:, cols] * _gelu(act), shift_ref[:, cols] == 0.0)

    _for_each_tile(row_hbm, bufs, sems, tm, tile_body)


def _peer_act(rows, xn3, shift, gate, tbl, tm):
    n = xn3.shape[0]
    tps = _tiles_per_step(n, tm)
    step = tps * tm
    col = pl.BlockSpec((N_PAIRS, step), lambda i: (0, i))
    return pl.pallas_call(
        functools.partial(_peer_act_kernel, tm=tm, tps=tps),
        grid=(n // step,),
        in_specs=[pl.BlockSpec(memory_space=pl.ANY),
                  pl.BlockSpec((step, SUBLANES, LANES), lambda i: (i, 0, 0)),
                  col, col, pl.BlockSpec(memory_space=pl.ANY)],
        out_specs=col,
        out_shape=jax.ShapeDtypeStruct((N_PAIRS, n), F32),
        scratch_shapes=([pltpu.VMEM(tbl.shape, I32)] + [pltpu.SMEM((N_PAIRS, tm), I32)] * tps
                        + [pltpu.VMEM((N_PAIRS, LANES), I32)] * ACT_RING
                        + [pltpu.VMEM((N_PAIRS // 2, SUBLANES, tm), F32),
                           pltpu.SemaphoreType.DMA((tps + 1,))]),
        compiler_params=_cparams(("arbitrary",)),
        name="peer_act",
    )(rows, xn3, shift, gate, tbl)


def _peer_out_kernel(row_hbm, w_ref, x1_ref, tbl_hbm, o_ref, tbl, *scratch, tm, tps):
    bufs = scratch[:tps]
    sh_ring = scratch[tps:tps + EXPAND_RING]
    wb_ring = scratch[tps + EXPAND_RING:tps + 2 * EXPAND_RING]
    sems = scratch[tps + 2 * EXPAND_RING]
    _load_table(tbl_hbm, tbl, sems.at[tps])
    n_acc = 4

    def tile_body(tile, rows):
        def expand(t, sh, wb):
            col = _column(w_ref, tile, tm, t)
            wb[...] = col
            sh[...] = ((pltpu.bitcast(col, I32) & 1) ^ 1) * BF16_BITS

        def gather(t, sh, wb):
            accs = [x1_ref[tile * tm + t]] + [jnp.zeros((SUBLANES, LANES), F32)] * (n_acc - 1)
            for k in range(N_PAIRS):
                v_row = _expert_row(_tile(tbl, rows.at[k][t]), _row8(sh, k))
                accs[k % n_acc] = accs[k % n_acc] + _row8(wb, k) * v_row
            o_ref[tile * tm + t] = (accs[0] + accs[1]) + (accs[2] + accs[3])

        _ring_loop(tm, list(zip(sh_ring, wb_ring)), expand, gather)

    _for_each_tile(row_hbm, bufs, sems, tm, tile_body)


def _peer_out(rows, w, x1_3, tbl, tm):
    n = x1_3.shape[0]
    tps = _tiles_per_step(n, tm)
    step = tps * tm
    blk = pl.BlockSpec((step, SUBLANES, LANES), lambda i: (i, 0, 0))
    col = pl.BlockSpec((N_PAIRS, step), lambda i: (0, i))
    return pl.pallas_call(
        functools.partial(_peer_out_kernel, tm=tm, tps=tps),
        grid=(n // step,),
        in_specs=[pl.BlockSpec(memory_space=pl.ANY), col, blk, pl.BlockSpec(memory_space=pl.ANY)],
        out_specs=blk,
        out_shape=jax.ShapeDtypeStruct(x1_3.shape, F32),
        scratch_shapes=([pltpu.VMEM(tbl.shape, I32)] + [pltpu.SMEM((N_PAIRS, tm), I32)] * tps
                        + [pltpu.VMEM((N_PAIRS, LANES), I32)] * EXPAND_RING
                        + [pltpu.VMEM((N_PAIRS, LANES), F32)] * EXPAND_RING
                        + [pltpu.SemaphoreType.DMA((tps + 1,))]),
        compiler_params=_cparams(("arbitrary",)),
        name="peer_out",
    )(rows, w, x1_3, tbl)


def _final_kernel(x_ref, g_ref, o_ref):
    o_ref[...] = _rms(x_ref[...], g_ref[...])


def _final_norm(x2d, g, tm):
    n = x2d.shape[0]
    return pl.pallas_call(
        _final_kernel,
        grid=(n // tm,),
        in_specs=[pl.BlockSpec((tm, D_MODEL), lambda i: (i, 0)), pl.BlockSpec((1, D_MODEL), lambda i: (0, 0))],
        out_specs=pl.BlockSpec((tm, D_MODEL), lambda i: (i, 0)),
        out_shape=jax.ShapeDtypeStruct(x2d.shape, F32),
        compiler_params=_cparams(("arbitrary",)),
        name="final_norm",
    )(x2d, g)


def _block_diag_in(b):
    eye = jnp.eye(S5_CHUNK_GROUPS, dtype=F32)
    b4 = b.reshape(S5_CHUNKS, S5_CHUNK_GROUPS, SSM_STATE, SSM_GROUP)
    return jnp.einsum('cgnp,gh->cgphn', b4, eye).reshape(S5_CHUNKS, LANES, S5_CHUNK_LANES)


def _block_diag_out(c):
    eye = jnp.eye(S5_CHUNK_GROUPS, dtype=F32)
    c4 = c.reshape(S5_CHUNKS, S5_CHUNK_GROUPS, SSM_GROUP, SSM_STATE)
    return jnp.einsum('cgpn,gh->cgnhp', c4, eye).reshape(S5_CHUNKS, S5_CHUNK_LANES, LANES)


def _layer_weights(norm_mix, w_in, b_in, attn_sinks, lam_re, lam_im, log_step, b_re, b_im, c_re, c_im,
                   d_skip, w_glu, b_glu, norm_attn_out, norm_ssm_out, w_out, norm_ffn, w_query, sub_keys,
                   u_table, v_table):
    w = {}
    w['norm_mix'] = norm_mix.reshape(1, D_MODEL)
    w['w_in'] = _split(w_in)
    w['b_in'] = b_in.reshape(1, IN_WIDTH)
    w['sinks'] = attn_sinks
    ab_re, ab_im, z_re, z_im = _s5_params(lam_re, lam_im, log_step)
    w['s5'] = (list(_split(_block_diag_in(b_re))) + list(_split(_block_diag_in(b_im)))
               + [z_re, z_im, ab_re, ab_im,
                  _block_diag_out(c_re).astype(BF16), _block_diag_out(c_im).astype(BF16),
                  d_skip.reshape(1, SSM_WIDTH), w_glu.astype(BF16), b_glu.reshape(1, SSM_WIDTH),
                  norm_ssm_out.reshape(1, SSM_WIDTH)])
    w['norm_attn_out'] = norm_attn_out.reshape(1, ATTN_WIDTH)
    w['w_out_attn'] = w_out[:ATTN_WIDTH].astype(BF16)
    w['w_out_ssm'] = w_out[ATTN_WIDTH:].astype(BF16)
    w['norm_ffn'] = norm_ffn.reshape(1, D_MODEL)
    wq = w_query.reshape(D_MODEL, PEER_HEADS, PEER_QDIM).transpose(1, 0, 2)
    w['w_query'] = _split(wq)
    w['keys'] = sub_keys.reshape(PEER_HEADS * 2, N_KEYS, PEER_HALF)
    w['u_tbl'] = _pack_table(u_table)
    w['v_tbl'] = _pack_table(v_table)
    return w


def _peer(xn2d, x1_2d, w, tm_topk, tm_gather):
    n = xn2d.shape[0]
    rows, shift, gate = _peer_topk(xn2d, w['w_query'][0], w['w_query'][1], w['keys'], tm_topk)
    wts = _peer_act(rows, xn2d.reshape(n, SUBLANES, LANES), shift, gate, w['u_tbl'], tm_gather)
    out = _peer_out(rows, wts, x1_2d.reshape(n, SUBLANES, LANES), w['v_tbl'], tm_gather)
    return out.reshape(n, D_MODEL)


def _token_tile(n, cap):
    t = cap
    while n % t:
        t //= 2
    return t


def _mix_and_ffn(x2d, attn_n, ssm_n, w, norm_final):
    n = x2d.shape[0]
    x1, xn2 = _out_proj(attn_n, ssm_n, x2d, w['w_out_attn'], w['w_out_ssm'], w['norm_ffn'], _token_tile(n, 512))
    x2 = _peer(xn2, x1, w, _token_tile(n, 256), _token_tile(n, 128))
    return x2


def _prompt_layer(x, w):
    b, t, _ = x.shape
    n = b * t
    x2d = x.reshape(n, D_MODEL)
    tm = _token_tile(t, 512)
    tables = _rope_tables(jnp.arange(t, dtype=I32))
    q, k, v, u = _in_proj(x2d, tables, t // tm, w['norm_mix'], w['w_in'][0], w['w_in'][1], w['b_in'], tm)
    k3 = k.reshape(b, t, KV_WIDTH)
    v3 = v.reshape(b, t, KV_WIDTH)
    (attn_n,) = _attention(w['sinks'], q.reshape(b, t, ATTN_WIDTH), k3, k3, v3, v3, w['norm_attn_out'], False)
    h0 = jnp.zeros((b, 1, SSM_LANES), F32)
    ssm_n, h_re, h_im = _s5(u.reshape(b, t, SSM_WIDTH), h0, h0, w['s5'], _token_tile(t, 256), 1)
    x2 = _mix_and_ffn(x2d, attn_n.reshape(n, ATTN_WIDTH), ssm_n.reshape(n, SSM_WIDTH), w, None)
    k_win = k3[:, -WINDOW:].reshape(b, WINDOW, N_KV_HEADS, HEAD_DIM)
    v_win = v3[:, -WINDOW:].reshape(b, WINDOW, N_KV_HEADS, HEAD_DIM)
    st = lambda h: h.reshape(b, N_SSM_GROUPS, SSM_STATE)
    return x2.reshape(b, t, D_MODEL), k_win, v_win, st(h_re), st(h_im)


def _sample_layer(x, k_buf, v_buf, h0_re, h0_im, w, past_len):
    b, t, _ = x.shape
    n = b * t
    x2d = x.reshape(n, D_MODEL)
    tm = _token_tile(n, 512)
    pos = past_len + jnp.arange(t, dtype=I32)
    tables = tuple(jnp.tile(a, (tm // t, 1)) for a in _rope_tables(pos))
    q, k, v, u = _in_proj(x2d, tables, 1, w['norm_mix'], w['w_in'][0], w['w_in'][1], w['b_in'], tm)
    pad = lambda a: jnp.pad(a.reshape(b, t, KV_WIDTH), ((0, 0), (0, WINDOW - t), (0, 0)))
    attn_n, k_win, v_win = _attention(w['sinks'], q.reshape(b, t, ATTN_WIDTH),
                                      k_buf.reshape(b, WINDOW, KV_WIDTH), pad(k),
                                      v_buf.reshape(b, WINDOW, KV_WIDTH), pad(v), w['norm_attn_out'], True)
    ns = min(b, S5_DECODE_SEQS)
    nbk = b // ns
    u_tm = u.reshape(nbk, ns, t, SSM_WIDTH).transpose(0, 2, 1, 3).reshape(nbk, t * ns, SSM_WIDTH)
    ssm_tm, h_re, h_im = _s5(u_tm, h0_re.reshape(nbk, ns, SSM_LANES), h0_im.reshape(nbk, ns, SSM_LANES),
                             w['s5'], t, ns)
    ssm_n = ssm_tm.reshape(nbk, t, ns, SSM_WIDTH).transpose(0, 2, 1, 3).reshape(n, SSM_WIDTH)
    x2 = _mix_and_ffn(x2d, attn_n.reshape(n, ATTN_WIDTH), ssm_n, w, None)
    win = lambda a: a.reshape(b, WINDOW, N_KV_HEADS, HEAD_DIM)
    st = lambda h: h.reshape(b, N_SSM_GROUPS, SSM_STATE)
    return x2.reshape(b, t, D_MODEL), win(k_win), win(v_win), st(h_re), st(h_im)


PAST_LEN = 16384
S5_DECODE_SEQS = 64
DECODE_SEQS_PER_STEP = 8


def kernel(x_prompt, x_sample, cache_k, cache_v, state_ssm_re, state_ssm_im, norm_mix, w_in, b_in, attn_sinks, ssm_lam_re, ssm_lam_im, ssm_log_step, ssm_b_re, ssm_b_im, ssm_c_re, ssm_c_im, ssm_d, ssm_w_glu, ssm_b_glu, norm_attn_out, norm_ssm_out, w_out, norm_ffn, peer_w_query, peer_sub_keys, peer_u, peer_v, norm_final):
    depth = norm_mix.shape[0]
    xp, xs = x_prompt, x_sample
    outs = [[] for _ in range(8)]
    for l in range(depth):
        w = _layer_weights(*[a[l] for a in (norm_mix, w_in, b_in, attn_sinks, ssm_lam_re, ssm_lam_im,
                                            ssm_log_step, ssm_b_re, ssm_b_im, ssm_c_re, ssm_c_im, ssm_d,
                                            ssm_w_glu, ssm_b_glu, norm_attn_out, norm_ssm_out, w_out,
                                            norm_ffn, peer_w_query, peer_sub_keys, peer_u, peer_v)])
        xs, k2, v2, r2, i2 = _sample_layer(xs, cache_k[l], cache_v[l], state_ssm_re[l], state_ssm_im[l],
                                           w, PAST_LEN)
        xp, k1, v1, r1, i1 = _prompt_layer(xp, w)
        for lst, a in zip(outs, (k1, v1, r1, i1, k2, v2, r2, i2)):
            lst.append(a)
    g = norm_final.reshape(1, D_MODEL)
    yp = _final_norm(xp.reshape(-1, D_MODEL), g, 512).reshape(xp.shape)
    ys = _final_norm(xs.reshape(-1, D_MODEL), g, 512).reshape(xs.shape)
    return (yp, ys) + tuple(jnp.stack(o) for o in outs)
```

```python
import functools
import math

import jax
import jax.numpy as jnp
from jax import lax
from jax.experimental import pallas as pl
from jax.experimental.pallas import tpu as pltpu

F32 = jnp.float32
BF16 = jnp.bfloat16
I32 = jnp.int32

D_MODEL = 1024
N_HEADS = 8
N_KV_HEADS = 2
HEAD_DIM = 64
ATTN_WIDTH = N_HEADS * HEAD_DIM
KV_WIDTH = N_KV_HEADS * HEAD_DIM
WINDOW = 128
ROT_DIM = HEAD_DIM // 4
ROPE_THETA = 500000.0
NEG_INF = -1e30
SSM_WIDTH = D_MODEL - ATTN_WIDTH
SSM_GROUP = 16
N_SSM_GROUPS = SSM_WIDTH // SSM_GROUP
SSM_STATE = 64
SSM_LANES = N_SSM_GROUPS * SSM_STATE
IN_WIDTH = ATTN_WIDTH + 2 * KV_WIDTH + SSM_WIDTH
PEER_HEADS = 8
N_KEYS = 128
PEER_TOPK = 16
PEER_QDIM = 256
PEER_HALF = PEER_QDIM // 2
N_PAIRS = PEER_HEADS * PEER_TOPK
RMS_EPS = 1e-5

LANES = 128
SUBLANES = 8
VMEM_LIMIT = 56 * 1024 * 1024

BF16_BITS = 16
HI_MASK = -65536

S5_CHUNK_GROUPS = LANES // SSM_GROUP
S5_CHUNKS = N_SSM_GROUPS // S5_CHUNK_GROUPS
S5_CHUNK_LANES = S5_CHUNK_GROUPS * SSM_STATE


def _cparams(sem):
    return pltpu.CompilerParams(dimension_semantics=sem, vmem_limit_bytes=VMEM_LIMIT)


def _split(w):
    hi = w.astype(BF16)
    lo = (w - hi.astype(F32)).astype(BF16)
    return hi, lo


def _dot1(a, b):
    return jnp.dot(a.astype(BF16), b, preferred_element_type=F32)


def _dot3(a, b_hi, b_lo):
    a_hi, a_lo = _split(a)
    d = functools.partial(jnp.dot, preferred_element_type=F32)
    return d(a_hi, b_hi) + (d(a_lo, b_hi) + d(a_hi, b_lo))


def _gelu(x):
    return 0.5 * x * (1.0 + lax.erf(x * (2.0 ** -0.5)))


def _rms(x, g):
    return x * lax.rsqrt(jnp.mean(x * x, axis=-1, keepdims=True) + RMS_EPS) * g


def _inproj_kernel(x_ref, g_ref, whi_ref, wlo_ref, b_ref, c_ref, sa_ref, sb_ref,
                   q_ref, k_ref, v_ref, u_ref):
    xn = _rms(x_ref[...], g_ref[...])
    proj = _dot3(xn, whi_ref[...], wlo_ref[...]) + b_ref[...]
    c = c_ref[...]
    sa = sa_ref[...]
    sb = sb_ref[...]

    def rope(t):
        return t * c + pltpu.roll(t, LANES - ROT_DIM // 2, 1) * sa + pltpu.roll(t, ROT_DIM // 2, 1) * sb

    for j in range(ATTN_WIDTH // LANES):
        q_ref[:, j * LANES:(j + 1) * LANES] = rope(proj[:, j * LANES:(j + 1) * LANES])
    k_ref[...] = rope(proj[:, ATTN_WIDTH:ATTN_WIDTH + KV_WIDTH])
    v_ref[...] = proj[:, ATTN_WIDTH + KV_WIDTH:ATTN_WIDTH + 2 * KV_WIDTH]
    u_ref[...] = proj[:, ATTN_WIDTH + 2 * KV_WIDTH:]


def _rope_tables(pos):
    half = ROT_DIM // 2
    inv_freq = ROPE_THETA ** (-jnp.arange(half, dtype=F32) * 2.0 / ROT_DIM)
    ang = pos.astype(F32)[:, None] * inv_freq[None, :]
    cos = jnp.cos(ang)
    sin = jnp.sin(ang)
    t = pos.shape[0]
    one = jnp.ones((t, HEAD_DIM - ROT_DIM), F32)
    zero = jnp.zeros((t, HEAD_DIM - ROT_DIM), F32)
    zh = jnp.zeros((t, half), F32)
    c = jnp.concatenate([cos, cos, one], axis=1)
    sa = jnp.concatenate([-sin, zh, zero], axis=1)
    sb = jnp.concatenate([zh, sin, zero], axis=1)
    return tuple(jnp.tile(a, (1, LANES // HEAD_DIM)) for a in (c, sa, sb))


def _in_proj(x2d, pos_tables, n_pos_tiles, g, w_hi, w_lo, b, tm):
    n = x2d.shape[0]
    c, sa, sb = pos_tables
    row = lambda i: (i, 0)
    fixed = lambda i: (0, 0)
    pos_map = lambda i: (i % n_pos_tiles, 0)
    return pl.pallas_call(
        _inproj_kernel,
        grid=(n // tm,),
        in_specs=[pl.BlockSpec((tm, D_MODEL), row), pl.BlockSpec((1, D_MODEL), fixed),
                  pl.BlockSpec((D_MODEL, IN_WIDTH), fixed), pl.BlockSpec((D_MODEL, IN_WIDTH), fixed),
                  pl.BlockSpec((1, IN_WIDTH), fixed),
                  pl.BlockSpec((tm, LANES), pos_map), pl.BlockSpec((tm, LANES), pos_map),
                  pl.BlockSpec((tm, LANES), pos_map)],
        out_specs=[pl.BlockSpec((tm, ATTN_WIDTH), row), pl.BlockSpec((tm, KV_WIDTH), row),
                   pl.BlockSpec((tm, KV_WIDTH), row), pl.BlockSpec((tm, SSM_WIDTH), row)],
        out_shape=[jax.ShapeDtypeStruct((n, ATTN_WIDTH), F32), jax.ShapeDtypeStruct((n, KV_WIDTH), F32),
                   jax.ShapeDtypeStruct((n, KV_WIDTH), F32), jax.ShapeDtypeStruct((n, SSM_WIDTH), F32)],
        compiler_params=_cparams(("arbitrary",)),
        name="in_proj",
    )(x2d, g, w_hi, w_lo, b, c, sa, sb)


def _attn_kernel(sink_ref, q_ref, kp_ref, kc_ref, vp_ref, vc_ref, g_ref, o_ref, *win_refs, tq, decode, nseq):
    for b in range(nseq):
        _attn_one(sink_ref, q_ref.at[b], kp_ref.at[b], kc_ref.at[b], vp_ref.at[b], vc_ref.at[b], g_ref,
                  o_ref.at[b], *[w.at[b] for w in win_refs], tq=tq, decode=decode)


def _attn_one(sink_ref, q_ref, kp_ref, kc_ref, vp_ref, vc_ref, g_ref, o_ref, *win_refs, tq, decode):
    q = q_ref[...] * (HEAD_DIM ** -0.5)
    kp = kp_ref[...]
    kc = kc_ref[...]
    vp = vp_ref[...]
    vc = vc_ref[...]
    half = LANES // 2
    lane = lax.broadcasted_iota(I32, (1, LANES), 1)
    lo = lane < half
    tiles_per_kv = (N_HEADS // N_KV_HEADS) // 2
    rows = tiles_per_kv * tq
    row = lax.broadcasted_iota(I32, (rows, 1), 0)
    qi = lax.broadcasted_iota(I32, (rows, WINDOW), 0) % tq
    kj = lax.broadcasted_iota(I32, (rows, WINDOW), 1)
    first_off = 0 if decode else jnp.where(pl.program_id(1) > 0, 0, WINDOW)
    m_prev = kj > qi + first_off
    m_cur = kj <= qi

    def variants(t):
        r = pltpu.roll(t, half, 1)
        return [[t, r], [r, t]]

    kpv = [[a.astype(BF16) for a in row] for row in variants(kp)]
    kcv = [[a.astype(BF16) for a in row] for row in variants(kc)]
    lane_sel = [lo, jnp.logical_not(lo)]
    vpv = [[jnp.where(lane_sel[s], a, 0.0).astype(BF16) for s, a in enumerate(row)] for row in variants(vp)]
    vcv = [[jnp.where(lane_sel[s], a, 0.0).astype(BF16) for s, a in enumerate(row)] for row in variants(vc)]
    nt = functools.partial(lax.dot_general, dimension_numbers=(((1,), (1,)), ((), ())),
                           preferred_element_type=F32)
    outs = [jnp.zeros((tq, LANES), F32)] * (ATTN_WIDTH // LANES)
    for g in range(N_KV_HEADS):
        tiles = [g * tiles_per_kv + i for i in range(tiles_per_kv)]
        qg = jnp.concatenate([q[:, j * LANES:(j + 1) * LANES] for j in tiles], axis=0)
        for s in range(2):
            qm = jnp.where(lane_sel[s], qg, 0.0).astype(BF16)
            sp = jnp.where(m_prev, nt(qm, kpv[g][s]), NEG_INF)
            sc = jnp.where(m_cur, nt(qm, kcv[g][s]), NEG_INF)
            sink = sink_ref[2 * tiles[0] + s]
            for i in range(1, tiles_per_kv):
                sink = jnp.where(row >= i * tq, sink_ref[2 * tiles[i] + s], sink)
            m = jnp.maximum(jnp.max(jnp.maximum(sp, sc), axis=-1, keepdims=True), sink)
            pp = jnp.exp(sp - m)
            pc = jnp.exp(sc - m)
            den = jnp.sum(pp + pc, axis=-1, keepdims=True) + jnp.exp(sink - m)
            o = (jnp.dot(pp.astype(BF16), vpv[g][s], preferred_element_type=F32)
                 + jnp.dot(pc.astype(BF16), vcv[g][s], preferred_element_type=F32)) / den
            for i, j in enumerate(tiles):
                outs[j] = outs[j] + o[i * tq:(i + 1) * tq]
    attn = jnp.concatenate(outs, axis=1)
    o_ref[...] = _rms(attn, g_ref[...])
    if decode:
        kw_ref, vw_ref = win_refs
        kw_ref[:WINDOW - tq] = kp[tq:]
        kw_ref[WINDOW - tq:] = kc[:tq]
        vw_ref[:WINDOW - tq] = vp[tq:]
        vw_ref[WINDOW - tq:] = vc[:tq]


def _attention(sinks, q3, k_prev, k_cur, v_prev, v_cur, g, decode):
    b, t, _ = q3.shape
    tq = t if decode else WINDOW
    nb = t // tq
    nseq = math.gcd(b, DECODE_SEQS_PER_STEP) if decode else 1
    cur = lambda i, n: (i, n, 0)
    prev = (lambda i, n: (i, 0, 0)) if decode else (lambda i, n: (i, jnp.maximum(n - 1, 0), 0))
    kvb = (nseq, WINDOW, KV_WIDTH)
    out_specs = [pl.BlockSpec((nseq, tq, ATTN_WIDTH), cur)]
    out_shape = [jax.ShapeDtypeStruct((b, t, ATTN_WIDTH), F32)]
    if decode:
        out_specs += [pl.BlockSpec(kvb, cur), pl.BlockSpec(kvb, cur)]
        out_shape += [jax.ShapeDtypeStruct((b, WINDOW, KV_WIDTH), F32)] * 2
    return pl.pallas_call(
        functools.partial(_attn_kernel, tq=tq, decode=decode, nseq=nseq),
        grid=(b // nseq, nb),
        in_specs=[pl.BlockSpec(memory_space=pltpu.SMEM),
                  pl.BlockSpec((nseq, tq, ATTN_WIDTH), cur),
                  pl.BlockSpec(kvb, prev), pl.BlockSpec(kvb, cur),
                  pl.BlockSpec(kvb, prev), pl.BlockSpec(kvb, cur),
                  pl.BlockSpec((1, ATTN_WIDTH), lambda i, n: (0, 0))],
        out_specs=out_specs,
        out_shape=out_shape,
        compiler_params=_cparams(("arbitrary", "arbitrary")),
        name="attn_decode" if decode else "attn_prompt",
    )(sinks, q3, k_prev, k_cur, v_prev, v_cur, g)


def _s5_param_kernel(lr_ref, li_ref, ls_ref, abr_ref, abi_ref, zr_ref, zi_ref):
    lr = lr_ref[...]
    li = li_ref[...]
    dt = jnp.exp(ls_ref[...])
    mag = jnp.exp(lr * dt)
    ab_re = mag * jnp.cos(li * dt)
    ab_im = mag * jnp.sin(li * dt)
    den = lr * lr + li * li
    abr_ref[...] = ab_re
    abi_ref[...] = ab_im
    zr_ref[...] = ((ab_re - 1.0) * lr + ab_im * li) / den
    zi_ref[...] = (ab_im * lr - (ab_re - 1.0) * li) / den


def _s5_params(lam_re, lam_im, log_step):
    ls = jnp.broadcast_to(log_step[:, None], lam_re.shape)
    outs = pl.pallas_call(
        _s5_param_kernel,
        out_shape=[jax.ShapeDtypeStruct(lam_re.shape, F32)] * 4,
        name="s5_params",
    )(lam_re, lam_im, ls)
    return [o.reshape(1, SSM_LANES) for o in outs]


def _s5_kernel(u_ref, h0r_ref, h0i_ref, brh_ref, brl_ref, bih_ref, bil_ref, zr_ref, zi_ref,
               ar_ref, ai_ref, cr_ref, ci_ref, d_ref, wg_ref, bg_ref, g_ref,
               y_ref, hr_ref, hi_ref, sr_ref, si_ref, *, tt, ns):
    @pl.when(pl.program_id(1) == 0)
    def _():
        hr_ref[0] = h0r_ref[0]
        hi_ref[0] = h0i_ref[0]

    u = u_ref[0]
    for c in range(S5_CHUNKS):
        uc = u[:, c * LANES:(c + 1) * LANES]
        lanes = slice(c * S5_CHUNK_LANES, (c + 1) * S5_CHUNK_LANES)
        pr = _dot3(uc, brh_ref[c], brl_ref[c])
        pi = _dot3(uc, bih_ref[c], bil_ref[c])
        zr = zr_ref[:, lanes]
        zi = zi_ref[:, lanes]
        sr_ref[:, lanes] = zr * pr - zi * pi
        si_ref[:, lanes] = zr * pi + zi * pr
    ar = jnp.broadcast_to(ar_ref[...], (ns, SSM_LANES))
    ai = jnp.broadcast_to(ai_ref[...], (ns, SSM_LANES))

    if ns <= SUBLANES:
        def body(t, carry):
            hr, hi = carry
            rows = pl.ds(t * ns, ns)
            nr = ar * hr - ai * hi + sr_ref[rows, :]
            ni = ar * hi + ai * hr + si_ref[rows, :]
            sr_ref[rows, :] = nr
            si_ref[rows, :] = ni
            return nr, ni

        hr, hi = lax.fori_loop(0, tt, body, (hr_ref[0], hi_ref[0]), unroll=8)
        hr_ref[0] = hr
        hi_ref[0] = hi
    else:
        def body(t, carry):
            rows = pl.ds(pl.multiple_of(t * ns, ns), ns)
            hr = hr_ref[0]
            hi = hi_ref[0]
            nr = ar * hr - ai * hi + sr_ref[rows, :]
            ni = ar * hi + ai * hr + si_ref[rows, :]
            sr_ref[rows, :] = nr
            si_ref[rows, :] = ni
            hr_ref[0] = nr
            hi_ref[0] = ni
            return carry

        lax.fori_loop(0, tt, body, 0)

    ch = []
    for c in range(S5_CHUNKS):
        lanes = slice(c * S5_CHUNK_LANES, (c + 1) * S5_CHUNK_LANES)
        ch.append(_dot1(sr_ref[:, lanes], cr_ref[c]) - _dot1(si_ref[:, lanes], ci_ref[c]))
    y = jnp.concatenate(ch, axis=1) + d_ref[...] * u
    y = _gelu(y)
    gate = _dot1(y, wg_ref[...]) + bg_ref[...]
    y = y * (1.0 / (1.0 + jnp.exp(-gate)))
    y_ref[0] = _rms(y, g_ref[...])


def _s5(u3, h0r, h0i, consts, tt, ns):
    nb, rows, _ = u3.shape
    r = tt * ns
    fixed = lambda b, t: (0, 0)
    tile = lambda b, t: (b, t, 0)
    seq = lambda b, t: (b, 0, 0)
    const_specs = [pl.BlockSpec(c.shape, lambda b, t, nd=c.ndim: (0,) * nd) for c in consts]
    return pl.pallas_call(
        functools.partial(_s5_kernel, tt=tt, ns=ns),
        grid=(nb, rows // r),
        in_specs=[pl.BlockSpec((1, r, SSM_WIDTH), tile),
                  pl.BlockSpec((1, ns, SSM_LANES), seq), pl.BlockSpec((1, ns, SSM_LANES), seq)] + const_specs,
        out_specs=[pl.BlockSpec((1, r, SSM_WIDTH), tile),
                   pl.BlockSpec((1, ns, SSM_LANES), seq), pl.BlockSpec((1, ns, SSM_LANES), seq)],
        out_shape=[jax.ShapeDtypeStruct((nb, rows, SSM_WIDTH), F32),
                   jax.ShapeDtypeStruct((nb, ns, SSM_LANES), F32),
                   jax.ShapeDtypeStruct((nb, ns, SSM_LANES), F32)],
        scratch_shapes=[pltpu.VMEM((r, SSM_LANES), F32), pltpu.VMEM((r, SSM_LANES), F32)],
        compiler_params=_cparams(("arbitrary", "arbitrary")),
        name="s5_ns%d" % ns,
    )(u3, h0r, h0i, *consts)


def _outproj_kernel(a_ref, s_ref, x_ref, wa_ref, ws_ref, g_ref, x1_ref, xn_ref):
    x1 = x_ref[...] + _dot1(a_ref[...], wa_ref[...]) + _dot1(s_ref[...], ws_ref[...])
    x1_ref[...] = x1
    xn_ref[...] = _rms(x1, g_ref[...])


def _out_proj(attn_n, ssm_n, x2d, w_attn, w_ssm, g, tm):
    n = x2d.shape[0]
    row = lambda i: (i, 0)
    fixed = lambda i: (0, 0)
    return pl.pallas_call(
        _outproj_kernel,
        grid=(n // tm,),
        in_specs=[pl.BlockSpec((tm, ATTN_WIDTH), row), pl.BlockSpec((tm, SSM_WIDTH), row),
                  pl.BlockSpec((tm, D_MODEL), row),
                  pl.BlockSpec((ATTN_WIDTH, D_MODEL), fixed), pl.BlockSpec((SSM_WIDTH, D_MODEL), fixed),
                  pl.BlockSpec((1, D_MODEL), fixed)],
        out_specs=[pl.BlockSpec((tm, D_MODEL), row), pl.BlockSpec((tm, D_MODEL), row)],
        out_shape=[jax.ShapeDtypeStruct((n, D_MODEL), F32)] * 2,
        compiler_params=_cparams(("arbitrary",)),
        name="out_proj",
    )(attn_n, ssm_n, x2d, w_attn, w_ssm, g)


TOPK_HEADS_PER_STEP = 8


def _topk_rows(s_ref, ids, k):
    vals, picks = [], []
    sentinel = jnp.iinfo(jnp.int32).max
    for _ in range(k):
        s = s_ref[...]
        m = jnp.max(s, axis=0, keepdims=True)
        pick = jnp.min(jnp.where(s == m, ids, sentinel), axis=0, keepdims=True)
        vals.append(m)
        picks.append(pick)
        s_ref[...] = jnp.where(ids == pick, -jnp.inf, s)
    return vals, picks


def _cand_layout():
    pieces = [(0, 1, 0, PEER_TOPK)]
    pieces += [(a, a + 1, 0, SUBLANES) for a in range(1, SUBLANES)]
    pieces += [(SUBLANES, PEER_TOPK, 0, 1)]
    return pieces


N_CAND_ROWS = sum(max(a1 - a0, b1 - b0) for a0, a1, b0, b1 in _cand_layout())


def _topk_kernel(xn_ref, wqh_ref, wql_ref, keys_ref, row_ref, shift_ref, gate_ref, s_scr, c_scr, *, tm):
    xn = xn_ref[...]
    for hh in range(TOPK_HEADS_PER_STEP):
        h = pl.program_id(1) * TOPK_HEADS_PER_STEP + hh
        q = _dot3(xn, wqh_ref[h], wql_ref[h])
        for c in range(2):
            q_hi, q_lo = _split(q[:, c * PEER_HALF:(c + 1) * PEER_HALF])
            k_hi, k_lo = _split(keys_ref[2 * h + c])
            d = functools.partial(lax.dot_general, dimension_numbers=(((1,), (1,)), ((), ())),
                                  preferred_element_type=F32)
            s_scr[2 * hh + c] = d(k_hi, q_hi) + (d(k_lo, q_hi) + d(k_hi, q_lo))
    for hh in range(TOPK_HEADS_PER_STEP):
        out = slice(hh * PEER_TOPK, (hh + 1) * PEER_TOPK)
        rows, shifts, gates = _select_experts([s_scr.at[2 * hh + c] for c in range(2)], c_scr.at[hh], tm)
        row_ref[out, :] = rows
        shift_ref[out, :] = shifts
        gate_ref[out, :] = gates


def _select_experts(score_refs, cand_ref, tm):
    key_id = lax.broadcasted_iota(I32, (N_KEYS, tm), 0)
    sub_v, sub_i = [], []
    for s_ref in score_refs:
        vals, picks = _topk_rows(s_ref, key_id, PEER_TOPK)
        sub_v.append(vals)
        sub_i.append(picks)
    cs, ce, cf = [], [], []

    def rows_of(lst, lo, hi):
        return lst[lo] if hi - lo == 1 else jnp.concatenate(lst[lo:hi], axis=0)

    for a_lo, a_hi, b_lo, b_hi in _cand_layout():
        na, nbb = a_hi - a_lo, b_hi - b_lo
        rows = max(na, nbb)
        cs.append(rows_of(sub_v[0], a_lo, a_hi) + rows_of(sub_v[1], b_lo, b_hi))
        ce.append(rows_of(sub_i[0], a_lo, a_hi) * N_KEYS + rows_of(sub_i[1], b_lo, b_hi))
        r = lax.broadcasted_iota(I32, (rows, tm), 0)
        cf.append((a_lo + r) * PEER_TOPK + b_lo if na > 1 else a_lo * PEER_TOPK + b_lo + r)
    cand_ref[...] = jnp.concatenate(cs, axis=0)
    cand_e = jnp.concatenate(ce, axis=0)
    cand_f = jnp.concatenate(cf, axis=0)
    best_v, best_e = [], []
    sentinel = jnp.iinfo(jnp.int32).max
    for _ in range(PEER_TOPK):
        cand_s = cand_ref[...]
        m = jnp.max(cand_s, axis=0, keepdims=True)
        f = jnp.min(jnp.where(cand_s == m, cand_f, sentinel), axis=0, keepdims=True)
        hit = cand_f == f
        best_v.append(m)
        best_e.append(jnp.max(jnp.where(hit, cand_e, -1), axis=0, keepdims=True))
        cand_ref[...] = jnp.where(hit, -jnp.inf, cand_s)
    bv = jnp.concatenate(best_v, axis=0)
    be = jnp.concatenate(best_e, axis=0)
    ex = jnp.exp(bv - bv[0:1])
    gates = ex / jnp.sum(ex, axis=0, keepdims=True)
    rows = lax.shift_right_logical(be, 1) * SUBLANES
    shifts = ((1 - (be & 1)) * BF16_BITS).astype(F32)
    return rows, shifts, gates


def _peer_topk(xn2d, wq_hi, wq_lo, keys, tm):
    n = xn2d.shape[0]
    out = lambda i, h: (h, i)
    return pl.pallas_call(
        functools.partial(_topk_kernel, tm=tm),
        grid=(n // tm, PEER_HEADS // TOPK_HEADS_PER_STEP),
        in_specs=[pl.BlockSpec((tm, D_MODEL), lambda i, h: (i, 0)),
                  pl.BlockSpec(wq_hi.shape, lambda i, h: (0, 0, 0)),
                  pl.BlockSpec(wq_lo.shape, lambda i, h: (0, 0, 0)),
                  pl.BlockSpec(keys.shape, lambda i, h: (0, 0, 0))],
        out_specs=[pl.BlockSpec((TOPK_HEADS_PER_STEP * PEER_TOPK, tm), out)] * 3,
        out_shape=[jax.ShapeDtypeStruct((N_PAIRS, n), I32), jax.ShapeDtypeStruct((N_PAIRS, n), F32),
                   jax.ShapeDtypeStruct((N_PAIRS, n), F32)],
        scratch_shapes=[pltpu.VMEM((2 * TOPK_HEADS_PER_STEP, N_KEYS, tm), F32),
                        pltpu.VMEM((TOPK_HEADS_PER_STEP, N_CAND_ROWS, tm), F32)],
        compiler_params=_cparams(("arbitrary", "arbitrary")),
        name="peer_topk",
    )(xn2d, wq_hi, wq_lo, keys)


def _pack_kernel(t_ref, o_ref):
    even = t_ref[:, 0].astype(BF16).astype(F32)
    odd = t_ref[:, 1].astype(BF16).astype(F32)
    hi = pltpu.bitcast(odd, I32) & HI_MASK
    lo = lax.shift_right_logical(pltpu.bitcast(even, I32), BF16_BITS)
    o_ref[...] = hi | lo


def _pack_table(table, tb=256):
    e = table.shape[0]
    t4 = table.reshape(e // 2, 2, SUBLANES, LANES)
    packed = pl.pallas_call(
        _pack_kernel,
        grid=(e // 2 // tb,),
        in_specs=[pl.BlockSpec((tb, 2, SUBLANES, LANES), lambda i: (i, 0, 0, 0))],
        out_specs=pl.BlockSpec((tb, SUBLANES, LANES), lambda i: (i, 0, 0)),
        out_shape=jax.ShapeDtypeStruct((e // 2, SUBLANES, LANES), I32),
        compiler_params=_cparams(("arbitrary",)),
        name="pack_table",
    )(t4)
    return packed.reshape(e // 2 * SUBLANES, LANES)


def _tile(tbl, offset):
    return tbl[pl.ds(pl.multiple_of(offset, SUBLANES), SUBLANES), :]


def _expert_row(word, shift):
    return pltpu.bitcast((word << shift) & HI_MASK, F32)


def _row8(ref, k):
    return jnp.broadcast_to(ref[k:k + 1, :], (SUBLANES, LANES))


def _fold(p, steps):
    sub = lax.broadcasted_iota(I32, p[0].shape, 0)
    for step in steps:
        first = (sub % (2 * step)) < step
        n = len(p) // 2
        if 2 * step == SUBLANES:
            p = [jnp.where(first, p[j], p[j + n]) + pltpu.roll(jnp.where(first, p[j + n], p[j]), step, 0)
                 for j in range(n)]
        else:
            p = [jnp.where(first, p[j] + pltpu.roll(p[j], SUBLANES - step, 0),
                           p[j + n] + pltpu.roll(p[j + n], step, 0)) for j in range(n)]
    return p


def _column(block_ref, tile, tm, t, zero=0.0):
    blk = block_ref[:, tile * tm:(tile + 1) * tm]
    lane = lax.broadcasted_iota(I32, blk.shape, 1)
    col = jnp.sum(jnp.where(lane == t, blk, zero), axis=1, keepdims=True)
    return jnp.broadcast_to(col, (N_PAIRS, LANES))


def _zero_after(value):
    bits = lax.shift_right_logical(lax.shift_right_logical(pltpu.bitcast(value, I32), 16), 16)
    return pltpu.bitcast(bits, F32)[0:1, :]


def _rows_copy(row_hbm, buf, sem, tile, tm):
    return pltpu.make_async_copy(row_hbm.at[:, pl.ds(tile * tm, tm)], buf, sem)


def _for_each_tile(row_hbm, bufs, sems, tm, body):
    i = pl.program_id(0)
    tps = len(bufs)
    first = i * tps
    total = pl.num_programs(0) * tps

    def copy(j, tile):
        return _rows_copy(row_hbm, bufs[j], sems.at[j], tile, tm)

    @pl.when(i == 0)
    def _():
        copy(0, 0).start()

    for j in range(tps):
        nxt = (j + 1) % tps
        if tps == 1:
            copy(0, first).wait()
            body(0, bufs[0])

            @pl.when(first + 1 < total)
            def _():
                copy(0, first + 1).start()
        else:
            @pl.when(first + j + 1 < total)
            def _():
                copy(nxt, first + j + 1).start()

            copy(j, first + j).wait()
            body(j, bufs[j])


def _load_table(tbl_hbm, tbl_vmem, sem):
    @pl.when(pl.program_id(0) == 0)
    def _():
        c = pltpu.make_async_copy(tbl_hbm, tbl_vmem, sem)
        c.start()
        c.wait()


ACT_RING = 8
EXPAND_RING = 8


def _ring_loop(tm, ring, expand, gather):
    n = len(ring)
    ahead = n // 2
    for r in range(ahead):
        expand(r, *ring[r], after=None)

    def ring_pass(j, carry):
        t0 = n * j
        for r in range(n):
            mid = gather(t0 + r, *ring[r])
            expand(jnp.minimum(t0 + r + ahead, tm - 1), *ring[(r + ahead) % n], after=mid)
        return carry

    lax.fori_loop(0, tm // n, ring_pass, 0)


def _tag_parity(w, is_odd):
    bits = (pltpu.bitcast(w, I32) & -2) | jnp.where(is_odd, 1, 0)
    return pltpu.bitcast(bits, F32)


def _tiles_per_step(n, tm):
    return 2 if (n // tm) % 2 == 0 else 1


def _peer_act_kernel(row_hbm, x_ref, shift_ref, gate_ref, tbl_hbm, w_ref, tbl, *scratch, tm, tps):
    bufs = scratch[:tps]
    sh_ring = scratch[tps:tps + ACT_RING]
    acc, sems = scratch[tps + ACT_RING:]
    _load_table(tbl_hbm, tbl, sems.at[tps])
    lane = lax.broadcasted_iota(I32, (SUBLANES, tm), 1)
    n_groups = N_PAIRS // SUBLANES

    def tile_body(tile, rows):
        def expand(t, sh, after=None):
            del after
            sh[...] = _column(shift_ref, tile, tm, t).astype(I32)

        def gather(t, sh):
            xt = x_ref[tile * tm + t]
            here = lane == t
            for g in range(n_groups):
                prods = []
                for j in range(SUBLANES):
                    k = g * SUBLANES + j
                    prods.append(_expert_row(_tile(tbl, rows.at[k][t]), _row8(sh, k)) * xt)
                for q, part in enumerate(_fold(prods, (SUBLANES // 2,))):
                    slot = g * (SUBLANES // 2) + q
                    acc[slot] = jnp.where(here, jnp.sum(part, axis=1, keepdims=True), acc[slot])

        acc[...] = jnp.zeros(acc.shape, F32)
        _ring_loop(tm, [(sh,) for sh in sh_ring], expand, gather)
        half = SUBLANES // 2
        act = jnp.concatenate([_fold([acc[g * half + q] for q in range(half)], (2, 1))[0]
                               for g in range(n_groups)], axis=0)
        cols = slice(tile * tm, (tile + 1) * tm)
        w_ref[:, cols] = _tag_parity(gate_ref[:, cols] * _gelu(act), shift_ref[:, cols] == 0.0)

    _for_each_tile(row_hbm, bufs, sems, tm, tile_body)


def _peer_act(rows, xn3, shift, gate, tbl, tm):
    n = xn3.shape[0]
    tps = _tiles_per_step(n, tm)
    step = tps * tm
    col = pl.BlockSpec((N_PAIRS, step), lambda i: (0, i))
    return pl.pallas_call(
        functools.partial(_peer_act_kernel, tm=tm, tps=tps),
        grid=(n // step,),
        in_specs=[pl.BlockSpec(memory_space=pl.ANY),
                  pl.BlockSpec((step, SUBLANES, LANES), lambda i: (i, 0, 0)),
                  col, col, pl.BlockSpec(memory_space=pl.ANY)],
        out_specs=col,
        out_shape=jax.ShapeDtypeStruct((N_PAIRS, n), F32),
        scratch_shapes=([pltpu.VMEM(tbl.shape, I32)] + [pltpu.SMEM((N_PAIRS, tm), I32)] * tps
                        + [pltpu.VMEM((N_PAIRS, LANES), I32)] * ACT_RING
                        + [pltpu.VMEM((N_PAIRS // 2, SUBLANES, tm), F32),
                           pltpu.SemaphoreType.DMA((tps + 1,))]),
        compiler_params=_cparams(("arbitrary",)),
        name="peer_act",
    )(rows, xn3, shift, gate, tbl)


def _peer_out_kernel(row_hbm, w_ref, x1_ref, tbl_hbm, o_ref, tbl, *scratch, tm, tps):
    bufs = scratch[:tps]
    sh_ring = scratch[tps:tps + EXPAND_RING]
    wb_ring = scratch[tps + EXPAND_RING:tps + 2 * EXPAND_RING]
    sems = scratch[tps + 2 * EXPAND_RING]
    _load_table(tbl_hbm, tbl, sems.at[tps])
    n_acc = 4

    def tile_body(tile, rows):
        def expand(t, sh, wb, after=None):
            col = _column(w_ref, tile, tm, t, 0.0 if after is None else _zero_after(after))
            wb[...] = col
            sh[...] = ((pltpu.bitcast(col, I32) & 1) ^ 1) * BF16_BITS

        def gather(t, sh, wb):
            accs = [x1_ref[tile * tm + t]] + [jnp.zeros((SUBLANES, LANES), F32)] * (n_acc - 1)
            mid = None
            for k in range(N_PAIRS):
                v_row = _expert_row(_tile(tbl, rows.at[k][t]), _row8(sh, k))
                accs[k % n_acc] = accs[k % n_acc] + _row8(wb, k) * v_row
                if k == N_PAIRS // 2:
                    mid = accs[k % n_acc]
            o_ref[tile * tm + t] = (accs[0] + accs[1]) + (accs[2] + accs[3])
            return mid

        _ring_loop(tm, list(zip(sh_ring, wb_ring)), expand, gather)

    _for_each_tile(row_hbm, bufs, sems, tm, tile_body)


def _peer_out(rows, w, x1_3, tbl, tm):
    n = x1_3.shape[0]
    tps = _tiles_per_step(n, tm)
    step = tps * tm
    blk = pl.BlockSpec((step, SUBLANES, LANES), lambda i: (i, 0, 0))
    col = pl.BlockSpec((N_PAIRS, step), lambda i: (0, i))
    return pl.pallas_call(
        functools.partial(_peer_out_kernel, tm=tm, tps=tps),
        grid=(n // step,),
        in_specs=[pl.BlockSpec(memory_space=pl.ANY), col, blk, pl.BlockSpec(memory_space=pl.ANY)],
        out_specs=blk,
        out_shape=jax.ShapeDtypeStruct(x1_3.shape, F32),
        scratch_shapes=([pltpu.VMEM(tbl.shape, I32)] + [pltpu.SMEM((N_PAIRS, tm), I32)] * tps
                        + [pltpu.VMEM((N_PAIRS, LANES), I32)] * EXPAND_RING
                        + [pltpu.VMEM((N_PAIRS, LANES), F32)] * EXPAND_RING
                        + [pltpu.SemaphoreType.DMA((tps + 1,))]),
        compiler_params=_cparams(("arbitrary",)),
        name="peer_out",
    )(rows, w, x1_3, tbl)


def _final_kernel(x_ref, g_ref, o_ref):
    o_ref[...] = _rms(x_ref[...], g_ref[...])


def _final_norm(x2d, g, tm):
    n = x2d.shape[0]
    return pl.pallas_call(
        _final_kernel,
        grid=(n // tm,),
        in_specs=[pl.BlockSpec((tm, D_MODEL), lambda i: (i, 0)), pl.BlockSpec((1, D_MODEL), lambda i: (0, 0))],
        out_specs=pl.BlockSpec((tm, D_MODEL), lambda i: (i, 0)),
        out_shape=jax.ShapeDtypeStruct(x2d.shape, F32),
        compiler_params=_cparams(("arbitrary",)),
        name="final_norm",
    )(x2d, g)


def _block_diag_in(b):
    eye = jnp.eye(S5_CHUNK_GROUPS, dtype=F32)
    b4 = b.reshape(S5_CHUNKS, S5_CHUNK_GROUPS, SSM_STATE, SSM_GROUP)
    return jnp.einsum('cgnp,gh->cgphn', b4, eye).reshape(S5_CHUNKS, LANES, S5_CHUNK_LANES)


def _block_diag_out(c):
    eye = jnp.eye(S5_CHUNK_GROUPS, dtype=F32)
    c4 = c.reshape(S5_CHUNKS, S5_CHUNK_GROUPS, SSM_GROUP, SSM_STATE)
    return jnp.einsum('cgpn,gh->cgnhp', c4, eye).reshape(S5_CHUNKS, S5_CHUNK_LANES, LANES)


def _layer_weights(norm_mix, w_in, b_in, attn_sinks, lam_re, lam_im, log_step, b_re, b_im, c_re, c_im,
                   d_skip, w_glu, b_glu, norm_attn_out, norm_ssm_out, w_out, norm_ffn, w_query, sub_keys,
                   u_table, v_table):
    w = {}
    w['norm_mix'] = norm_mix.reshape(1, D_MODEL)
    w['w_in'] = _split(w_in)
    w['b_in'] = b_in.reshape(1, IN_WIDTH)
    w['sinks'] = attn_sinks
    ab_re, ab_im, z_re, z_im = _s5_params(lam_re, lam_im, log_step)
    w['s5'] = (list(_split(_block_diag_in(b_re))) + list(_split(_block_diag_in(b_im)))
               + [z_re, z_im, ab_re, ab_im,
                  _block_diag_out(c_re).astype(BF16), _block_diag_out(c_im).astype(BF16),
                  d_skip.reshape(1, SSM_WIDTH), w_glu.astype(BF16), b_glu.reshape(1, SSM_WIDTH),
                  norm_ssm_out.reshape(1, SSM_WIDTH)])
    w['norm_attn_out'] = norm_attn_out.reshape(1, ATTN_WIDTH)
    w['w_out_attn'] = w_out[:ATTN_WIDTH].astype(BF16)
    w['w_out_ssm'] = w_out[ATTN_WIDTH:].astype(BF16)
    w['norm_ffn'] = norm_ffn.reshape(1, D_MODEL)
    wq = w_query.reshape(D_MODEL, PEER_HEADS, PEER_QDIM).transpose(1, 0, 2)
    w['w_query'] = _split(wq)
    w['keys'] = sub_keys.reshape(PEER_HEADS * 2, N_KEYS, PEER_HALF)
    w['u_tbl'] = _pack_table(u_table)
    w['v_tbl'] = _pack_table(v_table)
    return w


def _peer(xn2d, x1_2d, w, tm_topk, tm_gather):
    n = xn2d.shape[0]
    rows, shift, gate = _peer_topk(xn2d, w['w_query'][0], w['w_query'][1], w['keys'], tm_topk)
    wts = _peer_act(rows, xn2d.reshape(n, SUBLANES, LANES), shift, gate, w['u_tbl'], tm_gather)
    out = _peer_out(rows, wts, x1_2d.reshape(n, SUBLANES, LANES), w['v_tbl'], tm_gather)
    return out.reshape(n, D_MODEL)


def _token_tile(n, cap):
    t = cap
    while n % t:
        t //= 2
    return t


def _mix_and_ffn(x2d, attn_n, ssm_n, w, norm_final):
    n = x2d.shape[0]
    x1, xn2 = _out_proj(attn_n, ssm_n, x2d, w['w_out_attn'], w['w_out_ssm'], w['norm_ffn'], _token_tile(n, 512))
    x2 = _peer(xn2, x1, w, _token_tile(n, 256), _token_tile(n, 128))
    return x2


def _prompt_layer(x, w):
    b, t, _ = x.shape
    n = b * t
    x2d = x.reshape(n, D_MODEL)
    tm = _token_tile(t, 512)
    tables = _rope_tables(jnp.arange(t, dtype=I32))
    q, k, v, u = _in_proj(x2d, tables, t // tm, w['norm_mix'], w['w_in'][0], w['w_in'][1], w['b_in'], tm)
    k3 = k.reshape(b, t, KV_WIDTH)
    v3 = v.reshape(b, t, KV_WIDTH)
    (attn_n,) = _attention(w['sinks'], q.reshape(b, t, ATTN_WIDTH), k3, k3, v3, v3, w['norm_attn_out'], False)
    h0 = jnp.zeros((b, 1, SSM_LANES), F32)
    ssm_n, h_re, h_im = _s5(u.reshape(b, t, SSM_WIDTH), h0, h0, w['s5'], _token_tile(t, 256), 1)
    x2 = _mix_and_ffn(x2d, attn_n.reshape(n, ATTN_WIDTH), ssm_n.reshape(n, SSM_WIDTH), w, None)
    k_win = k3[:, -WINDOW:].reshape(b, WINDOW, N_KV_HEADS, HEAD_DIM)
    v_win = v3[:, -WINDOW:].reshape(b, WINDOW, N_KV_HEADS, HEAD_DIM)
    st = lambda h: h.reshape(b, N_SSM_GROUPS, SSM_STATE)
    return x2.reshape(b, t, D_MODEL), k_win, v_win, st(h_re), st(h_im)


def _sample_layer(x, k_buf, v_buf, h0_re, h0_im, w, past_len):
    b, t, _ = x.shape
    n = b * t
    x2d = x.reshape(n, D_MODEL)
    tm = _token_tile(n, 512)
    pos = past_len + jnp.arange(t, dtype=I32)
    tables = tuple(jnp.tile(a, (tm // t, 1)) for a in _rope_tables(pos))
    q, k, v, u = _in_proj(x2d, tables, 1, w['norm_mix'], w['w_in'][0], w['w_in'][1], w['b_in'], tm)
    pad = lambda a: jnp.pad(a.reshape(b, t, KV_WIDTH), ((0, 0), (0, WINDOW - t), (0, 0)))
    attn_n, k_win, v_win = _attention(w['sinks'], q.reshape(b, t, ATTN_WIDTH),
                                      k_buf.reshape(b, WINDOW, KV_WIDTH), pad(k),
                                      v_buf.reshape(b, WINDOW, KV_WIDTH), pad(v), w['norm_attn_out'], True)
    ns = min(b, S5_DECODE_SEQS)
    nbk = b // ns
    u_tm = u.reshape(nbk, ns, t, SSM_WIDTH).transpose(0, 2, 1, 3).reshape(nbk, t * ns, SSM_WIDTH)
    ssm_tm, h_re, h_im = _s5(u_tm, h0_re.reshape(nbk, ns, SSM_LANES), h0_im.reshape(nbk, ns, SSM_LANES),
                             w['s5'], t, ns)
    ssm_n = ssm_tm.reshape(nbk, t, ns, SSM_WIDTH).transpose(0, 2, 1, 3).reshape(n, SSM_WIDTH)
    x2 = _mix_and_ffn(x2d, attn_n.reshape(n, ATTN_WIDTH), ssm_n, w, None)
    win = lambda a: a.reshape(b, WINDOW, N_KV_HEADS, HEAD_DIM)
    st = lambda h: h.reshape(b, N_SSM_GROUPS, SSM_STATE)
    return x2.reshape(b, t, D_MODEL), win(k_win), win(v_win), st(h_re), st(h_im)


PAST_LEN = 16384
S5_DECODE_SEQS = 64
DECODE_SEQS_PER_STEP = 8


def kernel(x_prompt, x_sample, cache_k, cache_v, state_ssm_re, state_ssm_im, norm_mix, w_in, b_in, attn_sinks, ssm_lam_re, ssm_lam_im, ssm_log_step, ssm_b_re, ssm_b_im, ssm_c_re, ssm_c_im, ssm_d, ssm_w_glu, ssm_b_glu, norm_attn_out, norm_ssm_out, w_out, norm_ffn, peer_w_query, peer_sub_keys, peer_u, peer_v, norm_final):
    depth = norm_mix.shape[0]
    xp, xs = x_prompt, x_sample
    outs = [[] for _ in range(8)]
    for l in range(depth):
        w = _layer_weights(*[a[l] for a in (norm_mix, w_in, b_in, attn_sinks, ssm_lam_re, ssm_lam_im,
                                            ssm_log_step, ssm_b_re, ssm_b_im, ssm_c_re, ssm_c_im, ssm_d,
                                            ssm_w_glu, ssm_b_glu, norm_attn_out, norm_ssm_out, w_out,
                                            norm_ffn, peer_w_query, peer_sub_keys, peer_u, peer_v)])
        xs, k2, v2, r2, i2 = _sample_layer(xs, cache_k[l], cache_v[l], state_ssm_re[l], state_ssm_im[l],
                                           w, PAST_LEN)
        xp, k1, v1, r1, i1 = _prompt_layer(xp, w)
        for lst, a in zip(outs, (k1, v1, r1, i1, k2, v2, r2, i2)):
            lst.append(a)
    g = norm_final.reshape(1, D_MODEL)
    yp = _final_norm(xp.reshape(-1, D_MODEL), g, 512).reshape(xp.shape)
    ys = _final_norm(xs.reshape(-1, D_MODEL), g, 512).reshape(xs.shape)
    return (yp, ys) + tuple(jnp.stack(o) for o in outs)
```
